```python
import numpy as np
import jax
import jax.numpy as jnp
from jax import lax

D_MODEL = 2048
BATCH = 2
SEQ = 8192
DEPTH = 2

GRID_W = 64
CTX_LEN = 256
N_MIXERS = 4
D_GROUP = D_MODEL // N_MIXERS
HEAD_DIM = 128
GLA_HEADS = D_GROUP // HEAD_DIM
GLA_LOWRANK = 16
GLA_GATE_NORM = 16.0
GLA_CHUNK = 64
MLSTM_HEADS = D_GROUP // HEAD_DIM
MLSTM_CHUNK = 64
CM_GROUPS = D_GROUP // HEAD_DIM
CM_CHUNK = 128
CONV_W = 3
D_FF = 5632
EPS = 1e-6

GLA_COLS = 4 * D_GROUP + 2 * GLA_LOWRANK
MLSTM_COLS = 4 * D_GROUP + 4 * MLSTM_HEADS
SC_COLS = 3 * D_GROUP
CM_COLS = 2 * D_GROUP
IN_COLS = GLA_COLS + MLSTM_COLS + SC_COLS + CM_COLS

kernel_name = 'hybrid_parallel_gla_mlstm_conv_chunkmlp_adaln'


def rmsnorm(x, g):
    xf = x.astype(jnp.float32)
    y = xf * lax.rsqrt(jnp.mean(xf * xf, axis=-1, keepdims=True) + EPS)
    return (y * g.astype(jnp.float32)).astype(x.dtype)


def head_rmsnorm(x, g, n_heads):
    b, l, w = x.shape
    xh = x.reshape(b, l, n_heads, w // n_heads)
    return rmsnorm(xh, g.reshape(n_heads, w // n_heads)).reshape(b, l, w)


def split_cols(p, sizes):
    idx = [int(i) for i in np.cumsum(sizes)[:-1]]
    return jnp.split(p, idx, axis=-1)


def to_heads(t, n_heads):
    b, l, w = t.shape
    return t.astype(jnp.float32).reshape(b, l, n_heads, w // n_heads).transpose(0, 2, 1, 3)


def from_heads(t):
    b, h, l, d = t.shape
    return t.transpose(0, 2, 1, 3).reshape(b, l, h * d)


def dwconv3(x, w, on_grid):
    b, l, ch = x.shape
    if on_grid:
        rows = l // GRID_W
        x = x.reshape(b, rows, GRID_W, ch)
    ax = x.ndim - 2
    n = x.shape[ax]
    pad = [(0, 0)] * x.ndim
    pad[ax] = (CONV_W // 2, CONV_W // 2)
    xp = jnp.pad(x, pad)
    y = sum(w[j] * lax.slice_in_dim(xp, j, j + n, axis=ax) for j in range(CONV_W))
    return y.reshape(b, l, ch)


def _chunks(t, size):
    b, h, l = t.shape[:3]
    return jnp.moveaxis(t.reshape(b, h, l // size, size, *t.shape[3:]), 2, 0)


def _unchunks(t):
    t = jnp.moveaxis(t, 0, 2)
    return t.reshape(t.shape[0], t.shape[1], -1, t.shape[-1])


def gla_scan(q, k, v, log_a, state):
    size = GLA_CHUNK
    xs = (_chunks(q, size), _chunks(k, size), _chunks(v, size), _chunks(log_a, size))
    mask = jnp.tril(jnp.ones((size, size), dtype=bool))

    def step(s_mat, inp):
        qc, kc, vc, ac = inp
        b_cum = jnp.cumsum(ac, axis=-2)
        b_last = b_cum[..., -1:, :]
        q_in = qc * jnp.exp(b_cum)
        q_rel = qc * jnp.exp(b_cum - b_last)
        k_rel = kc * jnp.exp(b_last - b_cum)
        att = jnp.where(mask, jnp.einsum('bhtd,bhsd->bhts', q_rel, k_rel), 0.0)
        o = jnp.einsum('bhtd,bhdv->bhtv', q_in, s_mat) + jnp.einsum('bhts,bhsv->bhtv', att, vc)
        s_new = jnp.swapaxes(jnp.exp(b_last), -1, -2) * s_mat + jnp.einsum('bhsd,bhsv->bhdv', k_rel, vc)
        return s_new, o

    state, o = lax.scan(step, state, xs)
    return _unchunks(o), state


def mlstm_scan(q, k, v, log_i, log_f, state):
    size = MLSTM_CHUNK
    xs = (_chunks(q, size), _chunks(k, size), _chunks(v, size), _chunks(log_i, size), _chunks(log_f, size))
    mask = jnp.tril(jnp.ones((size, size), dtype=bool))

    def step(carry, inp):
        c_mat, n_vec, m = carry
        qc, kc, vc, ic, fc = inp
        f_cum = jnp.cumsum(fc, axis=-1)
        d_log = jnp.where(mask, f_cum[..., :, None] - f_cum[..., None, :] + ic[..., None, :], -jnp.inf)
        inter = f_cum + m[..., None]
        m_t = jnp.maximum(inter, jnp.max(d_log, axis=-1))
        w_inter = jnp.exp(inter - m_t)
        s = jnp.einsum('bhtd,bhsd->bhts', qc, kc) * jnp.exp(d_log - m_t[..., None])
        num = w_inter[..., None] * jnp.einsum('bhtd,bhdv->bhtv', qc, c_mat) + jnp.einsum('bhts,bhsv->bhtv', s, vc)
        den = w_inter * jnp.einsum('bhtd,bhd->bht', qc, n_vec) + jnp.sum(s, axis=-1)
        h = num / jnp.maximum(jnp.abs(den), jnp.exp(-m_t))[..., None]
        m_new = m_t[..., -1]
        decay = jnp.exp(f_cum[..., -1] + m - m_new)
        w_k = jnp.exp(f_cum[..., -1:] - f_cum + ic - m_new[..., None])
        c_new = decay[..., None, None] * c_mat + jnp.einsum('bhs,bhsd,bhsv->bhdv', w_k, kc, vc)
        n_new = decay[..., None] * n_vec + jnp.einsum('bhs,bhsd->bhd', w_k, kc)
        return (c_new, n_new, m_new), h

    state, h = lax.scan(step, state, xs)
    return _unchunks(h), state


def bidir_scan(scan_fn, ctx_fwd, lat_fwd, ctx_bwd, lat_bwd, init):
    flip = lambda t: jnp.flip(t, axis=2)
    oc_f, s_f = scan_fn(*ctx_fwd, init)
    ol_f, _ = scan_fn(*lat_fwd, s_f)
    oc_b, s_b = scan_fn(*[flip(t) for t in ctx_bwd], init)
    ol_b, _ = scan_fn(*[flip(t) for t in lat_bwd], s_b)
    return oc_f + flip(oc_b), ol_f + flip(ol_b)


def gla_prep(p, w_a2, b_a):
    q, k, v, g, z = split_cols(p, [D_GROUP] * 4 + [2 * GLA_LOWRANK])
    b, l, _ = z.shape
    z = z.astype(jnp.float32).reshape(b, l, 2, GLA_LOWRANK)
    log_a = jax.nn.log_sigmoid(jnp.einsum('bldr,drk->bldk', z, w_a2.astype(jnp.float32))
                               + b_a.astype(jnp.float32)) / GLA_GATE_NORM
    q = to_heads(q, GLA_HEADS) * HEAD_DIM ** -0.5
    k = to_heads(k, GLA_HEADS)
    v = to_heads(v, GLA_HEADS)
    fwd = (q, k, v, to_heads(log_a[:, :, 0], GLA_HEADS))
    bwd = (q, k, v, to_heads(log_a[:, :, 1], GLA_HEADS))
    return fwd, bwd, g


def gla_mixer(pc, pl, w_a2, b_a, g_norm, need_ctx):
    cf, cb, gc = gla_prep(pc, w_a2, b_a)
    lf, lb, gl = gla_prep(pl, w_a2, b_a)
    init = jnp.zeros(cf[0].shape[:2] + (HEAD_DIM, HEAD_DIM), jnp.float32)
    oc, ol = bidir_scan(gla_scan, cf, lf, cb, lb, init)

    def out(o, g):
        return (head_rmsnorm(from_heads(o), g_norm, GLA_HEADS) * jax.nn.silu(g.astype(jnp.float32))).astype(g.dtype)

    return (out(oc, gc) if need_ctx else None), out(ol, gl)


def mlstm_prep(p, conv_w, gate_b, on_grid):
    qk, v, o, gates = split_cols(p, [2 * D_GROUP, D_GROUP, D_GROUP, 4 * MLSTM_HEADS])
    q, k = jnp.split(dwconv3(qk, conv_w, on_grid), 2, axis=-1)
    q = to_heads(q, MLSTM_HEADS)
    k = to_heads(k, MLSTM_HEADS) * HEAD_DIM ** -0.5
    v = to_heads(v, MLSTM_HEADS)
    b, l, _ = gates.shape
    gates = gates.astype(jnp.float32).reshape(b, l, 2, 2, MLSTM_HEADS) + gate_b.astype(jnp.float32)
    gates = gates.transpose(2, 3, 0, 4, 1)
    log_i = gates[:, 0]
    log_f = jax.nn.log_sigmoid(gates[:, 1])
    return (q, k, v, log_i[0], log_f[0]), (q, k, v, log_i[1], log_f[1]), o


def mlstm_mixer(pc, pl, conv_w, gate_b, g_norm, need_ctx):
    cf, cb, oc_gate = mlstm_prep(pc, conv_w, gate_b, False)
    lf, lb, ol_gate = mlstm_prep(pl, conv_w, gate_b, True)
    bh = cf[0].shape[:2]
    init = (jnp.zeros(bh + (HEAD_DIM, HEAD_DIM), jnp.float32),
            jnp.zeros(bh + (HEAD_DIM,), jnp.float32),
            jnp.zeros(bh, jnp.float32))
    hc, hl = bidir_scan(mlstm_scan, cf, lf, cb, lb, init)

    def out(h, o):
        return (head_rmsnorm(from_heads(h), g_norm, MLSTM_HEADS) * jax.nn.sigmoid(o.astype(jnp.float32))).astype(o.dtype)

    return (out(hc, oc_gate) if need_ctx else None), out(hl, ol_gate)


def short_conv(p, w, on_grid):
    bg, cg, h = split_cols(p, [D_GROUP] * 3)
    return bg * dwconv3(cg * h, w, on_grid)


def chunk_mlp(p, w_s, b_s, g_v):
    u, v = jnp.split(p, 2, axis=-1)
    u = jax.nn.gelu(u)
    v = rmsnorm(jax.nn.gelu(v), g_v)
    b, l, _ = v.shape
    vc = v.reshape(b, l // CM_CHUNK, CM_CHUNK, CM_GROUPS, D_GROUP // CM_GROUPS)
    sv = jnp.einsum('gts,bnsgc->bntgc', w_s, vc) + b_s.T[:, :, None]
    return u * sv.reshape(b, l, D_GROUP)


def conv_ffn(h, w_up, w_conv, w_down, on_grid):
    a, v = jnp.split(h @ w_up, 2, axis=-1)
    return (jax.nn.silu(dwconv3(a, w_conv, on_grid)) * v) @ w_down


def modulation(cond, w, b):
    m = jax.nn.silu(cond) @ w + b
    m = m.reshape(m.shape[:-1] + (1, m.shape[-1]))
    return jnp.split(m, 6, axis=-1)


def setup_inputs(seed: int = 0) -> dict:
    key = jax.random.key(seed)
    ks = jax.random.split(key, 26)
    nrm = lambda k, shape, scale: jax.random.normal(k, shape, jnp.float32) * scale
    base_if = jnp.stack([jnp.zeros((MLSTM_HEADS,), jnp.float32),
                         jnp.linspace(3.0, 6.0, MLSTM_HEADS, dtype=jnp.float32)])
    return {
        'x': nrm(ks[0], (BATCH, SEQ, D_MODEL), 1.0),
        'c': nrm(ks[1], (BATCH, D_MODEL), 1.0),
        'ctx': nrm(ks[2], (BATCH, CTX_LEN, D_MODEL), 1.0),
        'c_ctx': nrm(ks[3], (D_MODEL,), 1.0),
        'w_mod': nrm(ks[4], (DEPTH, D_MODEL, 6 * D_MODEL), 0.5 * D_MODEL ** -0.5),
        'b_mod': nrm(ks[5], (DEPTH, 6 * D_MODEL), 0.01),
        'norm1': 1.0 + nrm(ks[6], (DEPTH, D_MODEL), 0.1),
        'norm2': 1.0 + nrm(ks[7], (DEPTH, D_MODEL), 0.1),
        'w_in': nrm(ks[8], (DEPTH, D_MODEL, IN_COLS), D_MODEL ** -0.5),
        'w_out': nrm(ks[9], (DEPTH, D_MODEL, D_MODEL), D_MODEL ** -0.5),
        'gla_wa2': nrm(ks[10], (DEPTH, 2, GLA_LOWRANK, D_GROUP), GLA_LOWRANK ** -0.5),
        'gla_ba': nrm(ks[11], (DEPTH, 2, D_GROUP), 0.1),
        'gla_norm': 1.0 + nrm(ks[12], (DEPTH, D_GROUP), 0.1),
        'mlstm_conv': nrm(ks[13], (DEPTH, CONV_W, 2 * D_GROUP), CONV_W ** -0.5),
        'mlstm_gate_b': base_if + nrm(ks[14], (DEPTH, 2, 2, MLSTM_HEADS), 0.1),
        'mlstm_norm': 1.0 + nrm(ks[15], (DEPTH, D_GROUP), 0.1),
        'sc_conv': nrm(ks[16], (DEPTH, CONV_W, D_GROUP), CONV_W ** -0.5),
        'cm_ws': nrm(ks[17], (DEPTH, CM_GROUPS, CM_CHUNK, CM_CHUNK), CM_CHUNK ** -0.5),
        'cm_bs': 1.0 + nrm(ks[18], (DEPTH, CM_GROUPS, CM_CHUNK), 0.1),
        'cm_norm': 1.0 + nrm(ks[19], (DEPTH, D_GROUP), 0.1),
        'ffn_up': nrm(ks[20], (DEPTH, D_MODEL, 2 * D_FF), D_MODEL ** -0.5),
        'ffn_conv': nrm(ks[21], (DEPTH, CONV_W, D_FF), CONV_W ** -0.5),
        'ffn_down': nrm(ks[22], (DEPTH, D_FF, D_MODEL), D_FF ** -0.5),
        'final_norm': 1.0 + nrm(ks[23], (D_MODEL,), 0.1),
    }


def reference(x, c, ctx, c_ctx, w_mod, b_mod, norm1, norm2, w_in, w_out, gla_wa2, gla_ba, gla_norm,
              mlstm_conv, mlstm_gate_b, mlstm_norm, sc_conv, cm_ws, cm_bs, cm_norm,
              ffn_up, ffn_conv, ffn_down, final_norm):
    col_sizes = [GLA_COLS, MLSTM_COLS, SC_COLS, CM_COLS]
    for l in range(DEPTH):
        need_ctx = l < DEPTH - 1
        sh1, sc1, g1, sh2, sc2, g2 = modulation(c, w_mod[l], b_mod[l])
        csh1, csc1, cg1, csh2, csc2, cg2 = modulation(c_ctx, w_mod[l], b_mod[l])
        hl = rmsnorm(x, norm1[l]) * (1.0 + sc1) + sh1
        hc = rmsnorm(ctx, norm1[l]) * (1.0 + csc1) + csh1
        pl_gla, pl_ml, pl_sc, pl_cm = split_cols(hl @ w_in[l], col_sizes)
        pc_gla, pc_ml, pc_sc, pc_cm = split_cols(hc @ w_in[l], col_sizes)
        yc_gla, yl_gla = gla_mixer(pc_gla, pl_gla, gla_wa2[l], gla_ba[l], gla_norm[l], need_ctx)
        yc_ml, yl_ml = mlstm_mixer(pc_ml, pl_ml, mlstm_conv[l], mlstm_gate_b[l], mlstm_norm[l], need_ctx)
        yl = jnp.concatenate([yl_gla, yl_ml,
                              short_conv(pl_sc, sc_conv[l], True),
                              chunk_mlp(pl_cm, cm_ws[l], cm_bs[l], cm_norm[l])], axis=-1) @ w_out[l]
        x = x + g1 * yl
        hl2 = rmsnorm(x, norm2[l]) * (1.0 + sc2) + sh2
        x = x + g2 * conv_ffn(hl2, ffn_up[l], ffn_conv[l], ffn_down[l], True)
        if need_ctx:
            yc = jnp.concatenate([yc_gla, yc_ml,
                                  short_conv(pc_sc, sc_conv[l], False),
                                  chunk_mlp(pc_cm, cm_ws[l], cm_bs[l], cm_norm[l])], axis=-1) @ w_out[l]
            ctx = ctx + cg1 * yc
            hc2 = rmsnorm(ctx, norm2[l]) * (1.0 + csc2) + csh2
            ctx = ctx + cg2 * conv_ffn(hc2, ffn_up[l], ffn_conv[l], ffn_down[l], False)
    return rmsnorm(x, final_norm)
```

```python
import functools

import jax
import jax.numpy as jnp
from jax import lax
from jax.experimental import pallas as pl
from jax.experimental.pallas import tpu as pltpu

F32 = jnp.float32
BF16 = jnp.bfloat16

GRID_W = 64
N_MIXERS = 4
HEAD_DIM = 128
N_HEADS = 4
D_GROUP = N_HEADS * HEAD_DIM
GLA_LOWRANK = 16
GLA_GATE_NORM = 16.0
CHUNK = 64
CM_CHUNK = 128
EPS = 1e-6
LANES = 128
SMALL_COLS = LANES
Z_FWD, Z_BWD, GATE0 = 0, GLA_LOWRANK, 2 * GLA_LOWRANK
VMEM_LIMIT = 56 * 1024 * 1024

(C_GQ, C_GK, C_GV, C_GG, C_MQK, C_MV, C_MO, C_SB, C_SC, C_SH, C_CU, C_CV) = (
    0, 1, 2, 3, 4, 6, 7, 8, 9, 10, 11, 12)
N_WIDE_BLOCKS = 13
HIGHEST = lax.Precision.HIGHEST


def _params(sem):
    return pltpu.CompilerParams(dimension_semantics=sem, vmem_limit_bytes=VMEM_LIMIT)


def _sigmoid(x):
    return 1.0 / (1.0 + jnp.exp(-x))


def _log_sigmoid(x):
    return jnp.minimum(x, 0.0) - jnp.log1p(jnp.exp(-jnp.abs(x)))


def _gelu_tanh(x):
    return x * (0.5 * (1.0 + jnp.tanh(0.7978845608028654 * (x + 0.044715 * (x * x * x)))))


def _shift_rows(x, period):
    n = x.shape[0]
    row = lax.broadcasted_iota(jnp.int32, (n, 1), 0) % period
    prev = jnp.where(row == 0, 0.0, pltpu.roll(x, 1, axis=0))
    nxt = jnp.where(row == period - 1, 0.0, pltpu.roll(x, n - 1, axis=0))
    return prev, nxt


def _dwconv3(x, w_ref, period):
    prev, nxt = _shift_rows(x, period)
    return w_ref[0:1, :] * prev + w_ref[1:2, :] * x + w_ref[2:3, :] * nxt


def _mod_kernel(c_ref, w_ref, b_ref, o_ref):
    s = c_ref[...]
    s = (s * _sigmoid(s)).astype(BF16)
    o_ref[...] = jnp.dot(s, w_ref[...].astype(BF16), preferred_element_type=F32) + b_ref[...]


def _modulation(cond, w_mod, b_mod, tn=1024):
    depth, d, n = w_mod.shape
    rows = cond.shape[0]
    return pl.pallas_call(
        _mod_kernel,
        grid=(depth, n // tn),
        in_specs=[
            pl.BlockSpec((rows, d), lambda l, j: (0, 0)),
            pl.BlockSpec((None, d, tn), lambda l, j: (l, 0, j)),
            pl.BlockSpec((None, 1, tn), lambda l, j: (l, 0, j)),
        ],
        out_specs=pl.BlockSpec((None, rows, tn), lambda l, j: (l, 0, j)),
        out_shape=jax.ShapeDtypeStruct((depth, rows, n), F32),
        compiler_params=_params(("parallel", "parallel")),
        name="modulation",
    )(cond, w_mod, b_mod.reshape(depth, 1, n))


def _modulated_norm(x, g, sc, sh):
    r = lax.rsqrt(jnp.mean(x * x, axis=-1, keepdims=True) + EPS)
    return (x * r * g) * (1.0 + sc) + sh


def _inproj_kernel(x_ref, g_ref, sh_ref, sc_ref, w_ref, ws_ref, p_ref, ps_ref, hn_ref):
    @pl.when(pl.program_id(1) == 0)
    def _():
        hn = _modulated_norm(x_ref[...], g_ref[...], sc_ref[...], sh_ref[...]).astype(BF16)
        hn_ref[...] = hn
        ps_ref[...] = jnp.dot(hn, ws_ref[...], preferred_element_type=F32)

    p_ref[...] = jnp.dot(hn_ref[...], w_ref[...], preferred_element_type=F32)


def _row_spec(piece, seq_len, tm, d):
    return pl.BlockSpec((None, None, 1, d), lambda i, *_: (piece, (i * tm) // seq_len, 0, 0))


def _in_projection(x2, gain, mod, w_wide, w_small, seq_len, tm, tn=D_GROUP):
    m, d = x2.shape
    n = w_wide.shape[1]
    return pl.pallas_call(
        _inproj_kernel,
        grid=(m // tm, n // tn),
        in_specs=[
            pl.BlockSpec((tm, d), lambda i, j: (i, 0)),
            pl.BlockSpec((1, d), lambda i, j: (0, 0)),
            _row_spec(0, seq_len, tm, d),
            _row_spec(1, seq_len, tm, d),
            pl.BlockSpec((d, tn), lambda i, j: (0, j)),
            pl.BlockSpec((d, SMALL_COLS), lambda i, j: (0, 0)),
        ],
        out_specs=[
            pl.BlockSpec((tm, tn), lambda i, j: (i, j)),
            pl.BlockSpec((tm, SMALL_COLS), lambda i, j: (i, 0)),
        ],
        out_shape=[
            jax.ShapeDtypeStruct((m, n), F32),
            jax.ShapeDtypeStruct((m, SMALL_COLS), F32),
        ],
        scratch_shapes=[pltpu.VMEM((tm, d), BF16)],
        compiler_params=_params(("parallel", "arbitrary")),
        name="in_projection",
    )(x2, gain.reshape(1, d), mod, mod, w_wide, w_small)


def _tri(reverse):
    r = lax.broadcasted_iota(jnp.int32, (CHUNK, CHUNK), 0)
    c = lax.broadcasted_iota(jnp.int32, (CHUNK, CHUNK), 1)
    return (c >= r) if reverse else (c <= r)


def _head_norm_gate(o, gain, gate):
    r = lax.rsqrt(jnp.mean(o * o, axis=-1, keepdims=True) + EPS)
    return o * r * gain * gate


def _gla_kernel(*refs, reverse, final, n_chunks):
    if final:
        (q_ref, k_ref, v_ref, z_ref, wa_ref, ba_ref, s0_ref, of_ref, g_ref, gn_ref,
         out_ref, sfin_ref, st_ref) = refs
    else:
        (q_ref, k_ref, v_ref, z_ref, wa_ref, ba_ref, s0_ref,
         out_ref, sfin_ref, st_ref) = refs
    step = pl.program_id(1)

    @pl.when(step == 0)
    def _():
        st_ref[...] = s0_ref[...]

    mask = _tri(reverse)
    tri = mask.astype(F32)
    last = 0 if reverse else CHUNK - 1
    scale = HEAD_DIM ** -0.5

    def chunk_body(ci, carry):
        cc = (n_chunks - 1 - ci) if reverse else ci
        rows = pl.ds(pl.multiple_of(cc * CHUNK, CHUNK), CHUNK)
        pre = jnp.dot(z_ref[rows, :], wa_ref[...], precision=HIGHEST,
                      preferred_element_type=F32) + ba_ref[...]
        log_a = _log_sigmoid(pre) * (1.0 / GLA_GATE_NORM)
        b_cum = jnp.dot(tri, log_a, precision=HIGHEST, preferred_element_type=F32)
        b_last = b_cum[last:last + 1, :]
        q = q_ref[rows, :] * scale
        k = k_ref[rows, :]
        q_in = (q * jnp.exp(b_cum)).astype(BF16)
        q_rel = (q * jnp.exp(b_cum - b_last)).astype(BF16)
        k_rel = (k * jnp.exp(b_last - b_cum)).astype(BF16)
        v = v_ref[rows, :].astype(BF16)
        decay = jnp.exp(b_last)
        for h in range(N_HEADS):
            sl = slice(h * HEAD_DIM, (h + 1) * HEAD_DIM)
            att = lax.dot_general(q_rel[:, sl], k_rel[:, sl], (((1,), (1,)), ((), ())),
                                  preferred_element_type=F32)
            att = jnp.where(mask, att, 0.0).astype(BF16)
            s_t = st_ref[h]
            o = lax.dot_general(q_in[:, sl], s_t.astype(BF16), (((1,), (1,)), ((), ())),
                                preferred_element_type=F32)
            o = o + jnp.dot(att, v[:, sl], preferred_element_type=F32)
            st_ref[h] = s_t * decay[:, sl] + lax.dot_general(
                v[:, sl], k_rel[:, sl], (((0,), (0,)), ((), ())), preferred_element_type=F32)
            if final:
                o = _head_norm_gate(o + of_ref[rows, sl], gn_ref[:, sl],
                                    g_ref[rows, sl] * _sigmoid(g_ref[rows, sl]))
            out_ref[rows, sl] = o.astype(out_ref.dtype)
        return carry

    lax.fori_loop(0, n_chunks, chunk_body, 0)

    @pl.when(step == pl.num_programs(1) - 1)
    def _():
        sfin_ref[...] = st_ref[...]


def _scan_block_spec(t, cols, col_block, n_blocks, reverse):
    if reverse:
        return pl.BlockSpec((None, t, cols), lambda b, i: (b, n_blocks - 1 - i, col_block))
    return pl.BlockSpec((None, t, cols), lambda b, i: (b, i, col_block))


def _gla_scan(p3, ps3, wa_pad, ba, s0, gn, o_fwd, *, reverse, t):
    bsz, seq_len, _ = p3.shape
    n_blocks = seq_len // t
    final = o_fwd is not None
    spec = functools.partial(_scan_block_spec, n_blocks=n_blocks, reverse=reverse)
    const2 = lambda shape: pl.BlockSpec(shape, lambda b, i: (0, 0))
    state_spec = pl.BlockSpec((None, N_HEADS, HEAD_DIM, HEAD_DIM), lambda b, i: (b, 0, 0, 0))
    in_specs = [
        spec(t, D_GROUP, C_GQ), spec(t, D_GROUP, C_GK), spec(t, D_GROUP, C_GV),
        spec(t, SMALL_COLS, 0),
        const2((SMALL_COLS, D_GROUP)), const2((1, D_GROUP)),
        state_spec,
    ]
    args = [p3, p3, p3, ps3, wa_pad, ba, s0]
    if final:
        in_specs += [spec(t, D_GROUP, 0), spec(t, D_GROUP, C_GG), const2((1, D_GROUP))]
        args += [o_fwd, p3, gn]
    return pl.pallas_call(
        functools.partial(_gla_kernel, reverse=reverse, final=final, n_chunks=t // CHUNK),
        grid=(bsz, n_blocks),
        in_specs=in_specs,
        out_specs=[spec(t, D_GROUP, 0), state_spec],
        out_shape=[
            jax.ShapeDtypeStruct((bsz, seq_len, D_GROUP), BF16 if final else F32),
            jax.ShapeDtypeStruct((bsz, N_HEADS, HEAD_DIM, HEAD_DIM), F32),
        ],
        scratch_shapes=[pltpu.VMEM((N_HEADS, HEAD_DIM, HEAD_DIM), F32)],
        compiler_params=_params(("parallel", "arbitrary")),
        name="gla_bwd" if reverse else "gla_fwd",
    )(*args)


def _mlstm_kernel(*refs, reverse, final, n_chunks, period, lane0):
    if final:
        (qk_ref, v_ref, gt_ref, cw_ref, gb_ref, c0_ref, m0_ref, hf_ref, og_ref, gn_ref,
         out_ref, cfin_ref, mfin_ref, c_ref, m_ref, qkc_ref) = refs
    else:
        (qk_ref, v_ref, gt_ref, cw_ref, gb_ref, c0_ref, m0_ref,
         out_ref, cfin_ref, mfin_ref, c_ref, m_ref, qkc_ref) = refs
    step = pl.program_id(1)

    @pl.when(step == 0)
    def _():
        c_ref[...] = c0_ref[...]
        m_ref[...] = m0_ref[...]

    qkc_ref[...] = _dwconv3(qk_ref[...], cw_ref, period)

    mask = _tri(reverse)
    tri = mask.astype(F32)
    tri_t = _tri(not reverse).astype(F32)
    last = 0 if reverse else CHUNK - 1
    scale = HEAD_DIM ** -0.5
    ones_col = (lax.broadcasted_iota(jnp.int32, (CHUNK, HEAD_DIM), 1) == 0).astype(BF16)

    def chunk_body(ci, carry):
        cc = (n_chunks - 1 - ci) if reverse else ci
        rows = pl.ds(pl.multiple_of(cc * CHUNK, CHUNK), CHUNK)
        gates = gt_ref[rows, :] + gb_ref[...]
        log_f = _log_sigmoid(gates)
        f_col = jnp.dot(tri, log_f, precision=HIGHEST, preferred_element_type=F32)
        f_row = jnp.dot(log_f.T, tri_t, precision=HIGHEST, preferred_element_type=F32)
        g_row = gates.T
        for h in range(N_HEADS):
            sl = slice(h * HEAD_DIM, (h + 1) * HEAD_DIM)
            li = lane0 + h
            lf = lane0 + N_HEADS + h
            q = qkc_ref[rows, sl]
            k = qkc_ref[rows, D_GROUP + h * HEAD_DIM:D_GROUP + (h + 1) * HEAD_DIM] * scale
            v_aug = jnp.concatenate([v_ref[rows, sl].astype(BF16), ones_col], axis=1)
            f_c = f_col[:, lf:lf + 1]
            i_c = gates[:, li:li + 1]
            m_prev = m_ref[h:h + 1, :]
            d_log = jnp.where(mask, f_c - f_row[lf:lf + 1, :] + g_row[li:li + 1, :], -jnp.inf)
            inter = f_c + m_prev[:, 0:1]
            m_t = jnp.maximum(inter, jnp.max(d_log, axis=-1, keepdims=True))
            w_inter = jnp.exp(inter - m_t)
            s = lax.dot_general(q.astype(BF16), k.astype(BF16), (((1,), (1,)), ((), ())),
                                preferred_element_type=F32) * jnp.exp(d_log - m_t)
            c_aug = c_ref[h]
            r = w_inter * jnp.dot(q.astype(BF16), c_aug.astype(BF16), preferred_element_type=F32)
            r = r + jnp.dot(s.astype(BF16), v_aug, preferred_element_type=F32)
            den = jnp.maximum(jnp.abs(r[:, HEAD_DIM:HEAD_DIM + 1]), jnp.exp(-m_t))
            hid = r[:, :HEAD_DIM] * (1.0 / den)
            m_new = jnp.broadcast_to(m_t, (CHUNK, HEAD_DIM))[last:last + 1, :]
            f_last = jnp.broadcast_to(f_c, (CHUNK, HEAD_DIM))[last:last + 1, :]
            decay = jnp.exp(f_last + m_prev - m_new)
            w_k = jnp.exp(f_last[:, 0:1] - f_c + i_c - m_new[:, 0:1])
            kw = (k * w_k).astype(BF16)
            c_ref[h] = c_aug * jnp.concatenate([decay, decay], axis=1) + lax.dot_general(
                kw, v_aug, (((0,), (0,)), ((), ())), preferred_element_type=F32)
            m_ref[h:h + 1, :] = m_new
            if final:
                og = og_ref[rows, sl]
                hid = _head_norm_gate(hid + hf_ref[rows, sl], gn_ref[:, sl], _sigmoid(og))
            out_ref[rows, sl] = hid.astype(out_ref.dtype)
        return carry

    lax.fori_loop(0, n_chunks, chunk_body, 0)

    @pl.when(step == pl.num_programs(1) - 1)
    def _():
        cfin_ref[...] = c_ref[...]
        mfin_ref[...] = m_ref[...]


def _mlstm_scan(p3, ps3, conv_w, gate_b, c0, m0, gn, h_fwd, *, reverse, t, period):
    bsz, seq_len, _ = p3.shape
    n_blocks = seq_len // t
    final = h_fwd is not None
    spec = functools.partial(_scan_block_spec, n_blocks=n_blocks, reverse=reverse)
    const2 = lambda shape: pl.BlockSpec(shape, lambda b, i: (0, 0))
    c_spec = pl.BlockSpec((None, N_HEADS, HEAD_DIM, 2 * HEAD_DIM), lambda b, i: (b, 0, 0, 0))
    m_spec = pl.BlockSpec((None, 8, LANES), lambda b, i: (b, 0, 0))
    in_specs = [
        spec(t, 2 * D_GROUP, C_MQK // 2), spec(t, D_GROUP, C_MV), spec(t, SMALL_COLS, 0),
        const2((3, 2 * D_GROUP)), const2((1, SMALL_COLS)), c_spec, m_spec,
    ]
    args = [p3, p3, ps3, conv_w, gate_b, c0, m0]
    if final:
        in_specs += [spec(t, D_GROUP, 0), spec(t, D_GROUP, C_MO), const2((1, D_GROUP))]
        args += [h_fwd, p3, gn]
    lane0 = GATE0 + (2 * N_HEADS if reverse else 0)
    return pl.pallas_call(
        functools.partial(_mlstm_kernel, reverse=reverse, final=final, n_chunks=t // CHUNK,
                          period=period, lane0=lane0),
        grid=(bsz, n_blocks),
        in_specs=in_specs,
        out_specs=[spec(t, D_GROUP, 0), c_spec, m_spec],
        out_shape=[
            jax.ShapeDtypeStruct((bsz, seq_len, D_GROUP), BF16 if final else F32),
            jax.ShapeDtypeStruct((bsz, N_HEADS, HEAD_DIM, 2 * HEAD_DIM), F32),
            jax.ShapeDtypeStruct((bsz, 8, LANES), F32),
        ],
        scratch_shapes=[
            pltpu.VMEM((N_HEADS, HEAD_DIM, 2 * HEAD_DIM), F32),
            pltpu.VMEM((8, LANES), F32),
            pltpu.VMEM((t, 2 * D_GROUP), F32),
        ],
        compiler_params=_params(("parallel", "arbitrary")),
        name="mlstm_bwd" if reverse else "mlstm_fwd",
    )(*args)


def _sccm_kernel(bg_ref, cg_ref, hh_ref, u_ref, v_ref, scw_ref, ws_ref, bs_ref, gv_ref,
                 ysc_ref, ycm_ref, *, period):
    ysc_ref[...] = (bg_ref[...] * _dwconv3(cg_ref[...] * hh_ref[...], scw_ref, period)
                    ).astype(ysc_ref.dtype)
    v = _gelu_tanh(v_ref[...])
    vn = (v * lax.rsqrt(jnp.mean(v * v, axis=-1, keepdims=True) + EPS) * gv_ref[...]).astype(BF16)
    for n in range(v.shape[0] // CM_CHUNK):
        rows = slice(n * CM_CHUNK, (n + 1) * CM_CHUNK)
        for g in range(N_HEADS):
            sl = slice(g * HEAD_DIM, (g + 1) * HEAD_DIM)
            sv = jnp.dot(ws_ref[g], vn[rows, sl], preferred_element_type=F32) + bs_ref[:, sl]
            ycm_ref[rows, sl] = (_gelu_tanh(u_ref[rows, sl]) * sv).astype(ycm_ref.dtype)


def _sc_cm_mixers(p2, sc_w, ws, bs_b, gv, *, t, period):
    m = p2.shape[0]
    col = lambda cb: pl.BlockSpec((t, D_GROUP), lambda i: (i, cb))
    const = lambda shape: pl.BlockSpec(shape, lambda i: (0,) * len(shape))
    return pl.pallas_call(
        functools.partial(_sccm_kernel, period=period),
        grid=(m // t,),
        in_specs=[col(C_SB), col(C_SC), col(C_SH), col(C_CU), col(C_CV),
                  const((3, D_GROUP)), const((N_HEADS, CM_CHUNK, CM_CHUNK)),
                  const((CM_CHUNK, D_GROUP)), const((1, D_GROUP))],
        out_specs=[pl.BlockSpec((t, D_GROUP), lambda i: (i, 0))] * 2,
        out_shape=[jax.ShapeDtypeStruct((m, D_GROUP), BF16)] * 2,
        compiler_params=_params(("parallel",)),
        name="sc_cm_mixers",
    )(p2, p2, p2, p2, p2, sc_w, ws, bs_b, gv)


def _outproj_kernel(y0_ref, y1_ref, y2_ref, y3_ref, w_ref, x_ref, g_ref, o_ref):
    acc = jnp.dot(y0_ref[...], w_ref[0:D_GROUP, :], preferred_element_type=F32)
    for n, y_ref in enumerate((y1_ref, y2_ref, y3_ref), start=1):
        acc = acc + jnp.dot(y_ref[...], w_ref[n * D_GROUP:(n + 1) * D_GROUP, :],
                            preferred_element_type=F32)
    o_ref[...] = x_ref[...] + g_ref[...] * acc


def _out_projection(ys, w_out, x2, mod, seq_len, tm):
    m, d = x2.shape
    y_spec = pl.BlockSpec((tm, D_GROUP), lambda i: (i, 0))
    return pl.pallas_call(
        _outproj_kernel,
        grid=(m // tm,),
        in_specs=[y_spec] * N_MIXERS + [
            pl.BlockSpec((d, d), lambda i: (0, 0)),
            pl.BlockSpec((tm, d), lambda i: (i, 0)),
            _row_spec(2, seq_len, tm, d),
        ],
        out_specs=pl.BlockSpec((tm, d), lambda i: (i, 0)),
        out_shape=jax.ShapeDtypeStruct((m, d), F32),
        compiler_params=_params(("parallel",)),
        name="out_projection",
    )(*ys, w_out, x2, mod)


def _ffn_kernel(*refs, period, final):
    if final:
        (x_ref, g_ref, sh_ref, sc_ref, gate_ref, wa_ref, wv_ref, cw_ref, wd_ref, fn_ref,
         o_ref, hn_ref, acc_ref) = refs
    else:
        (x_ref, g_ref, sh_ref, sc_ref, gate_ref, wa_ref, wv_ref, cw_ref, wd_ref,
         o_ref, hn_ref, acc_ref) = refs
    j = pl.program_id(1)

    @pl.when(j == 0)
    def _():
        hn_ref[...] = _modulated_norm(x_ref[...], g_ref[...], sc_ref[...], sh_ref[...]).astype(BF16)
        acc_ref[...] = jnp.zeros_like(acc_ref)

    hn = hn_ref[...]
    a = _dwconv3(jnp.dot(hn, wa_ref[...], preferred_element_type=F32), cw_ref, period)
    v = jnp.dot(hn, wv_ref[...], preferred_element_type=F32)
    hid = (a * _sigmoid(a) * v).astype(BF16)
    acc_ref[...] += jnp.dot(hid, wd_ref[...], preferred_element_type=F32)

    @pl.when(j == pl.num_programs(1) - 1)
    def _():
        y = x_ref[...] + gate_ref[...] * acc_ref[...]
        if final:
            y = y * lax.rsqrt(jnp.mean(y * y, axis=-1, keepdims=True) + EPS) * fn_ref[...]
        o_ref[...] = y


def _conv_ffn(x2, gain, mod, w_up, w_conv, w_down, final_gain, seq_len, period, tm, tf=512):
    m, d = x2.shape
    d_ff = w_down.shape[0]
    n_f = d_ff // tf
    final = final_gain is not None
    in_specs = [
        pl.BlockSpec((tm, d), lambda i, j: (i, 0)),
        pl.BlockSpec((1, d), lambda i, j: (0, 0)),
        _row_spec(3, seq_len, tm, d), _row_spec(4, seq_len, tm, d), _row_spec(5, seq_len, tm, d),
        pl.BlockSpec((d, tf), lambda i, j: (0, j)),
        pl.BlockSpec((d, tf), lambda i, j: (0, j + n_f)),
        pl.BlockSpec((3, tf), lambda i, j: (0, j)),
        pl.BlockSpec((tf, d), lambda i, j: (j, 0)),
    ]
    args = [x2, gain.reshape(1, d), mod, mod, mod, w_up, w_up, w_conv, w_down]
    if final:
        in_specs.append(pl.BlockSpec((1, d), lambda i, j: (0, 0)))
        args.append(final_gain.reshape(1, d))
    return pl.pallas_call(
        functools.partial(_ffn_kernel, period=period, final=final),
        grid=(m // tm, n_f),
        in_specs=in_specs,
        out_specs=pl.BlockSpec((tm, d), lambda i, j: (i, 0)),
        out_shape=jax.ShapeDtypeStruct((m, d), F32),
        scratch_shapes=[pltpu.VMEM((tm, d), BF16), pltpu.VMEM((tm, d), F32)],
        compiler_params=_params(("parallel", "arbitrary")),
        name="conv_ffn",
    )(*args)


def _relayout_w_in(w):
    d = w.shape[0]
    gla = 4 * D_GROUP + 2 * GLA_LOWRANK
    ml = 4 * D_GROUP + 4 * N_HEADS
    g0, m0 = 0, gla
    wide = jnp.concatenate([w[:, g0:g0 + 4 * D_GROUP], w[:, m0:m0 + 4 * D_GROUP],
                            w[:, m0 + ml:]], axis=1)
    small = jnp.concatenate([
        w[:, g0 + 4 * D_GROUP:g0 + gla], w[:, m0 + 4 * D_GROUP:m0 + ml],
        jnp.zeros((d, SMALL_COLS - 2 * GLA_LOWRANK - 4 * N_HEADS), w.dtype)], axis=1)
    return wide.astype(BF16), small.astype(BF16)


def _mixers(p2, ps2, bsz, seq_len, prm, states, *, t_scan, period, need_out):
    p3 = p2.reshape(bsz, seq_len, -1)
    ps3 = ps2.reshape(bsz, seq_len, SMALL_COLS)
    (gs_f, gs_b), (mc_f, mm_f, mc_b, mm_b) = states
    o_f, gs_f = _gla_scan(p3, ps3, prm["wa_f"], prm["ba_f"], gs_f, None, None,
                          reverse=False, t=t_scan)
    y_gla, gs_b = _gla_scan(p3, ps3, prm["wa_b"], prm["ba_b"], gs_b, prm["gla_norm"], o_f,
                            reverse=True, t=t_scan)
    h_f, mc_f, mm_f = _mlstm_scan(p3, ps3, prm["ml_conv"], prm["ml_gate_b"], mc_f, mm_f,
                                  None, None, reverse=False, t=t_scan, period=period)
    y_ml, mc_b, mm_b = _mlstm_scan(p3, ps3, prm["ml_conv"], prm["ml_gate_b"], mc_b, mm_b,
                                   prm["ml_norm"], h_f, reverse=True, t=t_scan, period=period)
    new_states = ((gs_f, gs_b), (mc_f, mm_f, mc_b, mm_b))
    if not need_out:
        return None, new_states
    y_sc, y_cm = _sc_cm_mixers(p2, prm["sc_conv"], prm["cm_ws"], prm["cm_bs"], prm["cm_norm"],
                               t=min(512, seq_len), period=period)
    m = bsz * seq_len
    return (y_gla.reshape(m, D_GROUP), y_ml.reshape(m, D_GROUP), y_sc, y_cm), new_states


def kernel(x, c, ctx, c_ctx, w_mod, b_mod, norm1, norm2, w_in, w_out, gla_wa2, gla_ba, gla_norm,
           mlstm_conv, mlstm_gate_b, mlstm_norm, sc_conv, cm_ws, cm_bs, cm_norm,
           ffn_up, ffn_conv, ffn_down, final_norm):
    bsz, seq_len, d = x.shape
    ctx_len = ctx.shape[1]
    depth = w_mod.shape[0]
    assert seq_len % 512 == 0 and ctx_len % CM_CHUNK == 0 and bsz + 1 <= 8

    cond = jnp.zeros((8, d), F32).at[:bsz].set(c).at[bsz].set(c_ctx)
    mod = _modulation(cond, w_mod, b_mod).reshape(depth, 8, 6, d).transpose(0, 2, 1, 3)

    xl = x.reshape(bsz * seq_len, d)
    xc = ctx.reshape(bsz * ctx_len, d)
    tm_l, tm_c = 512, ctx_len

    zero_states = (
        (jnp.zeros((bsz, N_HEADS, HEAD_DIM, HEAD_DIM), F32),) * 2,
        (jnp.zeros((bsz, N_HEADS, HEAD_DIM, 2 * HEAD_DIM), F32), jnp.zeros((bsz, 8, LANES), F32)) * 2,
    )

    for l in range(depth):
        need_ctx = l < depth - 1
        mod_l = mod[l][:, :bsz, None, :]
        mod_c = jnp.broadcast_to(mod[l][:, bsz:bsz + 1, None, :], (6, bsz, 1, d))
        w_wide, w_small = _relayout_w_in(w_in[l])
        wa = jnp.zeros((2, SMALL_COLS, D_GROUP), F32)
        wa = wa.at[0, Z_FWD:Z_FWD + GLA_LOWRANK].set(gla_wa2[l, 0])
        wa = wa.at[1, Z_BWD:Z_BWD + GLA_LOWRANK].set(gla_wa2[l, 1])
        gate_b = jnp.zeros((1, SMALL_COLS), F32).at[0, GATE0:GATE0 + 4 * N_HEADS].set(
            mlstm_gate_b[l].reshape(-1))
        prm = dict(
            wa_f=wa[0], wa_b=wa[1], ba_f=gla_ba[l, 0:1], ba_b=gla_ba[l, 1:2],
            gla_norm=gla_norm[l].reshape(1, D_GROUP),
            ml_conv=mlstm_conv[l], ml_gate_b=gate_b, ml_norm=mlstm_norm[l].reshape(1, D_GROUP),
            sc_conv=sc_conv[l], cm_ws=cm_ws[l].astype(BF16),
            cm_bs=jnp.repeat(cm_bs[l].T, HEAD_DIM, axis=1),
            cm_norm=cm_norm[l].reshape(1, D_GROUP),
        )
        w_out_l = w_out[l].astype(BF16)
        w_up_l, w_down_l = ffn_up[l].astype(BF16), ffn_down[l].astype(BF16)

        pc, psc = _in_projection(xc, norm1[l], mod_c, w_wide, w_small, ctx_len, tm_c)
        yc, ctx_states = _mixers(pc, psc, bsz, ctx_len, prm, zero_states,
                                 t_scan=ctx_len, period=ctx_len, need_out=need_ctx)
        p_l, ps_l = _in_projection(xl, norm1[l], mod_l, w_wide, w_small, seq_len, tm_l)
        yl, _ = _mixers(p_l, ps_l, bsz, seq_len, prm, ctx_states,
                        t_scan=512, period=GRID_W, need_out=True)

        xl = _out_projection(yl, w_out_l, xl, mod_l, seq_len, tm_l)
        xl = _conv_ffn(xl, norm2[l], mod_l, w_up_l, ffn_conv[l], w_down_l,
                       final_norm if l == depth - 1 else None, seq_len, GRID_W, tm_l)
        if need_ctx:
            xc = _out_projection(yc, w_out_l, xc, mod_c, ctx_len, tm_c)
            xc = _conv_ffn(xc, norm2[l], mod_c, w_up_l, ffn_conv[l], w_down_l, None,
                           ctx_len, ctx_len, tm_c)
    return xl.reshape(bsz, seq_len, d)
```

```python
import functools

import jax
import jax.numpy as jnp
from jax import lax
from jax.experimental import pallas as pl
from jax.experimental.pallas import tpu as pltpu

F32 = jnp.float32
BF16 = jnp.bfloat16

GRID_W = 64
N_MIXERS = 4
HEAD_DIM = 128
N_HEADS = 4
D_GROUP = N_HEADS * HEAD_DIM
GLA_LOWRANK = 16
GLA_GATE_NORM = 16.0
CHUNK = 64
CM_CHUNK = 128
EPS = 1e-6
LANES = 128
SMALL_COLS = LANES
Z_FWD, Z_BWD, GATE0 = 0, GLA_LOWRANK, 2 * GLA_LOWRANK
VMEM_LIMIT = 56 * 1024 * 1024

(C_GQ, C_GK, C_GV, C_GG, C_MQK, C_MV, C_MO, C_SB, C_SC, C_SH, C_CU, C_CV) = (
    0, 1, 2, 3, 4, 6, 7, 8, 9, 10, 11, 12)
N_WIDE_BLOCKS = 13


def _params(sem):
    return pltpu.CompilerParams(dimension_semantics=sem, vmem_limit_bytes=VMEM_LIMIT)


def _sigmoid(x):
    return 1.0 / (1.0 + jnp.exp(-x))


def _log_sigmoid(x):
    return jnp.minimum(x, 0.0) - jnp.log1p(jnp.exp(-jnp.abs(x)))


def _gelu_tanh(x):
    return x * (0.5 * (1.0 + jnp.tanh(0.7978845608028654 * (x + 0.044715 * (x * x * x)))))


def _shift_rows(x, period):
    n = x.shape[0]
    row = lax.broadcasted_iota(jnp.int32, (n, 1), 0) % period
    prev = jnp.where(row == 0, 0.0, pltpu.roll(x, 1, axis=0))
    nxt = jnp.where(row == period - 1, 0.0, pltpu.roll(x, n - 1, axis=0))
    return prev, nxt


def _dwconv3(x, w_ref, period):
    prev, nxt = _shift_rows(x, period)
    return w_ref[0:1, :] * prev + w_ref[1:2, :] * x + w_ref[2:3, :] * nxt


def _mod_kernel(c_ref, w_ref, b_ref, o_ref):
    s = c_ref[...]
    s = (s * _sigmoid(s)).astype(BF16)
    o_ref[...] = jnp.dot(s, w_ref[...].astype(BF16), preferred_element_type=F32) + b_ref[...]


def _modulation(cond, w_mod, b_mod, tn=1024):
    depth, d, n = w_mod.shape
    rows = cond.shape[0]
    return pl.pallas_call(
        _mod_kernel,
        grid=(depth, n // tn),
        in_specs=[
            pl.BlockSpec((rows, d), lambda l, j: (0, 0)),
            pl.BlockSpec((None, d, tn), lambda l, j: (l, 0, j)),
            pl.BlockSpec((None, 1, tn), lambda l, j: (l, 0, j)),
        ],
        out_specs=pl.BlockSpec((None, rows, tn), lambda l, j: (l, 0, j)),
        out_shape=jax.ShapeDtypeStruct((depth, rows, n), F32),
        compiler_params=_params(("parallel", "parallel")),
        name="modulation",
    )(cond, w_mod, b_mod.reshape(depth, 1, n))


def _modulated_norm(x, g, sc, sh):
    r = lax.rsqrt(jnp.mean(x * x, axis=-1, keepdims=True) + EPS)
    return (x * r * g) * (1.0 + sc) + sh


def _inproj_kernel(x_ref, g_ref, sh_ref, sc_ref, w_ref, ws_ref, p_ref, ps_ref, hn_ref):
    @pl.when(pl.program_id(1) == 0)
    def _():
        hn = _modulated_norm(x_ref[...], g_ref[...], sc_ref[...], sh_ref[...]).astype(BF16)
        hn_ref[...] = hn
        ps_ref[...] = jnp.dot(hn, ws_ref[...], preferred_element_type=F32)

    p_ref[...] = jnp.dot(hn_ref[...], w_ref[...], preferred_element_type=F32).astype(p_ref.dtype)


def _row_spec(piece, seq_len, tm, d):
    return pl.BlockSpec((None, None, 1, d), lambda i, *_: (piece, (i * tm) // seq_len, 0, 0))


def _in_projection(x2, gain, mod, w_wide, w_small, seq_len, tm, tn=D_GROUP):
    m, d = x2.shape
    n = w_wide.shape[1]
    return pl.pallas_call(
        _inproj_kernel,
        grid=(m // tm, n // tn),
        in_specs=[
            pl.BlockSpec((tm, d), lambda i, j: (i, 0)),
            pl.BlockSpec((1, d), lambda i, j: (0, 0)),
            _row_spec(0, seq_len, tm, d),
            _row_spec(1, seq_len, tm, d),
            pl.BlockSpec((d, tn), lambda i, j: (0, j)),
            pl.BlockSpec((d, SMALL_COLS), lambda i, j: (0, 0)),
        ],
        out_specs=[
            pl.BlockSpec((tm, tn), lambda i, j: (i, j)),
            pl.BlockSpec((tm, SMALL_COLS), lambda i, j: (i, 0)),
        ],
        out_shape=[
            jax.ShapeDtypeStruct((m, n), BF16),
            jax.ShapeDtypeStruct((m, SMALL_COLS), F32),
        ],
        scratch_shapes=[pltpu.VMEM((tm, d), BF16)],
        compiler_params=_params(("parallel", "arbitrary")),
        name="in_projection",
    )(x2, gain.reshape(1, d), mod, mod, w_wide, w_small)


def _tri(reverse):
    r = lax.broadcasted_iota(jnp.int32, (CHUNK, CHUNK), 0)
    c = lax.broadcasted_iota(jnp.int32, (CHUNK, CHUNK), 1)
    return (c >= r) if reverse else (c <= r)


def _tri3(reverse):
    r = lax.broadcasted_iota(jnp.int32, (CHUNK, 3 * CHUNK), 0)
    c = lax.broadcasted_iota(jnp.int32, (CHUNK, 3 * CHUNK), 1) % CHUNK
    return ((c >= r) if reverse else (c <= r)).astype(BF16)


def _split2(x):
    hi = x.astype(BF16)
    return hi, (x - hi.astype(F32)).astype(BF16)


def _chunk_cumsum(tri3, x):
    x1 = x.astype(BF16)
    r1 = x - x1.astype(F32)
    x2 = r1.astype(BF16)
    x3 = (r1 - x2.astype(F32)).astype(BF16)
    return jnp.dot(tri3, jnp.concatenate([x1, x2, x3], axis=0), preferred_element_type=F32)


def _head_norm_gate(o, gain, gate):
    r = lax.rsqrt(jnp.mean(o * o, axis=-1, keepdims=True) + EPS)
    return o * r * gain * gate


def _gla_kernel(*refs, reverse, final, n_chunks):
    if final:
        (q_ref, k_ref, v_ref, z_ref, wa_ref, ba_ref, s0_ref, of_ref, g_ref, gn_ref,
         out_ref, sfin_ref, st_ref, qin_ref, qrel_ref, krel_ref, dec_ref, att_ref, kv_ref,
         acc_ref) = refs
    else:
        (q_ref, k_ref, v_ref, z_ref, wa_ref, ba_ref, s0_ref,
         out_ref, sfin_ref, st_ref, qin_ref, qrel_ref, krel_ref, dec_ref, att_ref,
         kv_ref) = refs
        acc_ref = out_ref
    step = pl.program_id(1)

    @pl.when(step == 0)
    def _():
        st_ref[...] = s0_ref[...]

    mask = _tri(reverse)
    tri3 = _tri3(reverse)
    last = 0 if reverse else CHUNK - 1
    scale = HEAD_DIM ** -0.5
    chunk_rows = [slice(c * CHUNK, (c + 1) * CHUNK) for c in range(n_chunks)]
    head_cols = [slice(h * HEAD_DIM, (h + 1) * HEAD_DIM) for h in range(N_HEADS)]

    z_hi, z_lo = _split2(z_ref[...])
    pre = jnp.dot(jnp.concatenate([z_hi, z_lo, z_hi], axis=1), wa_ref[...],
                  preferred_element_type=F32) + ba_ref[...]
    log_a = _log_sigmoid(pre) * (1.0 / GLA_GATE_NORM)

    for c, rows in enumerate(chunk_rows):
        b_cum = _chunk_cumsum(tri3, log_a[rows, :])
        b_last = b_cum[last:last + 1, :]
        q = q_ref[rows, :].astype(F32) * scale
        qin_ref[rows, :] = (q * jnp.exp(b_cum)).astype(BF16)
        qrel_ref[rows, :] = (q * jnp.exp(b_cum - b_last)).astype(BF16)
        krel_ref[rows, :] = (k_ref[rows, :].astype(F32) * jnp.exp(b_last - b_cum)).astype(BF16)
        dec_ref[c] = jnp.exp(b_last)

    for c, rows in enumerate(chunk_rows):
        for h, sl in enumerate(head_cols):
            att = lax.dot_general(qrel_ref[rows, sl], krel_ref[rows, sl],
                                  (((1,), (1,)), ((), ())), preferred_element_type=F32)
            att_ref[c * N_HEADS + h] = jnp.where(mask, att, 0.0).astype(BF16)

    for c, rows in enumerate(chunk_rows):
        for h, sl in enumerate(head_cols):
            v = v_ref[rows, sl]
            acc_ref[rows, sl] = jnp.dot(att_ref[c * N_HEADS + h], v, preferred_element_type=F32)
            kv_ref[c * N_HEADS + h] = lax.dot_general(
                v, krel_ref[rows, sl], (((0,), (0,)), ((), ())), preferred_element_type=F32)

    for c in (reversed(range(n_chunks)) if reverse else range(n_chunks)):
        rows = chunk_rows[c]
        for h, sl in enumerate(head_cols):
            s_t = st_ref[h]
            acc_ref[rows, sl] += lax.dot_general(qin_ref[rows, sl], s_t.astype(BF16),
                                                 (((1,), (1,)), ((), ())),
                                                 preferred_element_type=F32)
            st_ref[h] = s_t * dec_ref[c][:, sl] + kv_ref[c * N_HEADS + h]

    if final:
        for sl in head_cols:
            g = g_ref[:, sl].astype(F32)
            out_ref[:, sl] = _head_norm_gate(acc_ref[:, sl] + of_ref[:, sl], gn_ref[:, sl],
                                             g * _sigmoid(g)).astype(out_ref.dtype)

    @pl.when(step == pl.num_programs(1) - 1)
    def _():
        sfin_ref[...] = st_ref[...]


def _scan_block_spec(t, cols, col_block, n_blocks, reverse):
    if reverse:
        return pl.BlockSpec((None, t, cols), lambda b, i: (b, n_blocks - 1 - i, col_block))
    return pl.BlockSpec((None, t, cols), lambda b, i: (b, i, col_block))


def _gla_scan(p3, ps3, wa_pad, ba, s0, gn, o_fwd, *, reverse, t):
    bsz, seq_len, _ = p3.shape
    n_blocks = seq_len // t
    n_chunks = t // CHUNK
    final = o_fwd is not None
    spec = functools.partial(_scan_block_spec, n_blocks=n_blocks, reverse=reverse)
    const2 = lambda shape: pl.BlockSpec(shape, lambda b, i: (0, 0))
    state_spec = pl.BlockSpec((None, N_HEADS, HEAD_DIM, HEAD_DIM), lambda b, i: (b, 0, 0, 0))
    in_specs = [
        spec(t, D_GROUP, C_GQ), spec(t, D_GROUP, C_GK), spec(t, D_GROUP, C_GV),
        spec(t, SMALL_COLS, 0),
        const2((3 * SMALL_COLS, D_GROUP)), const2((1, D_GROUP)),
        state_spec,
    ]
    args = [p3, p3, p3, ps3, wa_pad, ba, s0]
    if final:
        in_specs += [spec(t, D_GROUP, 0), spec(t, D_GROUP, C_GG), const2((1, D_GROUP))]
        args += [o_fwd, p3, gn]
    return pl.pallas_call(
        functools.partial(_gla_kernel, reverse=reverse, final=final, n_chunks=n_chunks),
        grid=(bsz, n_blocks),
        in_specs=in_specs,
        out_specs=[spec(t, D_GROUP, 0), state_spec],
        out_shape=[
            jax.ShapeDtypeStruct((bsz, seq_len, D_GROUP), BF16 if final else F32),
            jax.ShapeDtypeStruct((bsz, N_HEADS, HEAD_DIM, HEAD_DIM), F32),
        ],
        scratch_shapes=[
            pltpu.VMEM((N_HEADS, HEAD_DIM, HEAD_DIM), F32),
            pltpu.VMEM((t, D_GROUP), BF16), pltpu.VMEM((t, D_GROUP), BF16),
            pltpu.VMEM((t, D_GROUP), BF16),
            pltpu.VMEM((n_chunks, 1, D_GROUP), F32),
            pltpu.VMEM((n_chunks * N_HEADS, CHUNK, CHUNK), BF16),
            pltpu.VMEM((n_chunks * N_HEADS, HEAD_DIM, HEAD_DIM), F32),
        ] + ([pltpu.VMEM((t, D_GROUP), F32)] if final else []),
        compiler_params=_params(("parallel", "arbitrary")),
        name="gla_bwd" if reverse else "gla_fwd",
    )(*args)


def _mlstm_kernel(*refs, reverse, final, n_chunks, period, lane0):
    if final:
        (qk_ref, v_ref, gt_ref, cw_ref, gb_ref, c0_ref, m0_ref, hf_ref, og_ref, gn_ref,
         out_ref, cfin_ref, mfin_ref, c_ref, m_ref, qkc_ref, fcol_ref, jcol_ref, jt_ref,
         mprev_ref, ulast_ref, s_ref, wi_ref, en_ref, kv_ref, r_ref, acc_ref) = refs
    else:
        (qk_ref, v_ref, gt_ref, cw_ref, gb_ref, c0_ref, m0_ref,
         out_ref, cfin_ref, mfin_ref, c_ref, m_ref, qkc_ref, fcol_ref, jcol_ref, jt_ref,
         mprev_ref, ulast_ref, s_ref, wi_ref, en_ref, kv_ref, r_ref) = refs
        acc_ref = out_ref
    step = pl.program_id(1)

    @pl.when(step == 0)
    def _():
        c_ref[...] = c0_ref[...]
        m_ref[...] = m0_ref[...]

    qkc_ref[...] = _dwconv3(qk_ref[...].astype(F32), cw_ref, period)

    mask = _tri(reverse)
    tri3 = _tri3(reverse)
    last = 0 if reverse else CHUNK - 1
    scale = HEAD_DIM ** -0.5
    ones_col = (lax.broadcasted_iota(jnp.int32, (CHUNK, HEAD_DIM), 1) == 0).astype(BF16)
    chunk_rows = [slice(c * CHUNK, (c + 1) * CHUNK) for c in range(n_chunks)]
    head_cols = [slice(h * HEAD_DIM, (h + 1) * HEAD_DIM) for h in range(N_HEADS)]
    key_cols = [slice(D_GROUP + h * HEAD_DIM, D_GROUP + (h + 1) * HEAD_DIM) for h in range(N_HEADS)]
    f_lane = [lane0 + N_HEADS + h for h in range(N_HEADS)]
    scan_order = list(reversed(range(n_chunks))) if reverse else list(range(n_chunks))

    gates = gt_ref[...] + gb_ref[...]
    log_f = _log_sigmoid(gates)
    i_al = pltpu.roll(gates, N_HEADS, axis=1)
    j_max, f_tot = [], []
    for c, rows in enumerate(chunk_rows):
        f_col = _chunk_cumsum(tri3, log_f[rows, :])
        j_col = i_al[rows, :] - f_col
        fcol_ref[rows, :] = f_col
        jcol_ref[rows, :] = j_col
        jt_ref[c] = j_col.T
        j_max.append(jnp.max(j_col, axis=0, keepdims=True))
        f_tot.append(f_col[last:last + 1, :])

    m_run = m_ref[0:1, :]
    for c in scan_order:
        mprev_ref[c] = m_run
        u_last = jnp.maximum(m_run, j_max[c])
        ulast_ref[c] = u_last
        m_run = f_tot[c] + u_last
    m_ref[0:1, :] = m_run

    for c, rows in enumerate(chunk_rows):
        for h in range(N_HEADS):
            lf = f_lane[h]
            a = jnp.where(mask, jt_ref[c][lf:lf + 1, :], -jnp.inf)
            m_prev = mprev_ref[c][:, lf:lf + 1]
            u = jnp.maximum(jnp.max(a, axis=-1, keepdims=True), m_prev)
            q = qkc_ref[rows, head_cols[h]].astype(BF16)
            k = (qkc_ref[rows, key_cols[h]] * scale).astype(BF16)
            s = lax.dot_general(q, k, (((1,), (1,)), ((), ())),
                                preferred_element_type=F32) * jnp.exp(a - u)
            s_ref[c * N_HEADS + h] = s.astype(BF16)
            wi_ref[c * N_HEADS + h] = jnp.broadcast_to(jnp.exp(m_prev - u), (CHUNK, LANES))
            en_ref[c * N_HEADS + h] = jnp.broadcast_to(
                jnp.exp(-(fcol_ref[rows, lf:lf + 1] + u)), (CHUNK, LANES))

    for c, rows in enumerate(chunk_rows):
        for h in range(N_HEADS):
            lf = f_lane[h]
            v_aug = jnp.concatenate([v_ref[rows, head_cols[h]], ones_col], axis=1)
            r_ref[rows, 2 * h * HEAD_DIM:2 * (h + 1) * HEAD_DIM] = jnp.dot(
                s_ref[c * N_HEADS + h], v_aug, preferred_element_type=F32)
            w_k = jnp.exp(jcol_ref[rows, lf:lf + 1] - ulast_ref[c][:, lf:lf + 1])
            kw = (qkc_ref[rows, key_cols[h]] * scale * w_k).astype(BF16)
            kv_ref[c * N_HEADS + h] = lax.dot_general(
                kw, v_aug, (((0,), (0,)), ((), ())), preferred_element_type=F32)

    for c in scan_order:
        rows = chunk_rows[c]
        decay = jnp.exp(mprev_ref[c] - ulast_ref[c])
        for h in range(N_HEADS):
            lf = f_lane[h]
            idx = c * N_HEADS + h
            c_aug = c_ref[h]
            wi = wi_ref[idx]
            r = jnp.concatenate([wi, wi], axis=1) * jnp.dot(
                qkc_ref[rows, head_cols[h]].astype(BF16), c_aug.astype(BF16),
                preferred_element_type=F32) + r_ref[rows, 2 * h * HEAD_DIM:2 * (h + 1) * HEAD_DIM]
            den = jnp.maximum(jnp.abs(r[:, HEAD_DIM:HEAD_DIM + 1]), en_ref[idx][:, 0:1])
            acc_ref[rows, head_cols[h]] = r[:, :HEAD_DIM] * (1.0 / den)
            c_ref[h] = c_aug * decay[:, lf:lf + 1] + kv_ref[idx]

    if final:
        for sl in head_cols:
            out_ref[:, sl] = _head_norm_gate(acc_ref[:, sl] + hf_ref[:, sl], gn_ref[:, sl],
                                             _sigmoid(og_ref[:, sl].astype(F32))
                                             ).astype(out_ref.dtype)

    @pl.when(step == pl.num_programs(1) - 1)
    def _():
        cfin_ref[...] = c_ref[...]
        mfin_ref[...] = m_ref[...]


def _mlstm_scan(p3, ps3, conv_w, gate_b, c0, m0, gn, h_fwd, *, reverse, t, period):
    bsz, seq_len, _ = p3.shape
    n_blocks = seq_len // t
    n_chunks = t // CHUNK
    final = h_fwd is not None
    spec = functools.partial(_scan_block_spec, n_blocks=n_blocks, reverse=reverse)
    const2 = lambda shape: pl.BlockSpec(shape, lambda b, i: (0, 0))
    c_spec = pl.BlockSpec((None, N_HEADS, HEAD_DIM, 2 * HEAD_DIM), lambda b, i: (b, 0, 0, 0))
    m_spec = pl.BlockSpec((None, 8, LANES), lambda b, i: (b, 0, 0))
    in_specs = [
        spec(t, 2 * D_GROUP, C_MQK // 2), spec(t, D_GROUP, C_MV), spec(t, SMALL_COLS, 0),
        const2((3, 2 * D_GROUP)), const2((1, SMALL_COLS)), c_spec, m_spec,
    ]
    args = [p3, p3, ps3, conv_w, gate_b, c0, m0]
    if final:
        in_specs += [spec(t, D_GROUP, 0), spec(t, D_GROUP, C_MO), const2((1, D_GROUP))]
        args += [h_fwd, p3, gn]
    lane0 = GATE0 + (2 * N_HEADS if reverse else 0)
    return pl.pallas_call(
        functools.partial(_mlstm_kernel, reverse=reverse, final=final, n_chunks=n_chunks,
                          period=period, lane0=lane0),
        grid=(bsz, n_blocks),
        in_specs=in_specs,
        out_specs=[spec(t, D_GROUP, 0), c_spec, m_spec],
        out_shape=[
            jax.ShapeDtypeStruct((bsz, seq_len, D_GROUP), BF16 if final else F32),
            jax.ShapeDtypeStruct((bsz, N_HEADS, HEAD_DIM, 2 * HEAD_DIM), F32),
            jax.ShapeDtypeStruct((bsz, 8, LANES), F32),
        ],
        scratch_shapes=[
            pltpu.VMEM((N_HEADS, HEAD_DIM, 2 * HEAD_DIM), F32),
            pltpu.VMEM((8, LANES), F32),
            pltpu.VMEM((t, 2 * D_GROUP), F32),
            pltpu.VMEM((t, SMALL_COLS), F32),
            pltpu.VMEM((t, SMALL_COLS), F32),
            pltpu.VMEM((n_chunks, SMALL_COLS, CHUNK), F32),
            pltpu.VMEM((n_chunks, 1, SMALL_COLS), F32),
            pltpu.VMEM((n_chunks, 1, SMALL_COLS), F32),
            pltpu.VMEM((n_chunks * N_HEADS, CHUNK, CHUNK), BF16),
            pltpu.VMEM((n_chunks * N_HEADS, CHUNK, LANES), F32),
            pltpu.VMEM((n_chunks * N_HEADS, CHUNK, LANES), F32),
            pltpu.VMEM((n_chunks * N_HEADS, HEAD_DIM, 2 * HEAD_DIM), F32),
            pltpu.VMEM((t, 2 * D_GROUP), F32),
        ] + ([pltpu.VMEM((t, D_GROUP), F32)] if final else []),
        compiler_params=_params(("parallel", "arbitrary")),
        name="mlstm_bwd" if reverse else "mlstm_fwd",
    )(*args)


def _sccm_kernel(bg_ref, cg_ref, hh_ref, u_ref, v_ref, scw_ref, ws_ref, bs_ref, gv_ref,
                 ysc_ref, ycm_ref, *, period):
    gated = cg_ref[...].astype(F32) * hh_ref[...].astype(F32)
    ysc_ref[...] = (bg_ref[...].astype(F32) * _dwconv3(gated, scw_ref, period)
                    ).astype(ysc_ref.dtype)
    v = _gelu_tanh(v_ref[...].astype(F32))
    vn = (v * lax.rsqrt(jnp.mean(v * v, axis=-1, keepdims=True) + EPS) * gv_ref[...]).astype(BF16)
    for n in range(v.shape[0] // CM_CHUNK):
        rows = slice(n * CM_CHUNK, (n + 1) * CM_CHUNK)
        for g in range(N_HEADS):
            sl = slice(g * HEAD_DIM, (g + 1) * HEAD_DIM)
            sv = jnp.dot(ws_ref[g], vn[rows, sl], preferred_element_type=F32) + bs_ref[:, sl]
            ycm_ref[rows, sl] = (_gelu_tanh(u_ref[rows, sl].astype(F32)) * sv
                                 ).astype(ycm_ref.dtype)


def _sc_cm_mixers(p2, sc_w, ws, bs_b, gv, *, t, period):
    m = p2.shape[0]
    col = lambda cb: pl.BlockSpec((t, D_GROUP), lambda i: (i, cb))
    const = lambda shape: pl.BlockSpec(shape, lambda i: (0,) * len(shape))
    return pl.pallas_call(
        functools.partial(_sccm_kernel, period=period),
        grid=(m // t,),
        in_specs=[col(C_SB), col(C_SC), col(C_SH), col(C_CU), col(C_CV),
                  const((3, D_GROUP)), const((N_HEADS, CM_CHUNK, CM_CHUNK)),
                  const((CM_CHUNK, D_GROUP)), const((1, D_GROUP))],
        out_specs=[pl.BlockSpec((t, D_GROUP), lambda i: (i, 0))] * 2,
        out_shape=[jax.ShapeDtypeStruct((m, D_GROUP), BF16)] * 2,
        compiler_params=_params(("parallel",)),
        name="sc_cm_mixers",
    )(p2, p2, p2, p2, p2, sc_w, ws, bs_b, gv)


def _outproj_kernel(y0_ref, y1_ref, y2_ref, y3_ref, w_ref, x_ref, g_ref, o_ref):
    acc = jnp.dot(y0_ref[...], w_ref[0:D_GROUP, :], preferred_element_type=F32)
    for n, y_ref in enumerate((y1_ref, y2_ref, y3_ref), start=1):
        acc = acc + jnp.dot(y_ref[...], w_ref[n * D_GROUP:(n + 1) * D_GROUP, :],
                            preferred_element_type=F32)
    o_ref[...] = x_ref[...] + g_ref[...] * acc


def _out_projection(ys, w_out, x2, mod, seq_len, tm):
    m, d = x2.shape
    y_spec = pl.BlockSpec((tm, D_GROUP), lambda i: (i, 0))
    return pl.pallas_call(
        _outproj_kernel,
        grid=(m // tm,),
        in_specs=[y_spec] * N_MIXERS + [
            pl.BlockSpec((d, d), lambda i: (0, 0)),
            pl.BlockSpec((tm, d), lambda i: (i, 0)),
            _row_spec(2, seq_len, tm, d),
        ],
        out_specs=pl.BlockSpec((tm, d), lambda i: (i, 0)),
        out_shape=jax.ShapeDtypeStruct((m, d), F32),
        compiler_params=_params(("parallel",)),
        name="out_projection",
    )(*ys, w_out, x2, mod)


def _ffn_kernel(*refs, period, final):
    if final:
        (x_ref, g_ref, sh_ref, sc_ref, gate_ref, wa_ref, wv_ref, cw_ref, wd_ref, fn_ref,
         o_ref, hn_ref, acc_ref) = refs
    else:
        (x_ref, g_ref, sh_ref, sc_ref, gate_ref, wa_ref, wv_ref, cw_ref, wd_ref,
         o_ref, hn_ref, acc_ref) = refs
    j = pl.program_id(1)

    @pl.when(j == 0)
    def _():
        hn_ref[...] = _modulated_norm(x_ref[...], g_ref[...], sc_ref[...], sh_ref[...]).astype(BF16)
        acc_ref[...] = jnp.zeros_like(acc_ref)

    hn = hn_ref[...]
    a = _dwconv3(jnp.dot(hn, wa_ref[...], preferred_element_type=F32), cw_ref, period)
    v = jnp.dot(hn, wv_ref[...], preferred_element_type=F32)
    hid = (a * _sigmoid(a) * v).astype(BF16)
    acc_ref[...] += jnp.dot(hid, wd_ref[...], preferred_element_type=F32)

    @pl.when(j == pl.num_programs(1) - 1)
    def _():
        y = x_ref[...] + gate_ref[...] * acc_ref[...]
        if final:
            y = y * lax.rsqrt(jnp.mean(y * y, axis=-1, keepdims=True) + EPS) * fn_ref[...]
        o_ref[...] = y


def _conv_ffn(x2, gain, mod, w_up, w_conv, w_down, final_gain, seq_len, period, tm, tf=512):
    m, d = x2.shape
    d_ff = w_down.shape[0]
    n_f = d_ff // tf
    final = final_gain is not None
    in_specs = [
        pl.BlockSpec((tm, d), lambda i, j: (i, 0)),
        pl.BlockSpec((1, d), lambda i, j: (0, 0)),
        _row_spec(3, seq_len, tm, d), _row_spec(4, seq_len, tm, d), _row_spec(5, seq_len, tm, d),
        pl.BlockSpec((d, tf), lambda i, j: (0, j)),
        pl.BlockSpec((d, tf), lambda i, j: (0, j + n_f)),
        pl.BlockSpec((3, tf), lambda i, j: (0, j)),
        pl.BlockSpec((tf, d), lambda i, j: (j, 0)),
    ]
    args = [x2, gain.reshape(1, d), mod, mod, mod, w_up, w_up, w_conv, w_down]
    if final:
        in_specs.append(pl.BlockSpec((1, d), lambda i, j: (0, 0)))
        args.append(final_gain.reshape(1, d))
    return pl.pallas_call(
        functools.partial(_ffn_kernel, period=period, final=final),
        grid=(m // tm, n_f),
        in_specs=in_specs,
        out_specs=pl.BlockSpec((tm, d), lambda i, j: (i, 0)),
        out_shape=jax.ShapeDtypeStruct((m, d), F32),
        scratch_shapes=[pltpu.VMEM((tm, d), BF16), pltpu.VMEM((tm, d), F32)],
        compiler_params=_params(("parallel", "arbitrary")),
        name="conv_ffn",
    )(*args)


def _relayout_w_in(w):
    d = w.shape[0]
    gla = 4 * D_GROUP + 2 * GLA_LOWRANK
    ml = 4 * D_GROUP + 4 * N_HEADS
    g0, m0 = 0, gla
    wide = jnp.concatenate([w[:, g0:g0 + 4 * D_GROUP], w[:, m0:m0 + 4 * D_GROUP],
                            w[:, m0 + ml:]], axis=1)
    small = jnp.concatenate([
        w[:, g0 + 4 * D_GROUP:g0 + gla], w[:, m0 + 4 * D_GROUP:m0 + ml],
        jnp.zeros((d, SMALL_COLS - 2 * GLA_LOWRANK - 4 * N_HEADS), w.dtype)], axis=1)
    return wide.astype(BF16), small.astype(BF16)


def _mixers(p2, ps2, bsz, seq_len, prm, states, *, t_scan, period, need_out):
    p3 = p2.reshape(bsz, seq_len, -1)
    ps3 = ps2.reshape(bsz, seq_len, SMALL_COLS)
    (gs_f, gs_b), (mc_f, mm_f, mc_b, mm_b) = states
    o_f, gs_f = _gla_scan(p3, ps3, prm["wa_f"], prm["ba_f"], gs_f, None, None,
                          reverse=False, t=t_scan)
    y_gla, gs_b = _gla_scan(p3, ps3, prm["wa_b"], prm["ba_b"], gs_b, prm["gla_norm"], o_f,
                            reverse=True, t=t_scan)
    h_f, mc_f, mm_f = _mlstm_scan(p3, ps3, prm["ml_conv"], prm["ml_gate_b"], mc_f, mm_f,
                                  None, None, reverse=False, t=t_scan, period=period)
    y_ml, mc_b, mm_b = _mlstm_scan(p3, ps3, prm["ml_conv"], prm["ml_gate_b"], mc_b, mm_b,
                                   prm["ml_norm"], h_f, reverse=True, t=t_scan, period=period)
    new_states = ((gs_f, gs_b), (mc_f, mm_f, mc_b, mm_b))
    if not need_out:
        return None, new_states
    y_sc, y_cm = _sc_cm_mixers(p2, prm["sc_conv"], prm["cm_ws"], prm["cm_bs"], prm["cm_norm"],
                               t=min(512, seq_len), period=period)
    m = bsz * seq_len
    return (y_gla.reshape(m, D_GROUP), y_ml.reshape(m, D_GROUP), y_sc, y_cm), new_states


def kernel(x, c, ctx, c_ctx, w_mod, b_mod, norm1, norm2, w_in, w_out, gla_wa2, gla_ba, gla_norm,
           mlstm_conv, mlstm_gate_b, mlstm_norm, sc_conv, cm_ws, cm_bs, cm_norm,
           ffn_up, ffn_conv, ffn_down, final_norm):
    bsz, seq_len, d = x.shape
    ctx_len = ctx.shape[1]
    depth = w_mod.shape[0]
    assert seq_len % 512 == 0 and ctx_len % CM_CHUNK == 0 and bsz + 1 <= 8

    cond = jnp.zeros((8, d), F32).at[:bsz].set(c).at[bsz].set(c_ctx)
    mod = _modulation(cond, w_mod, b_mod).reshape(depth, 8, 6, d).transpose(0, 2, 1, 3)

    xl = x.reshape(bsz * seq_len, d)
    xc = ctx.reshape(bsz * ctx_len, d)
    tm_l, tm_c = 512, ctx_len
    tm_in = 1024

    zero_states = (
        (jnp.zeros((bsz, N_HEADS, HEAD_DIM, HEAD_DIM), F32),) * 2,
        (jnp.zeros((bsz, N_HEADS, HEAD_DIM, 2 * HEAD_DIM), F32), jnp.zeros((bsz, 8, LANES), F32)) * 2,
    )

    for l in range(depth):
        need_ctx = l < depth - 1
        mod_l = mod[l][:, :bsz, None, :]
        mod_c = jnp.broadcast_to(mod[l][:, bsz:bsz + 1, None, :], (6, bsz, 1, d))
        w_wide, w_small = _relayout_w_in(w_in[l])
        wa = jnp.zeros((2, SMALL_COLS, D_GROUP), F32)
        wa = wa.at[0, Z_FWD:Z_FWD + GLA_LOWRANK].set(gla_wa2[l, 0])
        wa = wa.at[1, Z_BWD:Z_BWD + GLA_LOWRANK].set(gla_wa2[l, 1])
        wa_hi = wa.astype(BF16)
        wa_lo = (wa - wa_hi.astype(F32)).astype(BF16)
        wa = jnp.concatenate([wa_hi, wa_hi, wa_lo], axis=1)
        gate_b = jnp.zeros((1, SMALL_COLS), F32).at[0, GATE0:GATE0 + 4 * N_HEADS].set(
            mlstm_gate_b[l].reshape(-1))
        prm = dict(
            wa_f=wa[0], wa_b=wa[1], ba_f=gla_ba[l, 0:1], ba_b=gla_ba[l, 1:2],
            gla_norm=gla_norm[l].reshape(1, D_GROUP),
            ml_conv=mlstm_conv[l], ml_gate_b=gate_b, ml_norm=mlstm_norm[l].reshape(1, D_GROUP),
            sc_conv=sc_conv[l], cm_ws=cm_ws[l].astype(BF16),
            cm_bs=jnp.repeat(cm_bs[l].T, HEAD_DIM, axis=1),
            cm_norm=cm_norm[l].reshape(1, D_GROUP),
        )
        w_out_l = w_out[l].astype(BF16)
        w_up_l, w_down_l = ffn_up[l].astype(BF16), ffn_down[l].astype(BF16)

        pc, psc = _in_projection(xc, norm1[l], mod_c, w_wide, w_small, ctx_len, tm_c)
        yc, ctx_states = _mixers(pc, psc, bsz, ctx_len, prm, zero_states,
                                 t_scan=ctx_len, period=ctx_len, need_out=need_ctx)
        p_l, ps_l = _in_projection(xl, norm1[l], mod_l, w_wide, w_small, seq_len, tm_in)
        yl, _ = _mixers(p_l, ps_l, bsz, seq_len, prm, ctx_states,
                        t_scan=512, period=GRID_W, need_out=True)

        xl = _out_projection(yl, w_out_l, xl, mod_l, seq_len, tm_l)
        xl = _conv_ffn(xl, norm2[l], mod_l, w_up_l, ffn_conv[l], w_down_l,
                       final_norm if l == depth - 1 else None, seq_len, GRID_W, tm_l)
        if need_ctx:
            xc = _out_projection(yc, w_out_l, xc, mod_c, ctx_len, tm_c)
            xc = _conv_ffn(xc, norm2[l], mod_c, w_up_l, ffn_conv[l], w_down_l, None,
                           ctx_len, ctx_len, tm_c)
    return xl.reshape(bsz, seq_len, d)
```

```python
import functools

import jax
import jax.numpy as jnp
from jax import lax
from jax.experimental import pallas as pl
from jax.experimental.pallas import tpu as pltpu

F32 = jnp.float32
BF16 = jnp.bfloat16

GRID_W = 64
N_MIXERS = 4
HEAD_DIM = 128
N_HEADS = 4
D_GROUP = N_HEADS * HEAD_DIM
GLA_LOWRANK = 16
GLA_GATE_NORM = 16.0
CHUNK = 64
CM_CHUNK = 128
EPS = 1e-6
LANES = 128
SMALL_COLS = LANES
Z_FWD, Z_BWD, GATE0 = 0, GLA_LOWRANK, 2 * GLA_LOWRANK
VMEM_LIMIT = 56 * 1024 * 1024
INPROJ_VMEM_LIMIT = 60 * 1024 * 1024

(C_GQ, C_GK, C_GV, C_GG, C_MQK, C_MV, C_MO, C_SB, C_SC, C_SH, C_CU, C_CV) = (
    0, 1, 2, 3, 4, 6, 7, 8, 9, 10, 11, 12)
N_WIDE_BLOCKS = 13


def _params(sem):
    return pltpu.CompilerParams(dimension_semantics=sem, vmem_limit_bytes=VMEM_LIMIT)


def _sigmoid(x):
    return 1.0 / (1.0 + jnp.exp(-x))


def _log_sigmoid(x):
    return jnp.minimum(x, 0.0) - jnp.log(1.0 + jnp.exp(-jnp.abs(x)))


def _gelu_tanh(x):
    return x * (0.5 * (1.0 + jnp.tanh(0.7978845608028654 * (x + 0.044715 * (x * x * x)))))


def _shift_rows(x, period):
    n = x.shape[0]
    row = lax.broadcasted_iota(jnp.int32, (n, 1), 0) % period
    prev = jnp.where(row == 0, 0.0, pltpu.roll(x, 1, axis=0))
    nxt = jnp.where(row == period - 1, 0.0, pltpu.roll(x, n - 1, axis=0))
    return prev, nxt


def _dwconv3(x, w_ref, period):
    prev, nxt = _shift_rows(x, period)
    return w_ref[0:1, :] * prev + w_ref[1:2, :] * x + w_ref[2:3, :] * nxt


def _mod_kernel(c_ref, w_ref, b_ref, o_ref):
    s = c_ref[...]
    s = (s * _sigmoid(s)).astype(BF16)
    o_ref[...] = jnp.dot(s, w_ref[...].astype(BF16), preferred_element_type=F32) + b_ref[...]


def _modulation(cond, w_mod, b_mod, tn=1024):
    depth, d, n = w_mod.shape
    rows = cond.shape[0]
    return pl.pallas_call(
        _mod_kernel,
        grid=(depth, n // tn),
        in_specs=[
            pl.BlockSpec((rows, d), lambda l, j: (0, 0)),
            pl.BlockSpec((None, d, tn), lambda l, j: (l, 0, j)),
            pl.BlockSpec((None, 1, tn), lambda l, j: (l, 0, j)),
        ],
        out_specs=pl.BlockSpec((None, rows, tn), lambda l, j: (l, 0, j)),
        out_shape=jax.ShapeDtypeStruct((depth, rows, n), F32),
        compiler_params=_params(("parallel", "parallel")),
        name="modulation",
    )(cond, w_mod, b_mod.reshape(depth, 1, n))


def _modulated_norm(x, g, sc, sh):
    r = lax.rsqrt(jnp.mean(x * x, axis=-1, keepdims=True) + EPS)
    return (x * r * g) * (1.0 + sc) + sh


def _inproj_kernel(x_ref, g_ref, sh_ref, sc_ref, w_ref, ws_ref, p_ref, ps_ref, *, tn):
    hn = _modulated_norm(x_ref[...], g_ref[...], sc_ref[...], sh_ref[...]).astype(BF16)
    ps_ref[...] = jnp.dot(hn, ws_ref[...], preferred_element_type=F32)
    for j in range(w_ref.shape[1] // tn):
        cols = slice(j * tn, (j + 1) * tn)
        p_ref[:, cols] = jnp.dot(hn, w_ref[:, cols], preferred_element_type=F32
                                 ).astype(p_ref.dtype)


def _row_spec(piece, seq_len, tm, d):
    return pl.BlockSpec((None, None, 1, d), lambda i, *_: (piece, (i * tm) // seq_len, 0, 0))


def _in_projection(x2, gain, mod, w_wide, w_small, layer, seq_len, tm, tn=D_GROUP):
    m, d = x2.shape
    n = w_wide.shape[2]
    resident = lambda shape: pl.BlockSpec((None,) + shape, lambda i: (layer, 0, 0),
                                          pipeline_mode=pl.Buffered(1))
    return pl.pallas_call(
        functools.partial(_inproj_kernel, tn=tn),
        grid=(m // tm,),
        in_specs=[
            pl.BlockSpec((tm, d), lambda i: (i, 0)),
            pl.BlockSpec((1, d), lambda i: (0, 0)),
            _row_spec(0, seq_len, tm, d),
            _row_spec(1, seq_len, tm, d),
            resident((d, n)),
            resident((d, SMALL_COLS)),
        ],
        out_specs=[
            pl.BlockSpec((tm, n), lambda i: (i, 0)),
            pl.BlockSpec((tm, SMALL_COLS), lambda i: (i, 0)),
        ],
        out_shape=[
            jax.ShapeDtypeStruct((m, n), BF16),
            jax.ShapeDtypeStruct((m, SMALL_COLS), F32),
        ],
        compiler_params=pltpu.CompilerParams(dimension_semantics=("parallel",),
                                             vmem_limit_bytes=INPROJ_VMEM_LIMIT),
        name="in_projection",
    )(x2, gain.reshape(1, d), mod, mod, w_wide, w_small)


def _tri(reverse):
    r = lax.broadcasted_iota(jnp.int32, (CHUNK, CHUNK), 0)
    c = lax.broadcasted_iota(jnp.int32, (CHUNK, CHUNK), 1)
    return (c >= r) if reverse else (c <= r)


def _tri3(reverse):
    r = lax.broadcasted_iota(jnp.int32, (CHUNK, 3 * CHUNK), 0)
    c = lax.broadcasted_iota(jnp.int32, (CHUNK, 3 * CHUNK), 1) % CHUNK
    return ((c >= r) if reverse else (c <= r)).astype(BF16)


def _split2(x):
    hi = x.astype(BF16)
    return hi, (x - hi.astype(F32)).astype(BF16)


def _chunk_cumsum(tri3, x):
    x1 = x.astype(BF16)
    r1 = x - x1.astype(F32)
    x2 = r1.astype(BF16)
    x3 = (r1 - x2.astype(F32)).astype(BF16)
    return jnp.dot(tri3, jnp.concatenate([x1, x2, x3], axis=0), preferred_element_type=F32)


def _head_norm_gate(o, gain, gate):
    r = lax.rsqrt(jnp.mean(o * o, axis=-1, keepdims=True) + EPS)
    return o * r * gain * gate


def _gla_kernel(*refs, reverse, final, n_chunks):
    if final:
        (q_ref, k_ref, v_ref, z_ref, wa_ref, ba_ref, s0_ref, of_ref, g_ref, gn_ref,
         out_ref, sfin_ref, st_ref, qin_ref, qrel_ref, krel_ref, dec_ref, att_ref, kv_ref,
         acc_ref) = refs
    else:
        (q_ref, k_ref, v_ref, z_ref, wa_ref, ba_ref, s0_ref,
         out_ref, sfin_ref, st_ref, qin_ref, qrel_ref, krel_ref, dec_ref, att_ref,
         kv_ref) = refs
        acc_ref = out_ref
    step = pl.program_id(1)

    @pl.when(step == 0)
    def _():
        st_ref[...] = s0_ref[...]

    mask = _tri(reverse)
    tri3 = _tri3(reverse)
    last = 0 if reverse else CHUNK - 1
    scale = HEAD_DIM ** -0.5
    chunk_rows = [slice(c * CHUNK, (c + 1) * CHUNK) for c in range(n_chunks)]
    head_cols = [slice(h * HEAD_DIM, (h + 1) * HEAD_DIM) for h in range(N_HEADS)]

    z_hi, z_lo = _split2(z_ref[...])
    pre = jnp.dot(jnp.concatenate([z_hi, z_lo, z_hi], axis=1), wa_ref[...],
                  preferred_element_type=F32) + ba_ref[...]
    log_a = _log_sigmoid(pre) * (1.0 / GLA_GATE_NORM)

    for c, rows in enumerate(chunk_rows):
        b_cum = _chunk_cumsum(tri3, log_a[rows, :])
        b_last = b_cum[last:last + 1, :]
        q = q_ref[rows, :].astype(F32) * scale
        qin_ref[rows, :] = (q * jnp.exp(b_cum)).astype(BF16)
        qrel_ref[rows, :] = (q * jnp.exp(b_cum - b_last)).astype(BF16)
        krel_ref[rows, :] = (k_ref[rows, :].astype(F32) * jnp.exp(b_last - b_cum)).astype(BF16)
        dec_ref[c] = jnp.exp(b_last)

    for c, rows in enumerate(chunk_rows):
        for h, sl in enumerate(head_cols):
            att = lax.dot_general(qrel_ref[rows, sl], krel_ref[rows, sl],
                                  (((1,), (1,)), ((), ())), preferred_element_type=F32)
            att_ref[c * N_HEADS + h] = jnp.where(mask, att, 0.0).astype(BF16)

    for c, rows in enumerate(chunk_rows):
        for h, sl in enumerate(head_cols):
            v = v_ref[rows, sl]
            acc_ref[rows, sl] = jnp.dot(att_ref[c * N_HEADS + h], v, preferred_element_type=F32)
            kv_ref[c * N_HEADS + h] = lax.dot_general(
                v, krel_ref[rows, sl], (((0,), (0,)), ((), ())), preferred_element_type=F32)

    for c in (reversed(range(n_chunks)) if reverse else range(n_chunks)):
        rows = chunk_rows[c]
        for h, sl in enumerate(head_cols):
            s_t = st_ref[h]
            acc_ref[rows, sl] += lax.dot_general(qin_ref[rows, sl], s_t.astype(BF16),
                                                 (((1,), (1,)), ((), ())),
                                                 preferred_element_type=F32)
            st_ref[h] = s_t * dec_ref[c][:, sl] + kv_ref[c * N_HEADS + h]

    if final:
        for sl in head_cols:
            g = g_ref[:, sl].astype(F32)
            out_ref[:, sl] = _head_norm_gate(acc_ref[:, sl] + of_ref[:, sl], gn_ref[:, sl],
                                             g * _sigmoid(g)).astype(out_ref.dtype)

    @pl.when(step == pl.num_programs(1) - 1)
    def _():
        sfin_ref[...] = st_ref[...]


def _scan_block_spec(t, cols, col_block, n_blocks, reverse):
    if reverse:
        return pl.BlockSpec((None, t, cols), lambda b, i: (b, n_blocks - 1 - i, col_block))
    return pl.BlockSpec((None, t, cols), lambda b, i: (b, i, col_block))


def _gla_scan(p3, ps3, wa_pad, ba, s0, gn, o_fwd, *, reverse, t):
    bsz, seq_len, _ = p3.shape
    n_blocks = seq_len // t
    n_chunks = t // CHUNK
    final = o_fwd is not None
    spec = functools.partial(_scan_block_spec, n_blocks=n_blocks, reverse=reverse)
    const2 = lambda shape: pl.BlockSpec(shape, lambda b, i: (0, 0))
    state_spec = pl.BlockSpec((None, N_HEADS, HEAD_DIM, HEAD_DIM), lambda b, i: (b, 0, 0, 0))
    in_specs = [
        spec(t, D_GROUP, C_GQ), spec(t, D_GROUP, C_GK), spec(t, D_GROUP, C_GV),
        spec(t, SMALL_COLS, 0),
        const2((3 * SMALL_COLS, D_GROUP)), const2((1, D_GROUP)),
        state_spec,
    ]
    args = [p3, p3, p3, ps3, wa_pad, ba, s0]
    if final:
        in_specs += [spec(t, D_GROUP, 0), spec(t, D_GROUP, C_GG), const2((1, D_GROUP))]
        args += [o_fwd, p3, gn]
    return pl.pallas_call(
        functools.partial(_gla_kernel, reverse=reverse, final=final, n_chunks=n_chunks),
        grid=(bsz, n_blocks),
        in_specs=in_specs,
        out_specs=[spec(t, D_GROUP, 0), state_spec],
        out_shape=[
            jax.ShapeDtypeStruct((bsz, seq_len, D_GROUP), BF16 if final else F32),
            jax.ShapeDtypeStruct((bsz, N_HEADS, HEAD_DIM, HEAD_DIM), F32),
        ],
        scratch_shapes=[
            pltpu.VMEM((N_HEADS, HEAD_DIM, HEAD_DIM), F32),
            pltpu.VMEM((t, D_GROUP), BF16), pltpu.VMEM((t, D_GROUP), BF16),
            pltpu.VMEM((t, D_GROUP), BF16),
            pltpu.VMEM((n_chunks, 1, D_GROUP), F32),
            pltpu.VMEM((n_chunks * N_HEADS, CHUNK, CHUNK), BF16),
            pltpu.VMEM((n_chunks * N_HEADS, HEAD_DIM, HEAD_DIM), F32),
        ] + ([pltpu.VMEM((t, D_GROUP), F32)] if final else []),
        compiler_params=_params(("parallel", "arbitrary")),
        name="gla_bwd" if reverse else "gla_fwd",
    )(*args)


def _mlstm_kernel(*refs, reverse, final, n_chunks, period, lane0):
    if final:
        (qkc_ref, v_ref, gt_ref, gb_ref, c0_ref, m0_ref, hf_ref, og_ref, gn_ref,
         out_ref, cfin_ref, mfin_ref, c_ref, m_ref, fcol_ref, jcol_ref, jt_ref,
         mprev_ref, ulast_ref, p_ref, s_ref, wi_ref, en_ref, kwt_ref, kv_ref, r_ref,
         acc_ref) = refs
    else:
        (qk_ref, v_ref, gt_ref, cw_ref, gb_ref, c0_ref, m0_ref,
         out_ref, cfin_ref, mfin_ref, qkc_ref, c_ref, m_ref, fcol_ref, jcol_ref, jt_ref,
         mprev_ref, ulast_ref, p_ref, s_ref, wi_ref, en_ref, kwt_ref, kv_ref, r_ref) = refs
        acc_ref = out_ref
        qkc_ref[...] = _dwconv3(qk_ref[...].astype(F32), cw_ref, period).astype(BF16)
    step = pl.program_id(1)

    @pl.when(step == 0)
    def _():
        c_ref[...] = c0_ref[...]
        m_ref[...] = m0_ref[...]

    mask = _tri(reverse)
    tri3 = _tri3(reverse)
    last = 0 if reverse else CHUNK - 1
    ones_col = (lax.broadcasted_iota(jnp.int32, (CHUNK, HEAD_DIM), 1) == 0).astype(BF16)
    chunk_rows = [slice(c * CHUNK, (c + 1) * CHUNK) for c in range(n_chunks)]
    head_cols = [slice(h * HEAD_DIM, (h + 1) * HEAD_DIM) for h in range(N_HEADS)]
    key_cols = [slice(D_GROUP + h * HEAD_DIM, D_GROUP + (h + 1) * HEAD_DIM) for h in range(N_HEADS)]
    f_lane = [lane0 + N_HEADS + h for h in range(N_HEADS)]
    scan_order = list(reversed(range(n_chunks))) if reverse else list(range(n_chunks))

    gates = gt_ref[...] + gb_ref[...]
    log_f = _log_sigmoid(gates)
    i_al = pltpu.roll(gates, N_HEADS, axis=1)
    j_max, f_tot = [], []
    for c, rows in enumerate(chunk_rows):
        f_col = _chunk_cumsum(tri3, log_f[rows, :])
        j_col = i_al[rows, :] - f_col
        fcol_ref[rows, :] = f_col
        jcol_ref[rows, :] = j_col
        jt_ref[c] = j_col.T
        j_max.append(jnp.max(j_col, axis=0, keepdims=True))
        f_tot.append(f_col[last:last + 1, :])

    m_run = m_ref[0:1, :]
    for c in scan_order:
        mprev_ref[c] = m_run
        u_last = jnp.maximum(m_run, j_max[c])
        ulast_ref[c] = u_last
        m_run = f_tot[c] + u_last
    m_ref[0:1, :] = m_run

    for c, rows in enumerate(chunk_rows):
        for h in range(N_HEADS):
            lf = f_lane[h]
            idx = c * N_HEADS + h
            a = jnp.where(mask, jt_ref[c][lf:lf + 1, :], -jnp.inf)
            m_prev = mprev_ref[c][:, lf:lf + 1]
            u = jnp.maximum(jnp.max(a, axis=-1, keepdims=True), m_prev)
            p_ref[idx] = jnp.exp(a - u)
            wi_ref[idx] = jnp.broadcast_to(jnp.exp(m_prev - u), (CHUNK, LANES))
            en_ref[idx] = jnp.broadcast_to(
                jnp.exp(-(fcol_ref[rows, lf:lf + 1] + u)), (CHUNK, LANES))

    for c, rows in enumerate(chunk_rows):
        for h in range(N_HEADS):
            idx = c * N_HEADS + h
            s = lax.dot_general(qkc_ref[rows, head_cols[h]], qkc_ref[rows, key_cols[h]],
                                (((1,), (1,)), ((), ())), preferred_element_type=F32)
            s_ref[idx] = (s * p_ref[idx]).astype(BF16)

    for c, rows in enumerate(chunk_rows):
        for h in range(N_HEADS):
            lf = f_lane[h]
            w_k = jnp.exp(jcol_ref[rows, lf:lf + 1] - ulast_ref[c][:, lf:lf + 1])
            kwt_ref[c * N_HEADS + h] = (qkc_ref[rows, key_cols[h]].astype(F32) * w_k
                                        ).T.astype(BF16)

    for c, rows in enumerate(chunk_rows):
        for h in range(N_HEADS):
            idx = c * N_HEADS + h
            v_aug = jnp.concatenate([v_ref[rows, head_cols[h]], ones_col], axis=1)
            r_ref[rows, 2 * h * HEAD_DIM:2 * (h + 1) * HEAD_DIM] = jnp.dot(
                s_ref[idx], v_aug, preferred_element_type=F32)
            kv_ref[idx] = jnp.dot(kwt_ref[idx], v_aug, preferred_element_type=F32)

    for c in scan_order:
        rows = chunk_rows[c]
        decay = jnp.exp(mprev_ref[c] - ulast_ref[c])
        for h in range(N_HEADS):
            lf = f_lane[h]
            idx = c * N_HEADS + h
            c_aug = c_ref[h]
            wi = wi_ref[idx]
            r = jnp.concatenate([wi, wi], axis=1) * jnp.dot(
                qkc_ref[rows, head_cols[h]], c_aug.astype(BF16),
                preferred_element_type=F32) + r_ref[rows, 2 * h * HEAD_DIM:2 * (h + 1) * HEAD_DIM]
            den = jnp.maximum(jnp.abs(r[:, HEAD_DIM:HEAD_DIM + 1]), en_ref[idx][:, 0:1])
            acc_ref[rows, head_cols[h]] = r[:, :HEAD_DIM] * (1.0 / den)
            c_ref[h] = c_aug * decay[:, lf:lf + 1] + kv_ref[idx]

    if final:
        for sl in head_cols:
            out_ref[:, sl] = _head_norm_gate(acc_ref[:, sl] + hf_ref[:, sl], gn_ref[:, sl],
                                             _sigmoid(og_ref[:, sl].astype(F32))
                                             ).astype(out_ref.dtype)

    @pl.when(step == pl.num_programs(1) - 1)
    def _():
        cfin_ref[...] = c_ref[...]
        mfin_ref[...] = m_ref[...]


def _mlstm_scan(p3, ps3, conv_w, gate_b, c0, m0, gn, fwd, *, reverse, t, period):
    bsz, seq_len, _ = p3.shape
    n_blocks = seq_len // t
    n_chunks = t // CHUNK
    final = fwd is not None
    spec = functools.partial(_scan_block_spec, n_blocks=n_blocks, reverse=reverse)
    const2 = lambda shape: pl.BlockSpec(shape, lambda b, i: (0, 0))
    c_spec = pl.BlockSpec((None, N_HEADS, HEAD_DIM, 2 * HEAD_DIM), lambda b, i: (b, 0, 0, 0))
    m_spec = pl.BlockSpec((None, 8, LANES), lambda b, i: (b, 0, 0))
    out_specs = [spec(t, D_GROUP, 0), c_spec, m_spec]
    out_shape = [
        jax.ShapeDtypeStruct((bsz, seq_len, D_GROUP), BF16 if final else F32),
        jax.ShapeDtypeStruct((bsz, N_HEADS, HEAD_DIM, 2 * HEAD_DIM), F32),
        jax.ShapeDtypeStruct((bsz, 8, LANES), F32),
    ]
    if final:
        h_fwd, qkc = fwd
        in_specs = [
            spec(t, 2 * D_GROUP, 0), spec(t, D_GROUP, C_MV), spec(t, SMALL_COLS, 0),
            const2((1, SMALL_COLS)), c_spec, m_spec,
            spec(t, D_GROUP, 0), spec(t, D_GROUP, C_MO), const2((1, D_GROUP)),
        ]
        args = [qkc, p3, ps3, gate_b, c0, m0, h_fwd, p3, gn]
    else:
        in_specs = [
            spec(t, 2 * D_GROUP, C_MQK // 2), spec(t, D_GROUP, C_MV), spec(t, SMALL_COLS, 0),
            const2((3, 2 * D_GROUP)), const2((1, SMALL_COLS)), c_spec, m_spec,
        ]
        args = [p3, p3, ps3, conv_w, gate_b, c0, m0]
        out_specs.append(spec(t, 2 * D_GROUP, 0))
        out_shape.append(jax.ShapeDtypeStruct((bsz, seq_len, 2 * D_GROUP), BF16))
    lane0 = GATE0 + (2 * N_HEADS if reverse else 0)
    units = n_chunks * N_HEADS
    return pl.pallas_call(
        functools.partial(_mlstm_kernel, reverse=reverse, final=final, n_chunks=n_chunks,
                          period=period, lane0=lane0),
        grid=(bsz, n_blocks),
        in_specs=in_specs,
        out_specs=out_specs,
        out_shape=out_shape,
        scratch_shapes=[
            pltpu.VMEM((N_HEADS, HEAD_DIM, 2 * HEAD_DIM), F32),
            pltpu.VMEM((8, LANES), F32),
            pltpu.VMEM((t, SMALL_COLS), F32),
            pltpu.VMEM((t, SMALL_COLS), F32),
            pltpu.VMEM((n_chunks, SMALL_COLS, CHUNK), F32),
            pltpu.VMEM((n_chunks, 1, SMALL_COLS), F32),
            pltpu.VMEM((n_chunks, 1, SMALL_COLS), F32),
            pltpu.VMEM((units, CHUNK, CHUNK), F32),
            pltpu.VMEM((units, CHUNK, CHUNK), BF16),
            pltpu.VMEM((units, CHUNK, LANES), F32),
            pltpu.VMEM((units, CHUNK, LANES), F32),
            pltpu.VMEM((units, HEAD_DIM, CHUNK), BF16),
            pltpu.VMEM((units, HEAD_DIM, 2 * HEAD_DIM), F32),
            pltpu.VMEM((t, 2 * D_GROUP), F32),
        ] + ([pltpu.VMEM((t, D_GROUP), F32)] if final else []),
        compiler_params=_params(("parallel", "arbitrary")),
        name="mlstm_bwd" if reverse else "mlstm_fwd",
    )(*args)


def _sccm_kernel(bg_ref, cg_ref, hh_ref, u_ref, v_ref, scw_ref, ws_ref, bs_ref, gv_ref,
                 ysc_ref, ycm_ref, *, period):
    gated = cg_ref[...].astype(F32) * hh_ref[...].astype(F32)
    ysc_ref[...] = (bg_ref[...].astype(F32) * _dwconv3(gated, scw_ref, period)
                    ).astype(ysc_ref.dtype)
    v = _gelu_tanh(v_ref[...].astype(F32))
    vn = (v * lax.rsqrt(jnp.mean(v * v, axis=-1, keepdims=True) + EPS) * gv_ref[...]).astype(BF16)
    for n in range(v.shape[0] // CM_CHUNK):
        rows = slice(n * CM_CHUNK, (n + 1) * CM_CHUNK)
        for g in range(N_HEADS):
            sl = slice(g * HEAD_DIM, (g + 1) * HEAD_DIM)
            sv = jnp.dot(ws_ref[g], vn[rows, sl], preferred_element_type=F32) + bs_ref[:, sl]
            ycm_ref[rows, sl] = (_gelu_tanh(u_ref[rows, sl].astype(F32)) * sv
                                 ).astype(ycm_ref.dtype)


def _sc_cm_mixers(p2, sc_w, ws, bs_b, gv, *, t, period):
    m = p2.shape[0]
    col = lambda cb: pl.BlockSpec((t, D_GROUP), lambda i: (i, cb))
    const = lambda shape: pl.BlockSpec(shape, lambda i: (0,) * len(shape))
    return pl.pallas_call(
        functools.partial(_sccm_kernel, period=period),
        grid=(m // t,),
        in_specs=[col(C_SB), col(C_SC), col(C_SH), col(C_CU), col(C_CV),
                  const((3, D_GROUP)), const((N_HEADS, CM_CHUNK, CM_CHUNK)),
                  const((CM_CHUNK, D_GROUP)), const((1, D_GROUP))],
        out_specs=[pl.BlockSpec((t, D_GROUP), lambda i: (i, 0))] * 2,
        out_shape=[jax.ShapeDtypeStruct((m, D_GROUP), BF16)] * 2,
        compiler_params=_params(("parallel",)),
        name="sc_cm_mixers",
    )(p2, p2, p2, p2, p2, sc_w, ws, bs_b, gv)


def _outproj_kernel(y0_ref, y1_ref, y2_ref, y3_ref, w_ref, x_ref, g_ref, o_ref):
    acc = jnp.dot(y0_ref[...], w_ref[0:D_GROUP, :], preferred_element_type=F32)
    for n, y_ref in enumerate((y1_ref, y2_ref, y3_ref), start=1):
        acc = acc + jnp.dot(y_ref[...], w_ref[n * D_GROUP:(n + 1) * D_GROUP, :],
                            preferred_element_type=F32)
    o_ref[...] = x_ref[...] + g_ref[...] * acc


def _out_projection(ys, w_out, x2, mod, seq_len, tm):
    m, d = x2.shape
    y_spec = pl.BlockSpec((tm, D_GROUP), lambda i: (i, 0))
    return pl.pallas_call(
        _outproj_kernel,
        grid=(m // tm,),
        in_specs=[y_spec] * N_MIXERS + [
            pl.BlockSpec((d, d), lambda i: (0, 0)),
            pl.BlockSpec((tm, d), lambda i: (i, 0)),
            _row_spec(2, seq_len, tm, d),
        ],
        out_specs=pl.BlockSpec((tm, d), lambda i: (i, 0)),
        out_shape=jax.ShapeDtypeStruct((m, d), F32),
        compiler_params=_params(("parallel",)),
        name="out_projection",
    )(*ys, w_out, x2, mod)


def _ffn_kernel(*refs, period, final):
    if final:
        (x_ref, g_ref, sh_ref, sc_ref, gate_ref, wa_ref, wv_ref, cw_ref, wd_ref, fn_ref,
         o_ref, hn_ref, acc_ref) = refs
    else:
        (x_ref, g_ref, sh_ref, sc_ref, gate_ref, wa_ref, wv_ref, cw_ref, wd_ref,
         o_ref, hn_ref, acc_ref) = refs
    j = pl.program_id(1)

    @pl.when(j == 0)
    def _():
        hn_ref[...] = _modulated_norm(x_ref[...], g_ref[...], sc_ref[...], sh_ref[...]).astype(BF16)
        acc_ref[...] = jnp.zeros_like(acc_ref)

    hn = hn_ref[...]
    a = _dwconv3(jnp.dot(hn, wa_ref[...], preferred_element_type=F32), cw_ref, period)
    v = jnp.dot(hn, wv_ref[...], preferred_element_type=F32)
    hid = (a * _sigmoid(a) * v).astype(BF16)
    acc_ref[...] += jnp.dot(hid, wd_ref[...], preferred_element_type=F32)

    @pl.when(j == pl.num_programs(1) - 1)
    def _():
        y = x_ref[...] + gate_ref[...] * acc_ref[...]
        if final:
            y = y * lax.rsqrt(jnp.mean(y * y, axis=-1, keepdims=True) + EPS) * fn_ref[...]
        o_ref[...] = y


def _conv_ffn(x2, gain, mod, w_up, w_conv, w_down, final_gain, seq_len, period, tm, tf=512):
    m, d = x2.shape
    d_ff = w_down.shape[0]
    n_f = d_ff // tf
    final = final_gain is not None
    in_specs = [
        pl.BlockSpec((tm, d), lambda i, j: (i, 0)),
        pl.BlockSpec((1, d), lambda i, j: (0, 0)),
        _row_spec(3, seq_len, tm, d), _row_spec(4, seq_len, tm, d), _row_spec(5, seq_len, tm, d),
        pl.BlockSpec((d, tf), lambda i, j: (0, j)),
        pl.BlockSpec((d, tf), lambda i, j: (0, j + n_f)),
        pl.BlockSpec((3, tf), lambda i, j: (0, j)),
        pl.BlockSpec((tf, d), lambda i, j: (j, 0)),
    ]
    args = [x2, gain.reshape(1, d), mod, mod, mod, w_up, w_up, w_conv, w_down]
    if final:
        in_specs.append(pl.BlockSpec((1, d), lambda i, j: (0, 0)))
        args.append(final_gain.reshape(1, d))
    return pl.pallas_call(
        functools.partial(_ffn_kernel, period=period, final=final),
        grid=(m // tm, n_f),
        in_specs=in_specs,
        out_specs=pl.BlockSpec((tm, d), lambda i, j: (i, 0)),
        out_shape=jax.ShapeDtypeStruct((m, d), F32),
        scratch_shapes=[pltpu.VMEM((tm, d), BF16), pltpu.VMEM((tm, d), F32)],
        compiler_params=_params(("parallel", "arbitrary")),
        name="conv_ffn",
    )(*args)


def _relayout_kernel(w_ref, wide_ref, small_ref):
    gla = 4 * D_GROUP + 2 * GLA_LOWRANK
    ml = 4 * D_GROUP + 4 * N_HEADS
    w = w_ref[...]
    wide_ref[:, :4 * D_GROUP] = w[:, :4 * D_GROUP].astype(BF16)
    wide_ref[:, 4 * D_GROUP:8 * D_GROUP] = w[:, gla:gla + 4 * D_GROUP].astype(BF16)
    wide_ref[:, 8 * D_GROUP:] = w[:, gla + ml:].astype(BF16)
    small_ref[...] = jnp.concatenate([
        w[:, 4 * D_GROUP:gla], w[:, gla + 4 * D_GROUP:gla + ml],
        jnp.zeros((w.shape[0], SMALL_COLS - 2 * GLA_LOWRANK - 4 * N_HEADS), F32)],
        axis=1).astype(BF16)


def _relayout_w_in(w_in, tr=256):
    depth, d, n = w_in.shape
    n_wide = N_WIDE_BLOCKS * D_GROUP
    return pl.pallas_call(
        _relayout_kernel,
        grid=(depth, d // tr),
        in_specs=[pl.BlockSpec((None, tr, n), lambda l, i: (l, i, 0))],
        out_specs=[pl.BlockSpec((None, tr, n_wide), lambda l, i: (l, i, 0)),
                   pl.BlockSpec((None, tr, SMALL_COLS), lambda l, i: (l, i, 0))],
        out_shape=[jax.ShapeDtypeStruct((depth, d, n_wide), BF16),
                   jax.ShapeDtypeStruct((depth, d, SMALL_COLS), BF16)],
        compiler_params=_params(("parallel", "parallel")),
        name="relayout_w_in",
    )(w_in)


def _mixers(p2, ps2, bsz, seq_len, prm, states, *, t_scan, period, need_out):
    p3 = p2.reshape(bsz, seq_len, -1)
    ps3 = ps2.reshape(bsz, seq_len, SMALL_COLS)
    (gs_f, gs_b), (mc_f, mm_f, mc_b, mm_b) = states
    o_f, gs_f = _gla_scan(p3, ps3, prm["wa_f"], prm["ba_f"], gs_f, None, None,
                          reverse=False, t=t_scan)
    y_gla, gs_b = _gla_scan(p3, ps3, prm["wa_b"], prm["ba_b"], gs_b, prm["gla_norm"], o_f,
                            reverse=True, t=t_scan)
    h_f, mc_f, mm_f, qkc = _mlstm_scan(p3, ps3, prm["ml_conv"], prm["ml_gate_b"], mc_f, mm_f,
                                       None, None, reverse=False, t=t_scan, period=period)
    y_ml, mc_b, mm_b = _mlstm_scan(p3, ps3, None, prm["ml_gate_b"], mc_b, mm_b,
                                   prm["ml_norm"], (h_f, qkc), reverse=True, t=t_scan,
                                   period=period)
    new_states = ((gs_f, gs_b), (mc_f, mm_f, mc_b, mm_b))
    if not need_out:
        return None, new_states
    y_sc, y_cm = _sc_cm_mixers(p2, prm["sc_conv"], prm["cm_ws"], prm["cm_bs"], prm["cm_norm"],
                               t=min(512, seq_len), period=period)
    m = bsz * seq_len
    return (y_gla.reshape(m, D_GROUP), y_ml.reshape(m, D_GROUP), y_sc, y_cm), new_states


def kernel(x, c, ctx, c_ctx, w_mod, b_mod, norm1, norm2, w_in, w_out, gla_wa2, gla_ba, gla_norm,
           mlstm_conv, mlstm_gate_b, mlstm_norm, sc_conv, cm_ws, cm_bs, cm_norm,
           ffn_up, ffn_conv, ffn_down, final_norm):
    bsz, seq_len, d = x.shape
    ctx_len = ctx.shape[1]
    depth = w_mod.shape[0]
    assert seq_len % 512 == 0 and ctx_len % CM_CHUNK == 0 and bsz + 1 <= 8

    cond = jnp.zeros((8, d), F32).at[:bsz].set(c).at[bsz].set(c_ctx)
    mod = _modulation(cond, w_mod, b_mod).reshape(depth, 8, 6, d).transpose(0, 2, 1, 3)

    xl = x.reshape(bsz * seq_len, d)
    xc = ctx.reshape(bsz * ctx_len, d)
    tm_l, tm_c = 512, ctx_len
    tm_in = 512

    zero_states = (
        (jnp.zeros((bsz, N_HEADS, HEAD_DIM, HEAD_DIM), F32),) * 2,
        (jnp.zeros((bsz, N_HEADS, HEAD_DIM, 2 * HEAD_DIM), F32), jnp.zeros((bsz, 8, LANES), F32)) * 2,
    )

    w_wide_all, w_small_all = _relayout_w_in(w_in)
    k_scale = jnp.concatenate([jnp.ones((1, D_GROUP), F32),
                               jnp.full((1, D_GROUP), HEAD_DIM ** -0.5, F32)], axis=1)
    for l in range(depth):
        need_ctx = l < depth - 1
        mod_l = mod[l][:, :bsz, None, :]
        mod_c = jnp.broadcast_to(mod[l][:, bsz:bsz + 1, None, :], (6, bsz, 1, d))
        wa = jnp.zeros((2, SMALL_COLS, D_GROUP), F32)
        wa = wa.at[0, Z_FWD:Z_FWD + GLA_LOWRANK].set(gla_wa2[l, 0])
        wa = wa.at[1, Z_BWD:Z_BWD + GLA_LOWRANK].set(gla_wa2[l, 1])
        wa_hi = wa.astype(BF16)
        wa_lo = (wa - wa_hi.astype(F32)).astype(BF16)
        wa = jnp.concatenate([wa_hi, wa_hi, wa_lo], axis=1)
        gate_b = jnp.zeros((1, SMALL_COLS), F32).at[0, GATE0:GATE0 + 4 * N_HEADS].set(
            mlstm_gate_b[l].reshape(-1))
        prm = dict(
            wa_f=wa[0], wa_b=wa[1], ba_f=gla_ba[l, 0:1], ba_b=gla_ba[l, 1:2],
            gla_norm=gla_norm[l].reshape(1, D_GROUP),
            ml_conv=mlstm_conv[l] * k_scale, ml_gate_b=gate_b,
            ml_norm=mlstm_norm[l].reshape(1, D_GROUP),
            sc_conv=sc_conv[l], cm_ws=cm_ws[l].astype(BF16),
            cm_bs=jnp.repeat(cm_bs[l].T, HEAD_DIM, axis=1),
            cm_norm=cm_norm[l].reshape(1, D_GROUP),
        )
        w_out_l = w_out[l].astype(BF16)
        w_up_l, w_down_l = ffn_up[l].astype(BF16), ffn_down[l].astype(BF16)

        pc, psc = _in_projection(xc, norm1[l], mod_c, w_wide_all, w_small_all, l, ctx_len, tm_c)
        yc, ctx_states = _mixers(pc, psc, bsz, ctx_len, prm, zero_states,
                                 t_scan=ctx_len, period=ctx_len, need_out=need_ctx)
        p_l, ps_l = _in_projection(xl, norm1[l], mod_l, w_wide_all, w_small_all, l, seq_len, tm_in)
        yl, _ = _mixers(p_l, ps_l, bsz, seq_len, prm, ctx_states,
                        t_scan=512, period=GRID_W, need_out=True)

        xl = _out_projection(yl, w_out_l, xl, mod_l, seq_len, tm_l)
        xl = _conv_ffn(xl, norm2[l], mod_l, w_up_l, ffn_conv[l], w_down_l,
                       final_norm if l == depth - 1 else None, seq_len, GRID_W, tm_l)
        if need_ctx:
            xc = _out_projection(yc, w_out_l, xc, mod_c, ctx_len, tm_c)
            xc = _conv_ffn(xc, norm2[l], mod_c, w_up_l, ffn_conv[l], w_down_l, None,
                           ctx_len, ctx_len, tm_c)
    return xl.reshape(bsz, seq_len, d)
```

```python
import functools

import jax
import jax.numpy as jnp
from jax import lax
from jax.experimental import pallas as pl
from jax.experimental.pallas import tpu as pltpu

F32 = jnp.float32
BF16 = jnp.bfloat16

GRID_W = 64
N_MIXERS = 4
HEAD_DIM = 128
N_HEADS = 4
D_GROUP = N_HEADS * HEAD_DIM
GLA_LOWRANK = 16
GLA_GATE_NORM = 16.0
CHUNK = 64
CM_CHUNK = 128
EPS = 1e-6
LANES = 128
SMALL_COLS = LANES
Z_FWD, Z_BWD, GATE0 = 0, GLA_LOWRANK, 2 * GLA_LOWRANK
VMEM_LIMIT = 56 * 1024 * 1024
INPROJ_VMEM_LIMIT = 60 * 1024 * 1024
FFN_VMEM_LIMIT = 62 * 1024 * 1024

(C_GQ, C_GK, C_GV, C_GG, C_MQK, C_MV, C_MO, C_SB, C_SC, C_SH, C_CU, C_CV) = (
    0, 1, 2, 3, 4, 6, 7, 8, 9, 10, 11, 12)
N_WIDE_BLOCKS = 13


def _params(sem):
    return pltpu.CompilerParams(dimension_semantics=sem, vmem_limit_bytes=VMEM_LIMIT)


def _sigmoid(x):
    return 1.0 / (1.0 + jnp.exp(-x))


def _log_sigmoid(x):
    return jnp.minimum(x, 0.0) - jnp.log(1.0 + jnp.exp(-jnp.abs(x)))


def _gelu_tanh(x):
    return x * (0.5 * (1.0 + jnp.tanh(0.7978845608028654 * (x + 0.044715 * (x * x * x)))))


def _shift_rows(x, period):
    n = x.shape[0]
    row = lax.broadcasted_iota(jnp.int32, (n, 1), 0) % period
    prev = jnp.where(row == 0, 0.0, pltpu.roll(x, 1, axis=0))
    nxt = jnp.where(row == period - 1, 0.0, pltpu.roll(x, n - 1, axis=0))
    return prev, nxt


def _dwconv3(x, w_ref, period):
    prev, nxt = _shift_rows(x, period)
    return w_ref[0:1, :] * prev + w_ref[1:2, :] * x + w_ref[2:3, :] * nxt


def _mod_kernel(c_ref, w_ref, b_ref, o_ref):
    s = c_ref[...]
    s = (s * _sigmoid(s)).astype(BF16)
    o_ref[...] = jnp.dot(s, w_ref[...].astype(BF16), preferred_element_type=F32) + b_ref[...]


def _modulation(cond, w_mod, b_mod, tn=1024):
    depth, d, n = w_mod.shape
    rows = cond.shape[0]
    return pl.pallas_call(
        _mod_kernel,
        grid=(depth, n // tn),
        in_specs=[
            pl.BlockSpec((rows, d), lambda l, j: (0, 0)),
            pl.BlockSpec((None, d, tn), lambda l, j: (l, 0, j)),
            pl.BlockSpec((None, 1, tn), lambda l, j: (l, 0, j)),
        ],
        out_specs=pl.BlockSpec((None, rows, tn), lambda l, j: (l, 0, j)),
        out_shape=jax.ShapeDtypeStruct((depth, rows, n), F32),
        compiler_params=_params(("parallel", "parallel")),
        name="modulation",
    )(cond, w_mod, b_mod.reshape(depth, 1, n))


def _modulated_norm(x, g, sc, sh):
    r = lax.rsqrt(jnp.mean(x * x, axis=-1, keepdims=True) + EPS)
    return (x * r * g) * (1.0 + sc) + sh


def _inproj_kernel(x_ref, g_ref, sh_ref, sc_ref, w_ref, ws_ref, p_ref, ps_ref, *, tn):
    hn = _modulated_norm(x_ref[...], g_ref[...], sc_ref[...], sh_ref[...]).astype(BF16)
    ps_ref[...] = jnp.dot(hn, ws_ref[...], preferred_element_type=F32)
    for j in range(w_ref.shape[1] // tn):
        cols = slice(j * tn, (j + 1) * tn)
        p_ref[:, cols] = jnp.dot(hn, w_ref[:, cols], preferred_element_type=F32
                                 ).astype(p_ref.dtype)


def _row_spec(piece, seq_len, tm, d):
    return pl.BlockSpec((None, None, 1, d), lambda i, *_: (piece, (i * tm) // seq_len, 0, 0))


def _in_projection(x2, gain, mod, w_wide, w_small, layer, seq_len, tm, tn=D_GROUP):
    m, d = x2.shape
    n = w_wide.shape[2]
    resident = lambda shape: pl.BlockSpec((None,) + shape, lambda i: (layer, 0, 0),
                                          pipeline_mode=pl.Buffered(1))
    return pl.pallas_call(
        functools.partial(_inproj_kernel, tn=tn),
        grid=(m // tm,),
        in_specs=[
            pl.BlockSpec((tm, d), lambda i: (i, 0)),
            pl.BlockSpec((1, d), lambda i: (0, 0)),
            _row_spec(0, seq_len, tm, d),
            _row_spec(1, seq_len, tm, d),
            resident((d, n)),
            resident((d, SMALL_COLS)),
        ],
        out_specs=[
            pl.BlockSpec((tm, n), lambda i: (i, 0)),
            pl.BlockSpec((tm, SMALL_COLS), lambda i: (i, 0)),
        ],
        out_shape=[
            jax.ShapeDtypeStruct((m, n), BF16),
            jax.ShapeDtypeStruct((m, SMALL_COLS), F32),
        ],
        compiler_params=pltpu.CompilerParams(dimension_semantics=("parallel",),
                                             vmem_limit_bytes=INPROJ_VMEM_LIMIT),
        name="in_projection",
    )(x2, gain.reshape(1, d), mod, mod, w_wide, w_small)


def _tri(reverse):
    r = lax.broadcasted_iota(jnp.int32, (CHUNK, CHUNK), 0)
    c = lax.broadcasted_iota(jnp.int32, (CHUNK, CHUNK), 1)
    return (c >= r) if reverse else (c <= r)


def _tri3(reverse):
    r = lax.broadcasted_iota(jnp.int32, (CHUNK, 3 * CHUNK), 0)
    c = lax.broadcasted_iota(jnp.int32, (CHUNK, 3 * CHUNK), 1) % CHUNK
    return ((c >= r) if reverse else (c <= r)).astype(BF16)


def _split2(x):
    hi = x.astype(BF16)
    return hi, (x - hi.astype(F32)).astype(BF16)


def _chunk_cumsum(tri3, x):
    x1 = x.astype(BF16)
    r1 = x - x1.astype(F32)
    x2 = r1.astype(BF16)
    x3 = (r1 - x2.astype(F32)).astype(BF16)
    return jnp.dot(tri3, jnp.concatenate([x1, x2, x3], axis=0), preferred_element_type=F32)


def _head_norm_gate(o, gain, gate):
    r = lax.rsqrt(jnp.mean(o * o, axis=-1, keepdims=True) + EPS)
    return o * r * gain * gate


def _gla_kernel(*refs, reverse, final, n_chunks):
    if final:
        (q_ref, k_ref, v_ref, z_ref, wa_ref, ba_ref, s0_ref, of_ref, g_ref, gn_ref,
         out_ref, sfin_ref, st_ref, qin_ref, qrel_ref, krel_ref, dec_ref, att_ref, kv_ref,
         acc_ref) = refs
    else:
        (q_ref, k_ref, v_ref, z_ref, wa_ref, ba_ref, s0_ref,
         out_ref, sfin_ref, st_ref, qin_ref, qrel_ref, krel_ref, dec_ref, att_ref,
         kv_ref) = refs
        acc_ref = out_ref
    step = pl.program_id(1)

    @pl.when(step == 0)
    def _():
        st_ref[...] = s0_ref[...]

    mask = _tri(reverse)
    tri3 = _tri3(reverse)
    last = 0 if reverse else CHUNK - 1
    scale = HEAD_DIM ** -0.5
    chunk_rows = [slice(c * CHUNK, (c + 1) * CHUNK) for c in range(n_chunks)]
    head_cols = [slice(h * HEAD_DIM, (h + 1) * HEAD_DIM) for h in range(N_HEADS)]

    z_hi, z_lo = _split2(z_ref[...])
    pre = jnp.dot(jnp.concatenate([z_hi, z_lo, z_hi], axis=1), wa_ref[...],
                  preferred_element_type=F32) + ba_ref[...]
    log_a = _log_sigmoid(pre) * (1.0 / GLA_GATE_NORM)

    for c, rows in enumerate(chunk_rows):
        b_cum = _chunk_cumsum(tri3, log_a[rows, :])
        b_last = b_cum[last:last + 1, :]
        q = q_ref[rows, :].astype(F32) * scale
        qin_ref[rows, :] = (q * jnp.exp(b_cum)).astype(BF16)
        qrel_ref[rows, :] = (q * jnp.exp(b_cum - b_last)).astype(BF16)
        krel_ref[rows, :] = (k_ref[rows, :].astype(F32) * jnp.exp(b_last - b_cum)).astype(BF16)
        dec_ref[c] = jnp.exp(b_last)

    for c, rows in enumerate(chunk_rows):
        for h, sl in enumerate(head_cols):
            att = lax.dot_general(qrel_ref[rows, sl], krel_ref[rows, sl],
                                  (((1,), (1,)), ((), ())), preferred_element_type=F32)
            att_ref[c * N_HEADS + h] = jnp.where(mask, att, 0.0).astype(BF16)

    for c, rows in enumerate(chunk_rows):
        for h, sl in enumerate(head_cols):
            v = v_ref[rows, sl]
            acc_ref[rows, sl] = jnp.dot(att_ref[c * N_HEADS + h], v, preferred_element_type=F32)
            kv_ref[c * N_HEADS + h] = lax.dot_general(
                v, krel_ref[rows, sl], (((0,), (0,)), ((), ())), preferred_element_type=F32)

    for c in (reversed(range(n_chunks)) if reverse else range(n_chunks)):
        rows = chunk_rows[c]
        for h, sl in enumerate(head_cols):
            s_t = st_ref[h]
            acc_ref[rows, sl] += lax.dot_general(qin_ref[rows, sl], s_t.astype(BF16),
                                                 (((1,), (1,)), ((), ())),
                                                 preferred_element_type=F32)
            st_ref[h] = s_t * dec_ref[c][:, sl] + kv_ref[c * N_HEADS + h]

    if final:
        for sl in head_cols:
            g = g_ref[:, sl].astype(F32)
            out_ref[:, sl] = _head_norm_gate(acc_ref[:, sl] + of_ref[:, sl], gn_ref[:, sl],
                                             g * _sigmoid(g)).astype(out_ref.dtype)

    @pl.when(step == pl.num_programs(1) - 1)
    def _():
        sfin_ref[...] = st_ref[...]


def _scan_block_spec(t, cols, col_block, n_blocks, reverse):
    if reverse:
        return pl.BlockSpec((None, t, cols), lambda b, i: (b, n_blocks - 1 - i, col_block))
    return pl.BlockSpec((None, t, cols), lambda b, i: (b, i, col_block))


def _gla_scan(p3, ps3, wa_pad, ba, s0, gn, o_fwd, *, reverse, t):
    bsz, seq_len, _ = p3.shape
    n_blocks = seq_len // t
    n_chunks = t // CHUNK
    final = o_fwd is not None
    spec = functools.partial(_scan_block_spec, n_blocks=n_blocks, reverse=reverse)
    const2 = lambda shape: pl.BlockSpec(shape, lambda b, i: (0, 0))
    state_spec = pl.BlockSpec((None, N_HEADS, HEAD_DIM, HEAD_DIM), lambda b, i: (b, 0, 0, 0))
    in_specs = [
        spec(t, D_GROUP, C_GQ), spec(t, D_GROUP, C_GK), spec(t, D_GROUP, C_GV),
        spec(t, SMALL_COLS, 0),
        const2((3 * SMALL_COLS, D_GROUP)), const2((1, D_GROUP)),
        state_spec,
    ]
    args = [p3, p3, p3, ps3, wa_pad, ba, s0]
    if final:
        in_specs += [spec(t, D_GROUP, 0), spec(t, D_GROUP, C_GG), const2((1, D_GROUP))]
        args += [o_fwd, p3, gn]
    return pl.pallas_call(
        functools.partial(_gla_kernel, reverse=reverse, final=final, n_chunks=n_chunks),
        grid=(bsz, n_blocks),
        in_specs=in_specs,
        out_specs=[spec(t, D_GROUP, 0), state_spec],
        out_shape=[
            jax.ShapeDtypeStruct((bsz, seq_len, D_GROUP), BF16 if final else F32),
            jax.ShapeDtypeStruct((bsz, N_HEADS, HEAD_DIM, HEAD_DIM), F32),
        ],
        scratch_shapes=[
            pltpu.VMEM((N_HEADS, HEAD_DIM, HEAD_DIM), F32),
            pltpu.VMEM((t, D_GROUP), BF16), pltpu.VMEM((t, D_GROUP), BF16),
            pltpu.VMEM((t, D_GROUP), BF16),
            pltpu.VMEM((n_chunks, 1, D_GROUP), F32),
            pltpu.VMEM((n_chunks * N_HEADS, CHUNK, CHUNK), BF16),
            pltpu.VMEM((n_chunks * N_HEADS, HEAD_DIM, HEAD_DIM), F32),
        ] + ([pltpu.VMEM((t, D_GROUP), F32)] if final else []),
        compiler_params=_params(("parallel", "arbitrary")),
        name="gla_bwd" if reverse else "gla_fwd",
    )(*args)


def _mlstm_kernel(*refs, reverse, final, n_chunks, period, lane0):
    if final:
        (qkc_ref, v_ref, gt_ref, gb_ref, c0_ref, m0_ref, hf_ref, og_ref, gn_ref,
         out_ref, cfin_ref, mfin_ref, c_ref, m_ref, fcol_ref, jcol_ref, jt_ref,
         mprev_ref, ulast_ref, p_ref, s_ref, wi_ref, en_ref, kwt_ref, kv_ref, r_ref,
         acc_ref) = refs
    else:
        (qk_ref, v_ref, gt_ref, cw_ref, gb_ref, c0_ref, m0_ref,
         out_ref, cfin_ref, mfin_ref, qkc_ref, c_ref, m_ref, fcol_ref, jcol_ref, jt_ref,
         mprev_ref, ulast_ref, p_ref, s_ref, wi_ref, en_ref, kwt_ref, kv_ref, r_ref) = refs
        acc_ref = out_ref
        qkc_ref[...] = _dwconv3(qk_ref[...].astype(F32), cw_ref, period).astype(BF16)
    step = pl.program_id(1)

    @pl.when(step == 0)
    def _():
        c_ref[...] = c0_ref[...]
        m_ref[...] = m0_ref[...]

    mask = _tri(reverse)
    tri3 = _tri3(reverse)
    last = 0 if reverse else CHUNK - 1
    ones_col = (lax.broadcasted_iota(jnp.int32, (CHUNK, HEAD_DIM), 1) == 0).astype(BF16)
    chunk_rows = [slice(c * CHUNK, (c + 1) * CHUNK) for c in range(n_chunks)]
    head_cols = [slice(h * HEAD_DIM, (h + 1) * HEAD_DIM) for h in range(N_HEADS)]
    key_cols = [slice(D_GROUP + h * HEAD_DIM, D_GROUP + (h + 1) * HEAD_DIM) for h in range(N_HEADS)]
    f_lane = [lane0 + N_HEADS + h for h in range(N_HEADS)]
    scan_order = list(reversed(range(n_chunks))) if reverse else list(range(n_chunks))

    gates = gt_ref[...] + gb_ref[...]
    log_f = _log_sigmoid(gates)
    i_al = pltpu.roll(gates, N_HEADS, axis=1)
    j_max, f_tot = [], []
    for c, rows in enumerate(chunk_rows):
        f_col = _chunk_cumsum(tri3, log_f[rows, :])
        j_col = i_al[rows, :] - f_col
        fcol_ref[rows, :] = f_col
        jcol_ref[rows, :] = j_col
        jt_ref[c] = j_col.T
        j_max.append(jnp.max(j_col, axis=0, keepdims=True))
        f_tot.append(f_col[last:last + 1, :])

    m_run = m_ref[0:1, :]
    for c in scan_order:
        mprev_ref[c] = m_run
        u_last = jnp.maximum(m_run, j_max[c])
        ulast_ref[c] = u_last
        m_run = f_tot[c] + u_last
    m_ref[0:1, :] = m_run

    for c, rows in enumerate(chunk_rows):
        for h in range(N_HEADS):
            lf = f_lane[h]
            idx = c * N_HEADS + h
            a = jnp.where(mask, jt_ref[c][lf:lf + 1, :], -jnp.inf)
            m_prev = mprev_ref[c][:, lf:lf + 1]
            u = jnp.maximum(jnp.max(a, axis=-1, keepdims=True), m_prev)
            p_ref[idx] = jnp.exp(a - u)
            wi_ref[idx] = jnp.broadcast_to(jnp.exp(m_prev - u), (CHUNK, LANES))
            en_ref[idx] = jnp.broadcast_to(
                jnp.exp(-(fcol_ref[rows, lf:lf + 1] + u)), (CHUNK, LANES))

    for c, rows in enumerate(chunk_rows):
        for h in range(N_HEADS):
            idx = c * N_HEADS + h
            s = lax.dot_general(qkc_ref[rows, head_cols[h]], qkc_ref[rows, key_cols[h]],
                                (((1,), (1,)), ((), ())), preferred_element_type=F32)
            s_ref[idx] = (s * p_ref[idx]).astype(BF16)

    for c, rows in enumerate(chunk_rows):
        for h in range(N_HEADS):
            lf = f_lane[h]
            w_k = jnp.exp(jcol_ref[rows, lf:lf + 1] - ulast_ref[c][:, lf:lf + 1])
            kwt_ref[c * N_HEADS + h] = (qkc_ref[rows, key_cols[h]].astype(F32) * w_k
                                        ).T.astype(BF16)

    for c, rows in enumerate(chunk_rows):
        for h in range(N_HEADS):
            idx = c * N_HEADS + h
            v_aug = jnp.concatenate([v_ref[rows, head_cols[h]], ones_col], axis=1)
            r_ref[rows, 2 * h * HEAD_DIM:2 * (h + 1) * HEAD_DIM] = jnp.dot(
                s_ref[idx], v_aug, preferred_element_type=F32)
            kv_ref[idx] = jnp.dot(kwt_ref[idx], v_aug, preferred_element_type=F32)

    for c in scan_order:
        rows = chunk_rows[c]
        decay = jnp.exp(mprev_ref[c] - ulast_ref[c])
        for h in range(N_HEADS):
            lf = f_lane[h]
            idx = c * N_HEADS + h
            c_aug = c_ref[h]
            wi = wi_ref[idx]
            r = jnp.concatenate([wi, wi], axis=1) * jnp.dot(
                qkc_ref[rows, head_cols[h]], c_aug.astype(BF16),
                preferred_element_type=F32) + r_ref[rows, 2 * h * HEAD_DIM:2 * (h + 1) * HEAD_DIM]
            den = jnp.maximum(jnp.abs(r[:, HEAD_DIM:HEAD_DIM + 1]), en_ref[idx][:, 0:1])
            acc_ref[rows, head_cols[h]] = r[:, :HEAD_DIM] * (1.0 / den)
            c_ref[h] = c_aug * decay[:, lf:lf + 1] + kv_ref[idx]

    if final:
        for sl in head_cols:
            out_ref[:, sl] = _head_norm_gate(acc_ref[:, sl] + hf_ref[:, sl], gn_ref[:, sl],
                                             _sigmoid(og_ref[:, sl].astype(F32))
                                             ).astype(out_ref.dtype)

    @pl.when(step == pl.num_programs(1) - 1)
    def _():
        cfin_ref[...] = c_ref[...]
        mfin_ref[...] = m_ref[...]


def _mlstm_scan(p3, ps3, conv_w, gate_b, c0, m0, gn, fwd, *, reverse, t, period):
    bsz, seq_len, _ = p3.shape
    n_blocks = seq_len // t
    n_chunks = t // CHUNK
    final = fwd is not None
    spec = functools.partial(_scan_block_spec, n_blocks=n_blocks, reverse=reverse)
    const2 = lambda shape: pl.BlockSpec(shape, lambda b, i: (0, 0))
    c_spec = pl.BlockSpec((None, N_HEADS, HEAD_DIM, 2 * HEAD_DIM), lambda b, i: (b, 0, 0, 0))
    m_spec = pl.BlockSpec((None, 8, LANES), lambda b, i: (b, 0, 0))
    out_specs = [spec(t, D_GROUP, 0), c_spec, m_spec]
    out_shape = [
        jax.ShapeDtypeStruct((bsz, seq_len, D_GROUP), BF16 if final else F32),
        jax.ShapeDtypeStruct((bsz, N_HEADS, HEAD_DIM, 2 * HEAD_DIM), F32),
        jax.ShapeDtypeStruct((bsz, 8, LANES), F32),
    ]
    if final:
        h_fwd, qkc = fwd
        in_specs = [
            spec(t, 2 * D_GROUP, 0), spec(t, D_GROUP, C_MV), spec(t, SMALL_COLS, 0),
            const2((1, SMALL_COLS)), c_spec, m_spec,
            spec(t, D_GROUP, 0), spec(t, D_GROUP, C_MO), const2((1, D_GROUP)),
        ]
        args = [qkc, p3, ps3, gate_b, c0, m0, h_fwd, p3, gn]
    else:
        in_specs = [
            spec(t, 2 * D_GROUP, C_MQK // 2), spec(t, D_GROUP, C_MV), spec(t, SMALL_COLS, 0),
            const2((3, 2 * D_GROUP)), const2((1, SMALL_COLS)), c_spec, m_spec,
        ]
        args = [p3, p3, ps3, conv_w, gate_b, c0, m0]
        out_specs.append(spec(t, 2 * D_GROUP, 0))
        out_shape.append(jax.ShapeDtypeStruct((bsz, seq_len, 2 * D_GROUP), BF16))
    lane0 = GATE0 + (2 * N_HEADS if reverse else 0)
    units = n_chunks * N_HEADS
    return pl.pallas_call(
        functools.partial(_mlstm_kernel, reverse=reverse, final=final, n_chunks=n_chunks,
                          period=period, lane0=lane0),
        grid=(bsz, n_blocks),
        in_specs=in_specs,
        out_specs=out_specs,
        out_shape=out_shape,
        scratch_shapes=[
            pltpu.VMEM((N_HEADS, HEAD_DIM, 2 * HEAD_DIM), F32),
            pltpu.VMEM((8, LANES), F32),
            pltpu.VMEM((t, SMALL_COLS), F32),
            pltpu.VMEM((t, SMALL_COLS), F32),
            pltpu.VMEM((n_chunks, SMALL_COLS, CHUNK), F32),
            pltpu.VMEM((n_chunks, 1, SMALL_COLS), F32),
            pltpu.VMEM((n_chunks, 1, SMALL_COLS), F32),
            pltpu.VMEM((units, CHUNK, CHUNK), F32),
            pltpu.VMEM((units, CHUNK, CHUNK), BF16),
            pltpu.VMEM((units, CHUNK, LANES), F32),
            pltpu.VMEM((units, CHUNK, LANES), F32),
            pltpu.VMEM((units, HEAD_DIM, CHUNK), BF16),
            pltpu.VMEM((units, HEAD_DIM, 2 * HEAD_DIM), F32),
            pltpu.VMEM((t, 2 * D_GROUP), F32),
        ] + ([pltpu.VMEM((t, D_GROUP), F32)] if final else []),
        compiler_params=_params(("parallel", "arbitrary")),
        name="mlstm_bwd" if reverse else "mlstm_fwd",
    )(*args)


def _sccm_kernel(bg_ref, cg_ref, hh_ref, u_ref, v_ref, scw_ref, ws_ref, bs_ref, gv_ref,
                 ysc_ref, ycm_ref, *, period):
    gated = cg_ref[...].astype(F32) * hh_ref[...].astype(F32)
    ysc_ref[...] = (bg_ref[...].astype(F32) * _dwconv3(gated, scw_ref, period)
                    ).astype(ysc_ref.dtype)
    v = _gelu_tanh(v_ref[...].astype(F32))
    vn = (v * lax.rsqrt(jnp.mean(v * v, axis=-1, keepdims=True) + EPS) * gv_ref[...]).astype(BF16)
    for n in range(v.shape[0] // CM_CHUNK):
        rows = slice(n * CM_CHUNK, (n + 1) * CM_CHUNK)
        for g in range(N_HEADS):
            sl = slice(g * HEAD_DIM, (g + 1) * HEAD_DIM)
            sv = jnp.dot(ws_ref[g], vn[rows, sl], preferred_element_type=F32) + bs_ref[:, sl]
            ycm_ref[rows, sl] = (_gelu_tanh(u_ref[rows, sl].astype(F32)) * sv
                                 ).astype(ycm_ref.dtype)


def _sc_cm_mixers(p2, sc_w, ws, bs_b, gv, *, t, period):
    m = p2.shape[0]
    col = lambda cb: pl.BlockSpec((t, D_GROUP), lambda i: (i, cb))
    const = lambda shape: pl.BlockSpec(shape, lambda i: (0,) * len(shape))
    return pl.pallas_call(
        functools.partial(_sccm_kernel, period=period),
        grid=(m // t,),
        in_specs=[col(C_SB), col(C_SC), col(C_SH), col(C_CU), col(C_CV),
                  const((3, D_GROUP)), const((N_HEADS, CM_CHUNK, CM_CHUNK)),
                  const((CM_CHUNK, D_GROUP)), const((1, D_GROUP))],
        out_specs=[pl.BlockSpec((t, D_GROUP), lambda i: (i, 0))] * 2,
        out_shape=[jax.ShapeDtypeStruct((m, D_GROUP), BF16)] * 2,
        compiler_params=_params(("parallel",)),
        name="sc_cm_mixers",
    )(p2, p2, p2, p2, p2, sc_w, ws, bs_b, gv)


def _outproj_kernel(y0_ref, y1_ref, y2_ref, y3_ref, w_ref, x_ref, g_ref, o_ref):
    acc = jnp.dot(y0_ref[...], w_ref[0:D_GROUP, :], preferred_element_type=F32)
    for n, y_ref in enumerate((y1_ref, y2_ref, y3_ref), start=1):
        acc = acc + jnp.dot(y_ref[...], w_ref[n * D_GROUP:(n + 1) * D_GROUP, :],
                            preferred_element_type=F32)
    o_ref[...] = x_ref[...] + g_ref[...] * acc


def _out_projection(ys, w_out, layer, x2, mod, seq_len, tm):
    m, d = x2.shape
    y_spec = pl.BlockSpec((tm, D_GROUP), lambda i: (i, 0))
    return pl.pallas_call(
        _outproj_kernel,
        grid=(m // tm,),
        in_specs=[y_spec] * N_MIXERS + [
            pl.BlockSpec((None, d, d), lambda i: (layer, 0, 0), pipeline_mode=pl.Buffered(1)),
            pl.BlockSpec((tm, d), lambda i: (i, 0)),
            _row_spec(2, seq_len, tm, d),
        ],
        out_specs=pl.BlockSpec((tm, d), lambda i: (i, 0)),
        out_shape=jax.ShapeDtypeStruct((m, d), F32),
        compiler_params=_params(("parallel",)),
        name="out_projection",
    )(*ys, w_out, x2, mod)


def _ffn_kernel(*refs, period, final):
    if final:
        (x_ref, g_ref, sh_ref, sc_ref, gate_ref, wa_ref, wv_ref, cw_ref, wd_ref, fn_ref,
         o_ref, hn_ref) = refs
    else:
        (x_ref, g_ref, sh_ref, sc_ref, gate_ref, wa_ref, wv_ref, cw_ref, wd_ref,
         o_ref, hn_ref) = refs
    j = pl.program_id(1)
    last = pl.num_programs(1) - 1

    @pl.when(j == 0)
    def _():
        hn_ref[...] = _modulated_norm(x_ref[...], g_ref[...], sc_ref[...], sh_ref[...]).astype(BF16)

    def tile_contribution():
        hn = hn_ref[...]
        a = _dwconv3(jnp.dot(hn, wa_ref[...], preferred_element_type=F32), cw_ref, period)
        v = jnp.dot(hn, wv_ref[...], preferred_element_type=F32)
        hid = (a * _sigmoid(a) * v).astype(BF16)
        return jnp.dot(hid, wd_ref[...], preferred_element_type=F32)

    @pl.when(j == 0)
    def _():
        o_ref[...] = tile_contribution()

    @pl.when(jnp.logical_and(j > 0, j < last))
    def _():
        o_ref[...] += tile_contribution()

    @pl.when(j == last)
    def _():
        y = x_ref[...] + gate_ref[...] * (o_ref[...] + tile_contribution())
        if final:
            y = y * lax.rsqrt(jnp.mean(y * y, axis=-1, keepdims=True) + EPS) * fn_ref[...]
        o_ref[...] = y


def _conv_ffn(x2, gain, mod, w_up, w_conv, w_down, layer, final_gain, seq_len, period, tm,
              tf=512):
    m, d = x2.shape
    d_ff = w_down.shape[1]
    n_f = d_ff // tf
    final = final_gain is not None
    in_specs = [
        pl.BlockSpec((tm, d), lambda i, j: (i, 0)),
        pl.BlockSpec((1, d), lambda i, j: (0, 0)),
        _row_spec(3, seq_len, tm, d), _row_spec(4, seq_len, tm, d), _row_spec(5, seq_len, tm, d),
        pl.BlockSpec((None, d, tf), lambda i, j: (layer, 0, j)),
        pl.BlockSpec((None, d, tf), lambda i, j: (layer, 0, j + n_f)),
        pl.BlockSpec((None, 3, tf), lambda i, j: (layer, 0, j)),
        pl.BlockSpec((None, tf, d), lambda i, j: (layer, j, 0)),
    ]
    args = [x2, gain.reshape(1, d), mod, mod, mod, w_up, w_up, w_conv, w_down]
    if final:
        in_specs.append(pl.BlockSpec((1, d), lambda i, j: (0, 0)))
        args.append(final_gain.reshape(1, d))
    return pl.pallas_call(
        functools.partial(_ffn_kernel, period=period, final=final),
        grid=(m // tm, n_f),
        in_specs=in_specs,
        out_specs=pl.BlockSpec((tm, d), lambda i, j: (i, 0)),
        out_shape=jax.ShapeDtypeStruct((m, d), F32),
        scratch_shapes=[pltpu.VMEM((tm, d), BF16)],
        compiler_params=pltpu.CompilerParams(dimension_semantics=("parallel", "arbitrary"),
                                             vmem_limit_bytes=FFN_VMEM_LIMIT),
        name="conv_ffn",
    )(*args)


def _relayout_w_in(w_in):
    depth, d, _ = w_in.shape
    gla = 4 * D_GROUP + 2 * GLA_LOWRANK
    ml = 4 * D_GROUP + 4 * N_HEADS
    wide = jnp.concatenate([w_in[..., :4 * D_GROUP], w_in[..., gla:gla + 4 * D_GROUP],
                            w_in[..., gla + ml:]], axis=-1).astype(BF16)
    small = jnp.concatenate([
        w_in[..., 4 * D_GROUP:gla], w_in[..., gla + 4 * D_GROUP:gla + ml],
        jnp.zeros((depth, d, SMALL_COLS - 2 * GLA_LOWRANK - 4 * N_HEADS), w_in.dtype)],
        axis=-1).astype(BF16)
    return wide, small


def _mixers(p2, ps2, bsz, seq_len, prm, states, *, t_scan, period, need_out):
    p3 = p2.reshape(bsz, seq_len, -1)
    ps3 = ps2.reshape(bsz, seq_len, SMALL_COLS)
    (gs_f, gs_b), (mc_f, mm_f, mc_b, mm_b) = states
    o_f, gs_f = _gla_scan(p3, ps3, prm["wa_f"], prm["ba_f"], gs_f, None, None,
                          reverse=False, t=t_scan)
    y_gla, gs_b = _gla_scan(p3, ps3, prm["wa_b"], prm["ba_b"], gs_b, prm["gla_norm"], o_f,
                            reverse=True, t=t_scan)
    h_f, mc_f, mm_f, qkc = _mlstm_scan(p3, ps3, prm["ml_conv"], prm["ml_gate_b"], mc_f, mm_f,
                                       None, None, reverse=False, t=t_scan, period=period)
    y_ml, mc_b, mm_b = _mlstm_scan(p3, ps3, None, prm["ml_gate_b"], mc_b, mm_b,
                                   prm["ml_norm"], (h_f, qkc), reverse=True, t=t_scan,
                                   period=period)
    new_states = ((gs_f, gs_b), (mc_f, mm_f, mc_b, mm_b))
    if not need_out:
        return None, new_states
    y_sc, y_cm = _sc_cm_mixers(p2, prm["sc_conv"], prm["cm_ws"], prm["cm_bs"], prm["cm_norm"],
                               t=min(512, seq_len), period=period)
    m = bsz * seq_len
    return (y_gla.reshape(m, D_GROUP), y_ml.reshape(m, D_GROUP), y_sc, y_cm), new_states


def kernel(x, c, ctx, c_ctx, w_mod, b_mod, norm1, norm2, w_in, w_out, gla_wa2, gla_ba, gla_norm,
           mlstm_conv, mlstm_gate_b, mlstm_norm, sc_conv, cm_ws, cm_bs, cm_norm,
           ffn_up, ffn_conv, ffn_down, final_norm):
    bsz, seq_len, d = x.shape
    ctx_len = ctx.shape[1]
    depth = w_mod.shape[0]
    assert seq_len % 512 == 0 and ctx_len % CM_CHUNK == 0 and bsz + 1 <= 8

    cond = jnp.zeros((8, d), F32).at[:bsz].set(c).at[bsz].set(c_ctx)
    mod = _modulation(cond, w_mod, b_mod).reshape(depth, 8, 6, d).transpose(0, 2, 1, 3)

    xl = x.reshape(bsz * seq_len, d)
    xc = ctx.reshape(bsz * ctx_len, d)
    tm_l, tm_c = 512, ctx_len
    tm_in = 512
    tm_ffn = 1024

    zero_states = (
        (jnp.zeros((bsz, N_HEADS, HEAD_DIM, HEAD_DIM), F32),) * 2,
        (jnp.zeros((bsz, N_HEADS, HEAD_DIM, 2 * HEAD_DIM), F32), jnp.zeros((bsz, 8, LANES), F32)) * 2,
    )

    w_wide_all, w_small_all = _relayout_w_in(w_in)
    w_out_b, w_up_b, w_down_b = w_out.astype(BF16), ffn_up.astype(BF16), ffn_down.astype(BF16)
    k_scale = jnp.concatenate([jnp.ones((1, D_GROUP), F32),
                               jnp.full((1, D_GROUP), HEAD_DIM ** -0.5, F32)], axis=1)
    for l in range(depth):
        need_ctx = l < depth - 1
        mod_l = mod[l][:, :bsz, None, :]
        mod_c = jnp.broadcast_to(mod[l][:, bsz:bsz + 1, None, :], (6, bsz, 1, d))
        wa = jnp.zeros((2, SMALL_COLS, D_GROUP), F32)
        wa = wa.at[0, Z_FWD:Z_FWD + GLA_LOWRANK].set(gla_wa2[l, 0])
        wa = wa.at[1, Z_BWD:Z_BWD + GLA_LOWRANK].set(gla_wa2[l, 1])
        wa_hi = wa.astype(BF16)
        wa_lo = (wa - wa_hi.astype(F32)).astype(BF16)
        wa = jnp.concatenate([wa_hi, wa_hi, wa_lo], axis=1)
        gate_b = jnp.zeros((1, SMALL_COLS), F32).at[0, GATE0:GATE0 + 4 * N_HEADS].set(
            mlstm_gate_b[l].reshape(-1))
        prm = dict(
            wa_f=wa[0], wa_b=wa[1], ba_f=gla_ba[l, 0:1], ba_b=gla_ba[l, 1:2],
            gla_norm=gla_norm[l].reshape(1, D_GROUP),
            ml_conv=mlstm_conv[l] * k_scale, ml_gate_b=gate_b,
            ml_norm=mlstm_norm[l].reshape(1, D_GROUP),
            sc_conv=sc_conv[l], cm_ws=cm_ws[l].astype(BF16),
            cm_bs=jnp.repeat(cm_bs[l].T, HEAD_DIM, axis=1),
            cm_norm=cm_norm[l].reshape(1, D_GROUP),
        )

        pc, psc = _in_projection(xc, norm1[l], mod_c, w_wide_all, w_small_all, l, ctx_len, tm_c)
        yc, ctx_states = _mixers(pc, psc, bsz, ctx_len, prm, zero_states,
                                 t_scan=ctx_len, period=ctx_len, need_out=need_ctx)
        p_l, ps_l = _in_projection(xl, norm1[l], mod_l, w_wide_all, w_small_all, l, seq_len, tm_in)
        yl, _ = _mixers(p_l, ps_l, bsz, seq_len, prm, ctx_states,
                        t_scan=512, period=GRID_W, need_out=True)

        xl = _out_projection(yl, w_out_b, l, xl, mod_l, seq_len, tm_l)
        xl = _conv_ffn(xl, norm2[l], mod_l, w_up_b, ffn_conv, w_down_b, l,
                       final_norm if l == depth - 1 else None, seq_len, GRID_W, tm_ffn)
        if need_ctx:
            xc = _out_projection(yc, w_out_b, l, xc, mod_c, ctx_len, tm_c)
            xc = _conv_ffn(xc, norm2[l], mod_c, w_up_b, ffn_conv, w_down_b, l, None,
                           ctx_len, ctx_len, tm_c)
    return xl.reshape(bsz, seq_len, d)
```

```python
import functools

import numpy as np
import jax
import jax.numpy as jnp
from jax import lax
from jax.experimental import pallas as pl
from jax.experimental.pallas import tpu as pltpu

F32 = jnp.float32
BF16 = jnp.bfloat16

GRID_W = 64
N_MIXERS = 4
HEAD_DIM = 128
N_HEADS = 4
D_GROUP = N_HEADS * HEAD_DIM
GLA_LOWRANK = 16
GLA_GATE_NORM = 16.0
CHUNK = 64
CM_CHUNK = 128
EPS = 1e-6
LANES = 128
SMALL_COLS = LANES
Z_FWD, Z_BWD, GATE0 = 0, GLA_LOWRANK, 2 * GLA_LOWRANK
VMEM_LIMIT = 56 * 1024 * 1024
INPROJ_VMEM_LIMIT = 60 * 1024 * 1024
FFN_VMEM_LIMIT = 62 * 1024 * 1024

(C_GQ, C_GK, C_GV, C_GG, C_MQK, C_MV, C_MO, C_SB, C_SC, C_SH, C_CU, C_CV) = (
    0, 1, 2, 3, 4, 6, 7, 8, 9, 10, 11, 12)
N_WIDE_BLOCKS = 13
ROW_PAD = 16


def _params(sem):
    return pltpu.CompilerParams(dimension_semantics=sem, vmem_limit_bytes=VMEM_LIMIT)


def _sigmoid(x):
    return 1.0 / (1.0 + jnp.exp(-x))


def _log_sigmoid(x):
    return jnp.minimum(x, 0.0) - jnp.log(1.0 + jnp.exp(-jnp.abs(x)))


def _gelu_tanh(x):
    return x * (0.5 * (1.0 + jnp.tanh(0.7978845608028654 * (x + 0.044715 * (x * x * x)))))


def _shift_rows(x, period):
    n = x.shape[0]
    row = lax.broadcasted_iota(jnp.int32, (n, 1), 0) % period
    prev = jnp.where(row == 0, 0.0, pltpu.roll(x, 1, axis=0))
    nxt = jnp.where(row == period - 1, 0.0, pltpu.roll(x, n - 1, axis=0))
    return prev, nxt


def _dwconv3(x, w_ref, period):
    prev, nxt = _shift_rows(x, period)
    return w_ref[0:1, :] * prev + w_ref[1:2, :] * x + w_ref[2:3, :] * nxt


def _mod_kernel(c_ref, w_ref, b_ref, o_ref):
    s = c_ref[...]
    s = (s * _sigmoid(s)).astype(BF16)
    o_ref[...] = jnp.dot(s, w_ref[...].astype(BF16), preferred_element_type=F32) + b_ref[...]


def _modulation(cond, w_mod, b_mod, tn=1024):
    depth, d, n = w_mod.shape
    rows = cond.shape[0]
    return pl.pallas_call(
        _mod_kernel,
        grid=(depth, n // tn),
        in_specs=[
            pl.BlockSpec((rows, d), lambda l, j: (0, 0)),
            pl.BlockSpec((None, d, tn), lambda l, j: (l, 0, j)),
            pl.BlockSpec((None, 1, tn), lambda l, j: (l, 0, j)),
        ],
        out_specs=pl.BlockSpec((None, rows, tn), lambda l, j: (l, 0, j)),
        out_shape=jax.ShapeDtypeStruct((depth, rows, n), F32),
        compiler_params=_params(("parallel", "parallel")),
        name="modulation",
    )(cond, w_mod, b_mod.reshape(depth, 1, n))


def _modulated_norm(x, g, sc, sh):
    r = lax.rsqrt(jnp.mean(x * x, axis=-1, keepdims=True) + EPS)
    return (x * r * g) * (1.0 + sc) + sh


def _inproj_kernel(x_ref, g_ref, sh_ref, sc_ref, w_ref, ws_ref, p_ref, ps_ref, *, tn):
    hn = _modulated_norm(x_ref[...], g_ref[...], sc_ref[...], sh_ref[...]).astype(BF16)
    ps_ref[...] = jnp.dot(hn, ws_ref[...], preferred_element_type=F32)
    for j in range(w_ref.shape[1] // tn):
        cols = slice(j * tn, (j + 1) * tn)
        p_ref[:, cols] = jnp.dot(hn, w_ref[:, cols], preferred_element_type=F32
                                 ).astype(p_ref.dtype)


def _row_spec(piece, seq_len, tm, d):
    return pl.BlockSpec((None, None, 1, d), lambda i, *_: (piece, (i * tm) // seq_len, 0, 0))


def _in_projection(x2, gain, mod, w_wide, w_small, layer, seq_len, tm, tn=D_GROUP):
    m, d = x2.shape
    n = w_wide.shape[2]
    resident = lambda shape: pl.BlockSpec((None,) + shape, lambda i: (layer, 0, 0),
                                          pipeline_mode=pl.Buffered(1))
    return pl.pallas_call(
        functools.partial(_inproj_kernel, tn=tn),
        grid=(m // tm,),
        in_specs=[
            pl.BlockSpec((tm, d), lambda i: (i, 0)),
            pl.BlockSpec((1, d), lambda i: (0, 0)),
            _row_spec(0, seq_len, tm, d),
            _row_spec(1, seq_len, tm, d),
            resident((d, n)),
            resident((d, SMALL_COLS)),
        ],
        out_specs=[
            pl.BlockSpec((tm, n), lambda i: (i, 0)),
            pl.BlockSpec((tm, SMALL_COLS), lambda i: (i, 0)),
        ],
        out_shape=[
            jax.ShapeDtypeStruct((m, n), BF16),
            jax.ShapeDtypeStruct((m, SMALL_COLS), F32),
        ],
        compiler_params=pltpu.CompilerParams(dimension_semantics=("parallel",),
                                             vmem_limit_bytes=INPROJ_VMEM_LIMIT),
        name="in_projection",
    )(x2, gain.reshape(1, d), mod, mod, w_wide, w_small)


def _tri(reverse):
    r = lax.broadcasted_iota(jnp.int32, (CHUNK, CHUNK), 0)
    c = lax.broadcasted_iota(jnp.int32, (CHUNK, CHUNK), 1)
    return (c >= r) if reverse else (c <= r)


def _tri3(reverse):
    r = lax.broadcasted_iota(jnp.int32, (CHUNK, 3 * CHUNK), 0)
    c = lax.broadcasted_iota(jnp.int32, (CHUNK, 3 * CHUNK), 1) % CHUNK
    return ((c >= r) if reverse else (c <= r)).astype(BF16)


def _split2(x):
    hi = x.astype(BF16)
    return hi, (x - hi.astype(F32)).astype(BF16)


def _chunk_cumsum(tri3, x):
    x1 = x.astype(BF16)
    r1 = x - x1.astype(F32)
    x2 = r1.astype(BF16)
    x3 = (r1 - x2.astype(F32)).astype(BF16)
    return jnp.dot(tri3, jnp.concatenate([x1, x2, x3], axis=0), preferred_element_type=F32)


def _emit_skewed(order, chunk_stages, head_stages):
    n_stage = len(chunk_stages) + len(head_stages)
    for slot in range(len(order) + n_stage - 1):
        for k, stage in enumerate(chunk_stages):
            if 0 <= slot - k < len(order):
                stage(order[slot - k])
        for h in range(N_HEADS):
            for k, stage in enumerate(head_stages, start=len(chunk_stages)):
                if 0 <= slot - k < len(order):
                    stage(order[slot - k], h)


def _cummax_rows(x, reverse):
    n = x.shape[0]
    row = lax.broadcasted_iota(jnp.int32, (n, 1), 0)
    k = 1
    while k < n:
        if reverse:
            shifted = jnp.where(row >= n - k, -jnp.inf, pltpu.roll(x, n - k, axis=0))
        else:
            shifted = jnp.where(row < k, -jnp.inf, pltpu.roll(x, k, axis=0))
        x = jnp.maximum(x, shifted)
        k *= 2
    return x


def _lane_spread(x, bm_ref):
    x1 = x.astype(BF16)
    r1 = x - x1.astype(F32)
    x2 = r1.astype(BF16)
    x3 = (r1 - x2.astype(F32)).astype(BF16)
    return jnp.dot(jnp.concatenate([x1, x2, x3], axis=1), bm_ref[...], preferred_element_type=F32)


def _head_norm_gate(o, gain, gate):
    r = lax.rsqrt(jnp.mean(o * o, axis=-1, keepdims=True) + EPS)
    return o * r * gain * gate


def _gla_kernel(*refs, reverse, final, n_chunks):
    if final:
        (q_ref, k_ref, v_ref, z_ref, wa_ref, ba_ref, s0_ref, of_ref, g_ref, gn_ref,
         out_ref, sfin_ref, st_ref, qin_ref, qrel_ref, krel_ref, dec_ref, att_ref, kv_ref,
         acc_ref) = refs
    else:
        (q_ref, k_ref, v_ref, z_ref, wa_ref, ba_ref, s0_ref,
         out_ref, sfin_ref, st_ref, qin_ref, qrel_ref, krel_ref, dec_ref, att_ref,
         kv_ref) = refs
        acc_ref = out_ref
    step = pl.program_id(1)

    @pl.when(step == 0)
    def _():
        st_ref[...] = s0_ref[...]

    mask = _tri(reverse)
    tri3 = _tri3(reverse)
    last = 0 if reverse else CHUNK - 1
    scale = HEAD_DIM ** -0.5
    chunk_rows = [slice(c * CHUNK, (c + 1) * CHUNK) for c in range(n_chunks)]
    head_cols = [slice(h * HEAD_DIM, (h + 1) * HEAD_DIM) for h in range(N_HEADS)]

    z_hi, z_lo = _split2(z_ref[...])
    pre = jnp.dot(jnp.concatenate([z_hi, z_lo, z_hi], axis=1), wa_ref[...],
                  preferred_element_type=F32) + ba_ref[...]
    log_a = _log_sigmoid(pre) * (1.0 / GLA_GATE_NORM)

    def decayed_qk(c):
        rows = chunk_rows[c]
        b_cum = _chunk_cumsum(tri3, log_a[rows, :])
        b_last = b_cum[last:last + 1, :]
        q = q_ref[rows, :].astype(F32) * scale
        qin_ref[rows, :] = (q * jnp.exp(b_cum)).astype(BF16)
        qrel_ref[rows, :] = (q * jnp.exp(b_cum - b_last)).astype(BF16)
        krel_ref[rows, :] = (k_ref[rows, :].astype(F32) * jnp.exp(b_last - b_cum)).astype(BF16)
        dec_ref[c] = jnp.exp(b_last)

    def scores(c, h):
        rows, sl = chunk_rows[c], head_cols[h]
        att = lax.dot_general(qrel_ref[rows, sl], krel_ref[rows, sl],
                              (((1,), (1,)), ((), ())), preferred_element_type=F32)
        att_ref[c * N_HEADS + h] = jnp.where(mask, att, 0.0).astype(BF16)

    def local_output(c, h):
        rows, sl = chunk_rows[c], head_cols[h]
        v = v_ref[rows, sl]
        acc_ref[rows, sl] = jnp.dot(att_ref[c * N_HEADS + h], v, preferred_element_type=F32)
        kv_ref[c * N_HEADS + h] = lax.dot_general(
            v, krel_ref[rows, sl], (((0,), (0,)), ((), ())), preferred_element_type=F32)

    def carry(c, h):
        rows, sl = chunk_rows[c], head_cols[h]
        s_t = st_ref[h]
        acc_ref[rows, sl] += lax.dot_general(qin_ref[rows, sl], s_t.astype(BF16),
                                             (((1,), (1,)), ((), ())),
                                             preferred_element_type=F32)
        st_ref[h] = s_t * dec_ref[c][:, sl] + kv_ref[c * N_HEADS + h]

    _emit_skewed(list(reversed(range(n_chunks))) if reverse else list(range(n_chunks)),
                 [decayed_qk], [scores, local_output, carry])

    if final:
        for sl in head_cols:
            g = g_ref[:, sl].astype(F32)
            out_ref[:, sl] = _head_norm_gate(acc_ref[:, sl] + of_ref[:, sl], gn_ref[:, sl],
                                             g * _sigmoid(g)).astype(out_ref.dtype)

    @pl.when(step == pl.num_programs(1) - 1)
    def _():
        sfin_ref[...] = st_ref[...]


def _scan_block_spec(t, cols, col_block, n_blocks, reverse):
    if reverse:
        return pl.BlockSpec((None, t, cols), lambda b, i: (b, n_blocks - 1 - i, col_block))
    return pl.BlockSpec((None, t, cols), lambda b, i: (b, i, col_block))


def _gla_scan(p3, ps3, wa_pad, ba, s0, gn, o_fwd, *, reverse, t):
    bsz, seq_len, _ = p3.shape
    n_blocks = seq_len // t
    n_chunks = t // CHUNK
    final = o_fwd is not None
    spec = functools.partial(_scan_block_spec, n_blocks=n_blocks, reverse=reverse)
    const2 = lambda shape: pl.BlockSpec(shape, lambda b, i: (0, 0))
    state_spec = pl.BlockSpec((None, N_HEADS, HEAD_DIM, HEAD_DIM), lambda b, i: (b, 0, 0, 0))
    in_specs = [
        spec(t, D_GROUP, C_GQ), spec(t, D_GROUP, C_GK), spec(t, D_GROUP, C_GV),
        spec(t, SMALL_COLS, 0),
        const2((3 * SMALL_COLS, D_GROUP)), const2((1, D_GROUP)),
        state_spec,
    ]
    args = [p3, p3, p3, ps3, wa_pad, ba, s0]
    if final:
        in_specs += [spec(t, D_GROUP, 0), spec(t, D_GROUP, C_GG), const2((1, D_GROUP))]
        args += [o_fwd, p3, gn]
    return pl.pallas_call(
        functools.partial(_gla_kernel, reverse=reverse, final=final, n_chunks=n_chunks),
        grid=(bsz, n_blocks),
        in_specs=in_specs,
        out_specs=[spec(t, D_GROUP, 0), state_spec],
        out_shape=[
            jax.ShapeDtypeStruct((bsz, seq_len, D_GROUP), BF16 if final else F32),
            jax.ShapeDtypeStruct((bsz, N_HEADS, HEAD_DIM, HEAD_DIM), F32),
        ],
        scratch_shapes=[
            pltpu.VMEM((N_HEADS, HEAD_DIM, HEAD_DIM), F32),
            pltpu.VMEM((t, D_GROUP), BF16), pltpu.VMEM((t, D_GROUP), BF16),
            pltpu.VMEM((t, D_GROUP), BF16),
            pltpu.VMEM((n_chunks, 1, D_GROUP), F32),
            pltpu.VMEM((n_chunks * N_HEADS, CHUNK, CHUNK), BF16),
            pltpu.VMEM((n_chunks * N_HEADS, HEAD_DIM, HEAD_DIM), F32),
        ] + ([pltpu.VMEM((t, D_GROUP), F32)] if final else []),
        compiler_params=_params(("parallel", "arbitrary")),
        name="gla_bwd" if reverse else "gla_fwd",
    )(*args)


def _mlstm_kernel(*refs, reverse, final, n_chunks, period, lane0):
    if final:
        (qkc_ref, v_ref, gt_ref, gb_ref, bm_ref, c0_ref, m0_ref, hf_ref, og_ref, gn_ref,
         out_ref, cfin_ref, mfin_ref, c_ref, m_ref, fcol_ref, jcol_ref, jt_ref,
         rowb_ref, ub_ref, en_ref, p_ref, s_ref, wi_ref, kwt_ref, kv_ref, r_ref,
         acc_ref) = refs
    else:
        (qk_ref, v_ref, gt_ref, cw_ref, gb_ref, bm_ref, c0_ref, m0_ref,
         out_ref, cfin_ref, mfin_ref, qkc_ref, c_ref, m_ref, fcol_ref, jcol_ref, jt_ref,
         rowb_ref, ub_ref, en_ref, p_ref, s_ref, wi_ref, kwt_ref, kv_ref, r_ref) = refs
        acc_ref = out_ref
        qkc_ref[...] = _dwconv3(qk_ref[...].astype(F32), cw_ref, period).astype(BF16)
    step = pl.program_id(1)

    @pl.when(step == 0)
    def _():
        c_ref[...] = c0_ref[...]
        m_ref[...] = m0_ref[...]

    mask = _tri(reverse)
    tri3 = _tri3(reverse)
    last = 0 if reverse else CHUNK - 1
    ones_col = (lax.broadcasted_iota(jnp.int32, (CHUNK, HEAD_DIM), 1) == 0).astype(BF16)
    chunk_rows = [slice(c * CHUNK, (c + 1) * CHUNK) for c in range(n_chunks)]
    head_cols = [slice(h * HEAD_DIM, (h + 1) * HEAD_DIM) for h in range(N_HEADS)]
    key_cols = [slice(D_GROUP + h * HEAD_DIM, D_GROUP + (h + 1) * HEAD_DIM) for h in range(N_HEADS)]
    f_lane = [lane0 + N_HEADS + h for h in range(N_HEADS)]
    scan_order = list(reversed(range(n_chunks))) if reverse else list(range(n_chunks))

    gates = gt_ref[...] + gb_ref[...]
    log_f = _log_sigmoid(gates)
    i_al = pltpu.roll(gates, N_HEADS, axis=1)
    j_max, f_tot = [], []
    for c, rows in enumerate(chunk_rows):
        f_col = _chunk_cumsum(tri3, log_f[rows, :])
        j_col = i_al[rows, :] - f_col
        fcol_ref[rows, :] = f_col
        jcol_ref[rows, :] = j_col
        jt_ref[c] = j_col.T
        j_max.append(jnp.max(j_col, axis=0, keepdims=True))
        f_tot.append(f_col[last:last + 1, :])

    m_run = m_ref[0:1, :]
    m_prev, u_last = [None] * n_chunks, [None] * n_chunks
    for c in scan_order:
        m_prev[c] = m_run
        u_last[c] = jnp.maximum(m_run, j_max[c])
        m_run = f_tot[c] + u_last[c]
    m_ref[0:1, :] = m_run
    pad_rows = [jnp.zeros((ROW_PAD - n_chunks, LANES), F32)]
    rowb_ref[0] = _lane_spread(jnp.concatenate(m_prev + pad_rows, axis=0), bm_ref)
    rowb_ref[1] = _lane_spread(jnp.concatenate(u_last + pad_rows, axis=0), bm_ref)

    for c, rows in enumerate(chunk_rows):
        u_col = jnp.maximum(_cummax_rows(jcol_ref[rows, :], reverse), m_prev[c])
        ub_ref[rows, :] = _lane_spread(u_col, bm_ref)
        en_ref[rows, :] = jnp.exp(-(fcol_ref[rows, :] + u_col))

    def decay_weights(c, h):
        rows, sl, lf, idx = chunk_rows[c], head_cols[h], f_lane[h], c * N_HEADS + h
        u = ub_ref[rows, sl]
        j_row = jt_ref[c][lf:lf + 1, :]
        a = jnp.where(mask, j_row, -jnp.inf)
        p_ref[idx] = jnp.exp(a - u[:, :CHUNK])
        wi_ref[idx] = jnp.exp(rowb_ref[0][c:c + 1, sl] - u)
        w_k = jnp.exp(j_row - rowb_ref[1][c:c + 1, sl][:, :CHUNK])
        kwt_ref[idx] = (qkc_ref[rows, key_cols[h]].astype(F32).T * w_k).astype(BF16)

    def scores(c, h):
        rows, idx = chunk_rows[c], c * N_HEADS + h
        s = lax.dot_general(qkc_ref[rows, head_cols[h]], qkc_ref[rows, key_cols[h]],
                            (((1,), (1,)), ((), ())), preferred_element_type=F32)
        s_ref[idx] = (s * p_ref[idx]).astype(BF16)

    def local_output(c, h):
        rows, idx = chunk_rows[c], c * N_HEADS + h
        v_aug = jnp.concatenate([v_ref[rows, head_cols[h]], ones_col], axis=1)
        r_ref[rows, 2 * h * HEAD_DIM:2 * (h + 1) * HEAD_DIM] = jnp.dot(
            s_ref[idx], v_aug, preferred_element_type=F32)
        kv_ref[idx] = jnp.dot(kwt_ref[idx], v_aug, preferred_element_type=F32)

    def carry(c, h):
        rows, sl, lf, idx = chunk_rows[c], head_cols[h], f_lane[h], c * N_HEADS + h
        decay = jnp.exp(rowb_ref[0][c:c + 1, sl] - rowb_ref[1][c:c + 1, sl])
        c_aug = c_ref[h]
        wi = wi_ref[idx]
        r = jnp.concatenate([wi, wi], axis=1) * jnp.dot(
            qkc_ref[rows, sl], c_aug.astype(BF16),
            preferred_element_type=F32) + r_ref[rows, 2 * h * HEAD_DIM:2 * (h + 1) * HEAD_DIM]
        den = jnp.maximum(jnp.abs(r[:, HEAD_DIM:HEAD_DIM + 1]), en_ref[rows, lf:lf + 1])
        acc_ref[rows, sl] = r[:, :HEAD_DIM] * (1.0 / den)
        c_ref[h] = c_aug * jnp.concatenate([decay, decay], axis=1) + kv_ref[idx]

    for c in range(n_chunks):
        for h in range(N_HEADS):
            decay_weights(c, h)
    for c in range(n_chunks):
        for h in range(N_HEADS):
            scores(c, h)
    for c in range(n_chunks):
        for h in range(N_HEADS):
            local_output(c, h)
    for c in scan_order:
        for h in range(N_HEADS):
            carry(c, h)

    if final:
        for sl in head_cols:
            out_ref[:, sl] = _head_norm_gate(acc_ref[:, sl] + hf_ref[:, sl], gn_ref[:, sl],
                                             _sigmoid(og_ref[:, sl].astype(F32))
                                             ).astype(out_ref.dtype)

    @pl.when(step == pl.num_programs(1) - 1)
    def _():
        cfin_ref[...] = c_ref[...]
        mfin_ref[...] = m_ref[...]


def _mlstm_scan(p3, ps3, conv_w, gate_b, c0, m0, gn, fwd, *, reverse, t, period):
    bsz, seq_len, _ = p3.shape
    n_blocks = seq_len // t
    n_chunks = t // CHUNK
    final = fwd is not None
    spec = functools.partial(_scan_block_spec, n_blocks=n_blocks, reverse=reverse)
    const2 = lambda shape: pl.BlockSpec(shape, lambda b, i: (0, 0))
    c_spec = pl.BlockSpec((None, N_HEADS, HEAD_DIM, 2 * HEAD_DIM), lambda b, i: (b, 0, 0, 0))
    m_spec = pl.BlockSpec((None, 8, LANES), lambda b, i: (b, 0, 0))
    out_specs = [spec(t, D_GROUP, 0), c_spec, m_spec]
    out_shape = [
        jax.ShapeDtypeStruct((bsz, seq_len, D_GROUP), BF16 if final else F32),
        jax.ShapeDtypeStruct((bsz, N_HEADS, HEAD_DIM, 2 * HEAD_DIM), F32),
        jax.ShapeDtypeStruct((bsz, 8, LANES), F32),
    ]
    lane0 = GATE0 + (2 * N_HEADS if reverse else 0)
    lane = np.arange(3 * SMALL_COLS)[:, None] % SMALL_COLS
    head = np.arange(D_GROUP)[None, :] // HEAD_DIM
    spread = jnp.asarray(lane == lane0 + N_HEADS + head, BF16)
    if final:
        h_fwd, qkc = fwd
        in_specs = [
            spec(t, 2 * D_GROUP, 0), spec(t, D_GROUP, C_MV), spec(t, SMALL_COLS, 0),
            const2((1, SMALL_COLS)), const2((3 * SMALL_COLS, D_GROUP)), c_spec, m_spec,
            spec(t, D_GROUP, 0), spec(t, D_GROUP, C_MO), const2((1, D_GROUP)),
        ]
        args = [qkc, p3, ps3, gate_b, spread, c0, m0, h_fwd, p3, gn]
    else:
        in_specs = [
            spec(t, 2 * D_GROUP, C_MQK // 2), spec(t, D_GROUP, C_MV), spec(t, SMALL_COLS, 0),
            const2((3, 2 * D_GROUP)), const2((1, SMALL_COLS)),
            const2((3 * SMALL_COLS, D_GROUP)), c_spec, m_spec,
        ]
        args = [p3, p3, ps3, conv_w, gate_b, spread, c0, m0]
        out_specs.append(spec(t, 2 * D_GROUP, 0))
        out_shape.append(jax.ShapeDtypeStruct((bsz, seq_len, 2 * D_GROUP), BF16))
    units = n_chunks * N_HEADS
    return pl.pallas_call(
        functools.partial(_mlstm_kernel, reverse=reverse, final=final, n_chunks=n_chunks,
                          period=period, lane0=lane0),
        grid=(bsz, n_blocks),
        in_specs=in_specs,
        out_specs=out_specs,
        out_shape=out_shape,
        scratch_shapes=[
            pltpu.VMEM((N_HEADS, HEAD_DIM, 2 * HEAD_DIM), F32),
            pltpu.VMEM((8, LANES), F32),
            pltpu.VMEM((t, SMALL_COLS), F32),
            pltpu.VMEM((t, SMALL_COLS), F32),
            pltpu.VMEM((n_chunks, SMALL_COLS, CHUNK), F32),
            pltpu.VMEM((2, ROW_PAD, D_GROUP), F32),
            pltpu.VMEM((t, D_GROUP), F32),
            pltpu.VMEM((t, SMALL_COLS), F32),
            pltpu.VMEM((units, CHUNK, CHUNK), F32),
            pltpu.VMEM((units, CHUNK, CHUNK), BF16),
            pltpu.VMEM((units, CHUNK, LANES), F32),
            pltpu.VMEM((units, HEAD_DIM, CHUNK), BF16),
            pltpu.VMEM((units, HEAD_DIM, 2 * HEAD_DIM), F32),
            pltpu.VMEM((t, 2 * D_GROUP), F32),
        ] + ([pltpu.VMEM((t, D_GROUP), F32)] if final else []),
        compiler_params=_params(("parallel", "arbitrary")),
        name="mlstm_bwd" if reverse else "mlstm_fwd",
    )(*args)


def _sccm_kernel(bg_ref, cg_ref, hh_ref, u_ref, v_ref, scw_ref, ws_ref, bs_ref, gv_ref,
                 ysc_ref, ycm_ref, *, period):
    gated = cg_ref[...].astype(F32) * hh_ref[...].astype(F32)
    ysc_ref[...] = (bg_ref[...].astype(F32) * _dwconv3(gated, scw_ref, period)
                    ).astype(ysc_ref.dtype)
    v = _gelu_tanh(v_ref[...].astype(F32))
    vn = (v * lax.rsqrt(jnp.mean(v * v, axis=-1, keepdims=True) + EPS) * gv_ref[...]).astype(BF16)
    for n in range(v.shape[0] // CM_CHUNK):
        rows = slice(n * CM_CHUNK, (n + 1) * CM_CHUNK)
        for g in range(N_HEADS):
            sl = slice(g * HEAD_DIM, (g + 1) * HEAD_DIM)
            sv = jnp.dot(ws_ref[g], vn[rows, sl], preferred_element_type=F32) + bs_ref[:, sl]
            ycm_ref[rows, sl] = (_gelu_tanh(u_ref[rows, sl].astype(F32)) * sv
                                 ).astype(ycm_ref.dtype)


def _sc_cm_mixers(p2, sc_w, ws, bs_b, gv, *, t, period):
    m = p2.shape[0]
    col = lambda cb: pl.BlockSpec((t, D_GROUP), lambda i: (i, cb))
    const = lambda shape: pl.BlockSpec(shape, lambda i: (0,) * len(shape))
    return pl.pallas_call(
        functools.partial(_sccm_kernel, period=period),
        grid=(m // t,),
        in_specs=[col(C_SB), col(C_SC), col(C_SH), col(C_CU), col(C_CV),
                  const((3, D_GROUP)), const((N_HEADS, CM_CHUNK, CM_CHUNK)),
                  const((CM_CHUNK, D_GROUP)), const((1, D_GROUP))],
        out_specs=[pl.BlockSpec((t, D_GROUP), lambda i: (i, 0))] * 2,
        out_shape=[jax.ShapeDtypeStruct((m, D_GROUP), BF16)] * 2,
        compiler_params=_params(("parallel",)),
        name="sc_cm_mixers",
    )(p2, p2, p2, p2, p2, sc_w, ws, bs_b, gv)


def _outproj_kernel(y0_ref, y1_ref, y2_ref, y3_ref, w_ref, x_ref, g_ref, o_ref):
    acc = jnp.dot(y0_ref[...], w_ref[0:D_GROUP, :], preferred_element_type=F32)
    for n, y_ref in enumerate((y1_ref, y2_ref, y3_ref), start=1):
        acc = acc + jnp.dot(y_ref[...], w_ref[n * D_GROUP:(n + 1) * D_GROUP, :],
                            preferred_element_type=F32)
    o_ref[...] = x_ref[...] + g_ref[...] * acc


def _out_projection(ys, w_out, layer, x2, mod, seq_len, tm):
    m, d = x2.shape
    y_spec = pl.BlockSpec((tm, D_GROUP), lambda i: (i, 0))
    return pl.pallas_call(
        _outproj_kernel,
        grid=(m // tm,),
        in_specs=[y_spec] * N_MIXERS + [
            pl.BlockSpec((None, d, d), lambda i: (layer, 0, 0), pipeline_mode=pl.Buffered(1)),
            pl.BlockSpec((tm, d), lambda i: (i, 0)),
            _row_spec(2, seq_len, tm, d),
        ],
        out_specs=pl.BlockSpec((tm, d), lambda i: (i, 0)),
        out_shape=jax.ShapeDtypeStruct((m, d), F32),
        compiler_params=_params(("parallel",)),
        name="out_projection",
    )(*ys, w_out, x2, mod)


def _ffn_kernel(*refs, period, final):
    if final:
        (x_ref, g_ref, sh_ref, sc_ref, gate_ref, wa_ref, wv_ref, cw_ref, wd_ref, fn_ref,
         o_ref, hn_ref) = refs
    else:
        (x_ref, g_ref, sh_ref, sc_ref, gate_ref, wa_ref, wv_ref, cw_ref, wd_ref,
         o_ref, hn_ref) = refs
    j = pl.program_id(1)
    last = pl.num_programs(1) - 1

    @pl.when(j == 0)
    def _():
        hn_ref[...] = _modulated_norm(x_ref[...], g_ref[...], sc_ref[...], sh_ref[...]).astype(BF16)

    def tile_contribution():
        hn = hn_ref[...]
        a = _dwconv3(jnp.dot(hn, wa_ref[...], preferred_element_type=F32), cw_ref, period)
        v = jnp.dot(hn, wv_ref[...], preferred_element_type=F32)
        hid = (a * _sigmoid(a) * v).astype(BF16)
        return jnp.dot(hid, wd_ref[...], preferred_element_type=F32)

    @pl.when(j == 0)
    def _():
        o_ref[...] = tile_contribution()

    @pl.when(jnp.logical_and(j > 0, j < last))
    def _():
        o_ref[...] += tile_contribution()

    @pl.when(j == last)
    def _():
        y = x_ref[...] + gate_ref[...] * (o_ref[...] + tile_contribution())
        if final:
            y = y * lax.rsqrt(jnp.mean(y * y, axis=-1, keepdims=True) + EPS) * fn_ref[...]
        o_ref[...] = y


def _conv_ffn(x2, gain, mod, w_up, w_conv, w_down, layer, final_gain, seq_len, period, tm,
              tf=512):
    m, d = x2.shape
    d_ff = w_down.shape[1]
    n_f = d_ff // tf
    final = final_gain is not None
    in_specs = [
        pl.BlockSpec((tm, d), lambda i, j: (i, 0)),
        pl.BlockSpec((1, d), lambda i, j: (0, 0)),
        _row_spec(3, seq_len, tm, d), _row_spec(4, seq_len, tm, d), _row_spec(5, seq_len, tm, d),
        pl.BlockSpec((None, d, tf), lambda i, j: (layer, 0, j)),
        pl.BlockSpec((None, d, tf), lambda i, j: (layer, 0, j + n_f)),
        pl.BlockSpec((None, 3, tf), lambda i, j: (layer, 0, j)),
        pl.BlockSpec((None, tf, d), lambda i, j: (layer, j, 0)),
    ]
    args = [x2, gain.reshape(1, d), mod, mod, mod, w_up, w_up, w_conv, w_down]
    if final:
        in_specs.append(pl.BlockSpec((1, d), lambda i, j: (0, 0)))
        args.append(final_gain.reshape(1, d))
    return pl.pallas_call(
        functools.partial(_ffn_kernel, period=period, final=final),
        grid=(m // tm, n_f),
        in_specs=in_specs,
        out_specs=pl.BlockSpec((tm, d), lambda i, j: (i, 0)),
        out_shape=jax.ShapeDtypeStruct((m, d), F32),
        scratch_shapes=[pltpu.VMEM((tm, d), BF16)],
        compiler_params=pltpu.CompilerParams(dimension_semantics=("parallel", "arbitrary"),
                                             vmem_limit_bytes=FFN_VMEM_LIMIT),
        name="conv_ffn",
    )(*args)


def _relayout_w_in(w_in):
    depth, d, _ = w_in.shape
    gla = 4 * D_GROUP + 2 * GLA_LOWRANK
    ml = 4 * D_GROUP + 4 * N_HEADS
    wide = jnp.concatenate([w_in[..., :4 * D_GROUP], w_in[..., gla:gla + 4 * D_GROUP],
                            w_in[..., gla + ml:]], axis=-1).astype(BF16)
    small = jnp.concatenate([
        w_in[..., 4 * D_GROUP:gla], w_in[..., gla + 4 * D_GROUP:gla + ml],
        jnp.zeros((depth, d, SMALL_COLS - 2 * GLA_LOWRANK - 4 * N_HEADS), w_in.dtype)],
        axis=-1).astype(BF16)
    return wide, small


def _mixers(p2, ps2, bsz, seq_len, prm, states, *, t_scan, period, need_out):
    p3 = p2.reshape(bsz, seq_len, -1)
    ps3 = ps2.reshape(bsz, seq_len, SMALL_COLS)
    (gs_f, gs_b), (mc_f, mm_f, mc_b, mm_b) = states
    o_f, gs_f = _gla_scan(p3, ps3, prm["wa_f"], prm["ba_f"], gs_f, None, None,
                          reverse=False, t=t_scan)
    y_gla, gs_b = _gla_scan(p3, ps3, prm["wa_b"], prm["ba_b"], gs_b, prm["gla_norm"], o_f,
                            reverse=True, t=t_scan)
    h_f, mc_f, mm_f, qkc = _mlstm_scan(p3, ps3, prm["ml_conv"], prm["ml_gate_b"], mc_f, mm_f,
                                       None, None, reverse=False, t=t_scan, period=period)
    y_ml, mc_b, mm_b = _mlstm_scan(p3, ps3, None, prm["ml_gate_b"], mc_b, mm_b,
                                   prm["ml_norm"], (h_f, qkc), reverse=True, t=t_scan,
                                   period=period)
    new_states = ((gs_f, gs_b), (mc_f, mm_f, mc_b, mm_b))
    if not need_out:
        return None, new_states
    y_sc, y_cm = _sc_cm_mixers(p2, prm["sc_conv"], prm["cm_ws"], prm["cm_bs"], prm["cm_norm"],
                               t=min(512, seq_len), period=period)
    m = bsz * seq_len
    return (y_gla.reshape(m, D_GROUP), y_ml.reshape(m, D_GROUP), y_sc, y_cm), new_states


def kernel(x, c, ctx, c_ctx, w_mod, b_mod, norm1, norm2, w_in, w_out, gla_wa2, gla_ba, gla_norm,
           mlstm_conv, mlstm_gate_b, mlstm_norm, sc_conv, cm_ws, cm_bs, cm_norm,
           ffn_up, ffn_conv, ffn_down, final_norm):
    bsz, seq_len, d = x.shape
    ctx_len = ctx.shape[1]
    depth = w_mod.shape[0]
    assert seq_len % 512 == 0 and ctx_len % CM_CHUNK == 0 and bsz + 1 <= 8

    cond = jnp.zeros((8, d), F32).at[:bsz].set(c).at[bsz].set(c_ctx)
    mod = _modulation(cond, w_mod, b_mod).reshape(depth, 8, 6, d).transpose(0, 2, 1, 3)

    xl = x.reshape(bsz * seq_len, d)
    xc = ctx.reshape(bsz * ctx_len, d)
    tm_l, tm_c = 512, bsz * ctx_len
    tm_in = 512
    tm_ffn = 1024

    zero_states = (
        (jnp.zeros((bsz, N_HEADS, HEAD_DIM, HEAD_DIM), F32),) * 2,
        (jnp.zeros((bsz, N_HEADS, HEAD_DIM, 2 * HEAD_DIM), F32), jnp.zeros((bsz, 8, LANES), F32)) * 2,
    )

    w_wide_all, w_small_all = _relayout_w_in(w_in)
    w_out_b, w_up_b, w_down_b = w_out.astype(BF16), ffn_up.astype(BF16), ffn_down.astype(BF16)
    k_scale = jnp.concatenate([jnp.ones((1, D_GROUP), F32),
                               jnp.full((1, D_GROUP), HEAD_DIM ** -0.5, F32)], axis=1)
    for l in range(depth):
        need_ctx = l < depth - 1
        mod_l = mod[l][:, :bsz, None, :]
        mod_c = jnp.broadcast_to(mod[l][:, bsz:bsz + 1, None, :], (6, bsz, 1, d))
        wa = jnp.zeros((2, SMALL_COLS, D_GROUP), F32)
        wa = wa.at[0, Z_FWD:Z_FWD + GLA_LOWRANK].set(gla_wa2[l, 0])
        wa = wa.at[1, Z_BWD:Z_BWD + GLA_LOWRANK].set(gla_wa2[l, 1])
        wa_hi = wa.astype(BF16)
        wa_lo = (wa - wa_hi.astype(F32)).astype(BF16)
        wa = jnp.concatenate([wa_hi, wa_hi, wa_lo], axis=1)
        gate_b = jnp.zeros((1, SMALL_COLS), F32).at[0, GATE0:GATE0 + 4 * N_HEADS].set(
            mlstm_gate_b[l].reshape(-1))
        prm = dict(
            wa_f=wa[0], wa_b=wa[1], ba_f=gla_ba[l, 0:1], ba_b=gla_ba[l, 1:2],
            gla_norm=gla_norm[l].reshape(1, D_GROUP),
            ml_conv=mlstm_conv[l] * k_scale, ml_gate_b=gate_b,
            ml_norm=mlstm_norm[l].reshape(1, D_GROUP),
            sc_conv=sc_conv[l], cm_ws=cm_ws[l].astype(BF16),
            cm_bs=jnp.repeat(cm_bs[l].T, HEAD_DIM, axis=1),
            cm_norm=cm_norm[l].reshape(1, D_GROUP),
        )

        pc, psc = _in_projection(xc, norm1[l], mod_c, w_wide_all, w_small_all, l, ctx_len, tm_c)
        yc, ctx_states = _mixers(pc, psc, bsz, ctx_len, prm, zero_states,
                                 t_scan=ctx_len, period=ctx_len, need_out=need_ctx)
        p_l, ps_l = _in_projection(xl, norm1[l], mod_l, w_wide_all, w_small_all, l, seq_len, tm_in)
        yl, _ = _mixers(p_l, ps_l, bsz, seq_len, prm, ctx_states,
                        t_scan=512, period=GRID_W, need_out=True)

        xl = _out_projection(yl, w_out_b, l, xl, mod_l, seq_len, tm_l)
        xl = _conv_ffn(xl, norm2[l], mod_l, w_up_b, ffn_conv, w_down_b, l,
                       final_norm if l == depth - 1 else None, seq_len, GRID_W, tm_ffn)
        if need_ctx:
            xc = _out_projection(yc, w_out_b, l, xc, mod_c, ctx_len, tm_c)
            xc = _conv_ffn(xc, norm2[l], mod_c, w_up_b, ffn_conv, w_down_b, l, None,
                           ctx_len, ctx_len, tm_c)
    return xl.reshape(bsz, seq_len, d)
```

```python
import functools

import numpy as np
import jax
import jax.numpy as jnp
from jax import lax
from jax.experimental import pallas as pl
from jax.experimental.pallas import tpu as pltpu

F32 = jnp.float32
BF16 = jnp.bfloat16

GRID_W = 64
N_MIXERS = 4
HEAD_DIM = 128
N_HEADS = 4
D_GROUP = N_HEADS * HEAD_DIM
GLA_LOWRANK = 16
GLA_GATE_NORM = 16.0
CHUNK = 64
CM_CHUNK = 128
EPS = 1e-6
LANES = 128
SMALL_COLS = LANES
Z_FWD, Z_BWD, GATE0 = 0, GLA_LOWRANK, 2 * GLA_LOWRANK
VMEM_LIMIT = 56 * 1024 * 1024
INPROJ_VMEM_LIMIT = 60 * 1024 * 1024
FFN_VMEM_LIMIT = 62 * 1024 * 1024

(C_GQ, C_GK, C_GV, C_GG, C_MQK, C_MV, C_MO, C_SB, C_SC, C_SH, C_CU, C_CV) = (
    0, 1, 2, 3, 4, 6, 7, 8, 9, 10, 11, 12)
N_WIDE_BLOCKS = 13
N_SCAN_BLOCKS = 8
ROW_PAD = 16


def _params(sem):
    return pltpu.CompilerParams(dimension_semantics=sem, vmem_limit_bytes=VMEM_LIMIT)


def _sigmoid(x):
    return 1.0 / (1.0 + jnp.exp(-x))


def _log_sigmoid(x):
    return jnp.minimum(x, 0.0) - jnp.log(1.0 + jnp.exp(-jnp.abs(x)))


def _gelu_tanh(x):
    return x * (0.5 * (1.0 + jnp.tanh(0.7978845608028654 * (x + 0.044715 * (x * x * x)))))


def _shift_rows(x, period):
    n = x.shape[0]
    row = lax.broadcasted_iota(jnp.int32, (n, 1), 0) % period
    prev = jnp.where(row == 0, 0.0, pltpu.roll(x, 1, axis=0))
    nxt = jnp.where(row == period - 1, 0.0, pltpu.roll(x, n - 1, axis=0))
    return prev, nxt


def _dwconv3(x, w_ref, period):
    prev, nxt = _shift_rows(x, period)
    return w_ref[0:1, :] * prev + w_ref[1:2, :] * x + w_ref[2:3, :] * nxt


def _mod_kernel(c_ref, w_ref, b_ref, o_ref):
    s = c_ref[...]
    s = (s * _sigmoid(s)).astype(BF16)
    o_ref[...] = jnp.dot(s, w_ref[...].astype(BF16), preferred_element_type=F32) + b_ref[...]


def _modulation(cond, w_mod, b_mod, tn=1024):
    depth, d, n = w_mod.shape
    rows = cond.shape[0]
    return pl.pallas_call(
        _mod_kernel,
        grid=(depth, n // tn),
        in_specs=[
            pl.BlockSpec((rows, d), lambda l, j: (0, 0)),
            pl.BlockSpec((None, d, tn), lambda l, j: (l, 0, j)),
            pl.BlockSpec((None, 1, tn), lambda l, j: (l, 0, j)),
        ],
        out_specs=pl.BlockSpec((None, rows, tn), lambda l, j: (l, 0, j)),
        out_shape=jax.ShapeDtypeStruct((depth, rows, n), F32),
        compiler_params=_params(("parallel", "parallel")),
        name="modulation",
    )(cond, w_mod, b_mod.reshape(depth, 1, n))


def _modulated_norm(x, g, sc, sh):
    r = lax.rsqrt(jnp.mean(x * x, axis=-1, keepdims=True) + EPS)
    return (x * r * g) * (1.0 + sc) + sh


def _inproj_kernel(x_ref, g_ref, sh_ref, sc_ref, w_ref, ws_ref, scw_ref, cws_ref, cbs_ref,
                   cgv_ref, p_ref, ps_ref, ysc_ref, ycm_ref, *, period):
    hn = _modulated_norm(x_ref[...], g_ref[...], sc_ref[...], sh_ref[...]).astype(BF16)

    def proj(block):
        cols = slice(block * D_GROUP, (block + 1) * D_GROUP)
        return jnp.dot(hn, w_ref[:, cols], preferred_element_type=F32)

    ysc_ref[...] = (proj(C_SB) * _dwconv3(proj(C_SC) * proj(C_SH), scw_ref, period)
                    ).astype(ysc_ref.dtype)

    v = _gelu_tanh(proj(C_CV))
    vn = (v * lax.rsqrt(jnp.mean(v * v, axis=-1, keepdims=True) + EPS) * cgv_ref[...]).astype(BF16)
    u = _gelu_tanh(proj(C_CU))
    for n in range(v.shape[0] // CM_CHUNK):
        rows = slice(n * CM_CHUNK, (n + 1) * CM_CHUNK)
        for g in range(N_HEADS):
            sl = slice(g * HEAD_DIM, (g + 1) * HEAD_DIM)
            sv = jnp.dot(cws_ref[g], vn[rows, sl], preferred_element_type=F32) + cbs_ref[:, sl]
            ycm_ref[rows, sl] = (u[rows, sl] * sv).astype(ycm_ref.dtype)

    ps_ref[...] = jnp.dot(hn, ws_ref[...], preferred_element_type=F32)
    for block in range(N_SCAN_BLOCKS):
        p_ref[:, block * D_GROUP:(block + 1) * D_GROUP] = proj(block).astype(p_ref.dtype)


def _row_spec(piece, seq_len, tm, d):
    return pl.BlockSpec((None, None, 1, d), lambda i, *_: (piece, (i * tm) // seq_len, 0, 0))


def _in_projection(x2, gain, mod, w_wide, w_small, layer, prm, seq_len, tm, period):
    m, d = x2.shape
    n = w_wide.shape[2]
    n_scan = N_SCAN_BLOCKS * D_GROUP
    resident = lambda shape: pl.BlockSpec((None,) + shape, lambda i: (layer, 0, 0),
                                          pipeline_mode=pl.Buffered(1))
    const = lambda shape: pl.BlockSpec(shape, lambda i: (0,) * len(shape))
    tile = lambda cols: pl.BlockSpec((tm, cols), lambda i: (i, 0))
    return pl.pallas_call(
        functools.partial(_inproj_kernel, period=period),
        grid=(m // tm,),
        in_specs=[
            tile(d),
            const((1, d)),
            _row_spec(0, seq_len, tm, d),
            _row_spec(1, seq_len, tm, d),
            resident((d, n)),
            resident((d, SMALL_COLS)),
            const((3, D_GROUP)), const((N_HEADS, CM_CHUNK, CM_CHUNK)),
            const((CM_CHUNK, D_GROUP)), const((1, D_GROUP)),
        ],
        out_specs=[tile(n_scan), tile(SMALL_COLS), tile(D_GROUP), tile(D_GROUP)],
        out_shape=[
            jax.ShapeDtypeStruct((m, n_scan), BF16),
            jax.ShapeDtypeStruct((m, SMALL_COLS), F32),
            jax.ShapeDtypeStruct((m, D_GROUP), BF16),
            jax.ShapeDtypeStruct((m, D_GROUP), BF16),
        ],
        compiler_params=pltpu.CompilerParams(dimension_semantics=("parallel",),
                                             vmem_limit_bytes=INPROJ_VMEM_LIMIT),
        name="in_projection",
    )(x2, gain.reshape(1, d), mod, mod, w_wide, w_small,
      prm["sc_conv"], prm["cm_ws"], prm["cm_bs"], prm["cm_norm"])


def _tri(reverse):
    r = lax.broadcasted_iota(jnp.int32, (CHUNK, CHUNK), 0)
    c = lax.broadcasted_iota(jnp.int32, (CHUNK, CHUNK), 1)
    return (c >= r) if reverse else (c <= r)


def _tri2(reverse):
    r = lax.broadcasted_iota(jnp.int32, (CHUNK, 2 * CHUNK), 0)
    c = lax.broadcasted_iota(jnp.int32, (CHUNK, 2 * CHUNK), 1) % CHUNK
    return ((c >= r) if reverse else (c <= r)).astype(BF16)


def _split2(x):
    hi = x.astype(BF16)
    return hi, (x - hi.astype(F32)).astype(BF16)


def _chunk_cumsum(tri2, x):
    hi, lo = _split2(x)
    return jnp.dot(tri2, jnp.concatenate([hi, lo], axis=0), preferred_element_type=F32)


def _emit_skewed(order, chunk_stages, head_stages):
    n_stage = len(chunk_stages) + len(head_stages)
    for slot in range(len(order) + n_stage - 1):
        for k, stage in enumerate(chunk_stages):
            if 0 <= slot - k < len(order):
                stage(order[slot - k])
        for h in range(N_HEADS):
            for k, stage in enumerate(head_stages, start=len(chunk_stages)):
                if 0 <= slot - k < len(order):
                    stage(order[slot - k], h)


def _cummax_rows(x, reverse):
    n = x.shape[0]
    row = lax.broadcasted_iota(jnp.int32, (n, 1), 0)
    k = 1
    while k < n:
        if reverse:
            shifted = jnp.where(row >= n - k, -jnp.inf, pltpu.roll(x, n - k, axis=0))
        else:
            shifted = jnp.where(row < k, -jnp.inf, pltpu.roll(x, k, axis=0))
        x = jnp.maximum(x, shifted)
        k *= 2
    return x


def _lane_spread(x, bm_ref):
    hi, lo = _split2(x)
    return jnp.dot(jnp.concatenate([hi, lo], axis=1), bm_ref[...], preferred_element_type=F32)


def _head_norm_gate(o, gain, gate):
    r = lax.rsqrt(jnp.mean(o * o, axis=-1, keepdims=True) + EPS)
    return o * r * gain * gate


def _gla_kernel(*refs, reverse, final, n_chunks):
    if final:
        (q_ref, k_ref, v_ref, z_ref, wa_ref, ba_ref, s0_ref, of_ref, g_ref, gn_ref,
         out_ref, sfin_ref, st_ref, qin_ref, qrel_ref, krel_ref, dec_ref, att_ref, kv_ref,
         acc_ref) = refs
    else:
        (q_ref, k_ref, v_ref, z_ref, wa_ref, ba_ref, s0_ref,
         out_ref, sfin_ref, st_ref, qin_ref, qrel_ref, krel_ref, dec_ref, att_ref,
         kv_ref) = refs
        acc_ref = out_ref
    step = pl.program_id(1)

    @pl.when(step == 0)
    def _():
        st_ref[...] = s0_ref[...]

    mask = _tri(reverse)
    tri2 = _tri2(reverse)
    last = 0 if reverse else CHUNK - 1
    scale = HEAD_DIM ** -0.5
    chunk_rows = [slice(c * CHUNK, (c + 1) * CHUNK) for c in range(n_chunks)]
    head_cols = [slice(h * HEAD_DIM, (h + 1) * HEAD_DIM) for h in range(N_HEADS)]

    z_hi, z_lo = _split2(z_ref[...])
    pre = jnp.dot(jnp.concatenate([z_hi, z_lo, z_hi], axis=1), wa_ref[...],
                  preferred_element_type=F32) + ba_ref[...]
    log_a = _log_sigmoid(pre) * (1.0 / GLA_GATE_NORM)

    def decayed_qk(c):
        rows = chunk_rows[c]
        b_cum = _chunk_cumsum(tri2, log_a[rows, :])
        b_last = b_cum[last:last + 1, :]
        q = q_ref[rows, :].astype(F32) * scale
        qin_ref[rows, :] = (q * jnp.exp(b_cum)).astype(BF16)
        qrel_ref[rows, :] = (q * jnp.exp(b_cum - b_last)).astype(BF16)
        krel_ref[rows, :] = (k_ref[rows, :].astype(F32) * jnp.exp(b_last - b_cum)).astype(BF16)
        dec_ref[c] = jnp.exp(b_last)

    def scores(c, h):
        rows, sl = chunk_rows[c], head_cols[h]
        att = lax.dot_general(qrel_ref[rows, sl], krel_ref[rows, sl],
                              (((1,), (1,)), ((), ())), preferred_element_type=F32)
        att_ref[c * N_HEADS + h] = jnp.where(mask, att, 0.0).astype(BF16)

    def local_output(c, h):
        rows, sl = chunk_rows[c], head_cols[h]
        v = v_ref[rows, sl]
        acc_ref[rows, sl] = jnp.dot(att_ref[c * N_HEADS + h], v, preferred_element_type=F32)
        kv_ref[c * N_HEADS + h] = lax.dot_general(
            v, krel_ref[rows, sl], (((0,), (0,)), ((), ())), preferred_element_type=F32)

    def carry(c, h):
        rows, sl = chunk_rows[c], head_cols[h]
        s_t = st_ref[h]
        acc_ref[rows, sl] += lax.dot_general(qin_ref[rows, sl], s_t.astype(BF16),
                                             (((1,), (1,)), ((), ())),
                                             preferred_element_type=F32)
        st_ref[h] = s_t * dec_ref[c][:, sl] + kv_ref[c * N_HEADS + h]

    _emit_skewed(list(reversed(range(n_chunks))) if reverse else list(range(n_chunks)),
                 [decayed_qk], [scores, local_output, carry])

    if final:
        for sl in head_cols:
            g = g_ref[:, sl].astype(F32)
            out_ref[:, sl] = _head_norm_gate(acc_ref[:, sl] + of_ref[:, sl], gn_ref[:, sl],
                                             g * _sigmoid(g)).astype(out_ref.dtype)

    @pl.when(step == pl.num_programs(1) - 1)
    def _():
        sfin_ref[...] = st_ref[...]


def _scan_block_spec(t, cols, col_block, n_blocks, reverse):
    if reverse:
        return pl.BlockSpec((None, t, cols), lambda b, i: (b, n_blocks - 1 - i, col_block))
    return pl.BlockSpec((None, t, cols), lambda b, i: (b, i, col_block))


def _gla_scan(p3, ps3, wa_pad, ba, s0, gn, o_fwd, *, reverse, t):
    bsz, seq_len, _ = p3.shape
    n_blocks = seq_len // t
    n_chunks = t // CHUNK
    final = o_fwd is not None
    spec = functools.partial(_scan_block_spec, n_blocks=n_blocks, reverse=reverse)
    const2 = lambda shape: pl.BlockSpec(shape, lambda b, i: (0, 0))
    state_spec = pl.BlockSpec((None, N_HEADS, HEAD_DIM, HEAD_DIM), lambda b, i: (b, 0, 0, 0))
    in_specs = [
        spec(t, D_GROUP, C_GQ), spec(t, D_GROUP, C_GK), spec(t, D_GROUP, C_GV),
        spec(t, SMALL_COLS, 0),
        const2((3 * SMALL_COLS, D_GROUP)), const2((1, D_GROUP)),
        state_spec,
    ]
    args = [p3, p3, p3, ps3, wa_pad, ba, s0]
    if final:
        in_specs += [spec(t, D_GROUP, 0), spec(t, D_GROUP, C_GG), const2((1, D_GROUP))]
        args += [o_fwd, p3, gn]
    return pl.pallas_call(
        functools.partial(_gla_kernel, reverse=reverse, final=final, n_chunks=n_chunks),
        grid=(bsz, n_blocks),
        in_specs=in_specs,
        out_specs=[spec(t, D_GROUP, 0), state_spec],
        out_shape=[
            jax.ShapeDtypeStruct((bsz, seq_len, D_GROUP), BF16 if final else F32),
            jax.ShapeDtypeStruct((bsz, N_HEADS, HEAD_DIM, HEAD_DIM), F32),
        ],
        scratch_shapes=[
            pltpu.VMEM((N_HEADS, HEAD_DIM, HEAD_DIM), F32),
            pltpu.VMEM((t, D_GROUP), BF16), pltpu.VMEM((t, D_GROUP), BF16),
            pltpu.VMEM((t, D_GROUP), BF16),
            pltpu.VMEM((n_chunks, 1, D_GROUP), F32),
            pltpu.VMEM((n_chunks * N_HEADS, CHUNK, CHUNK), BF16),
            pltpu.VMEM((n_chunks * N_HEADS, HEAD_DIM, HEAD_DIM), F32),
        ] + ([pltpu.VMEM((t, D_GROUP), F32)] if final else []),
        compiler_params=_params(("parallel", "arbitrary")),
        name="gla_bwd" if reverse else "gla_fwd",
    )(*args)


def _mlstm_kernel(*refs, reverse, final, n_chunks, period, lane0):
    if final:
        (qkc_ref, v_ref, gt_ref, gb_ref, bm_ref, c0_ref, m0_ref, hf_ref, og_ref, gn_ref,
         out_ref, cfin_ref, mfin_ref, c_ref, m_ref, fcol_ref, jcol_ref, jt_ref,
         rowb_ref, ub_ref, en_ref, p_ref, s_ref, wi_ref, kwt_ref, kv_ref, r_ref,
         acc_ref) = refs
    else:
        (qk_ref, v_ref, gt_ref, cw_ref, gb_ref, bm_ref, c0_ref, m0_ref,
         out_ref, cfin_ref, mfin_ref, qkc_ref, c_ref, m_ref, fcol_ref, jcol_ref, jt_ref,
         rowb_ref, ub_ref, en_ref, p_ref, s_ref, wi_ref, kwt_ref, kv_ref, r_ref) = refs
        acc_ref = out_ref
        qkc_ref[...] = _dwconv3(qk_ref[...].astype(F32), cw_ref, period).astype(BF16)
    step = pl.program_id(1)

    @pl.when(step == 0)
    def _():
        c_ref[...] = c0_ref[...]
        m_ref[...] = m0_ref[...]

    mask = _tri(reverse)
    tri2 = _tri2(reverse)
    last = 0 if reverse else CHUNK - 1
    ones_col = (lax.broadcasted_iota(jnp.int32, (CHUNK, HEAD_DIM), 1) == 0).astype(BF16)
    chunk_rows = [slice(c * CHUNK, (c + 1) * CHUNK) for c in range(n_chunks)]
    head_cols = [slice(h * HEAD_DIM, (h + 1) * HEAD_DIM) for h in range(N_HEADS)]
    key_cols = [slice(D_GROUP + h * HEAD_DIM, D_GROUP + (h + 1) * HEAD_DIM) for h in range(N_HEADS)]
    f_lane = [lane0 + N_HEADS + h for h in range(N_HEADS)]
    scan_order = list(reversed(range(n_chunks))) if reverse else list(range(n_chunks))

    gates = gt_ref[...] + gb_ref[...]
    log_f = _log_sigmoid(gates)
    i_al = pltpu.roll(gates, N_HEADS, axis=1)
    j_max, f_tot = [], []
    for c, rows in enumerate(chunk_rows):
        f_col = _chunk_cumsum(tri2, log_f[rows, :])
        j_col = i_al[rows, :] - f_col
        fcol_ref[rows, :] = f_col
        jcol_ref[rows, :] = j_col
        jt_ref[c] = j_col.T
        j_max.append(jnp.max(j_col, axis=0, keepdims=True))
        f_tot.append(f_col[last:last + 1, :])

    m_run = m_ref[0:1, :]
    m_prev, u_last = [None] * n_chunks, [None] * n_chunks
    for c in scan_order:
        m_prev[c] = m_run
        u_last[c] = jnp.maximum(m_run, j_max[c])
        m_run = f_tot[c] + u_last[c]
    m_ref[0:1, :] = m_run
    pad_rows = [jnp.zeros((ROW_PAD - n_chunks, LANES), F32)]
    rowb_ref[0] = _lane_spread(jnp.concatenate(m_prev + pad_rows, axis=0), bm_ref)
    rowb_ref[1] = _lane_spread(jnp.concatenate(u_last + pad_rows, axis=0), bm_ref)

    for c, rows in enumerate(chunk_rows):
        u_col = jnp.maximum(_cummax_rows(jcol_ref[rows, :], reverse), m_prev[c])
        ub_ref[rows, :] = _lane_spread(u_col, bm_ref)
        en_ref[rows, :] = jnp.exp(-(fcol_ref[rows, :] + u_col))

    def decay_weights(c, h):
        rows, sl, lf, idx = chunk_rows[c], head_cols[h], f_lane[h], c * N_HEADS + h
        u = ub_ref[rows, sl]
        j_row = jt_ref[c][lf:lf + 1, :]
        a = jnp.where(mask, j_row, -jnp.inf)
        p_ref[idx] = jnp.exp(a - u[:, :CHUNK])
        wi_ref[idx] = jnp.exp(rowb_ref[0][c:c + 1, sl] - u)
        w_k = jnp.exp(j_row - rowb_ref[1][c:c + 1, sl][:, :CHUNK])
        kwt_ref[idx] = (qkc_ref[rows, key_cols[h]].astype(F32).T * w_k).astype(BF16)

    def scores(c, h):
        rows, idx = chunk_rows[c], c * N_HEADS + h
        s = lax.dot_general(qkc_ref[rows, head_cols[h]], qkc_ref[rows, key_cols[h]],
                            (((1,), (1,)), ((), ())), preferred_element_type=F32)
        s_ref[idx] = (s * p_ref[idx]).astype(BF16)

    def local_output(c, h):
        rows, idx = chunk_rows[c], c * N_HEADS + h
        v_aug = jnp.concatenate([v_ref[rows, head_cols[h]], ones_col], axis=1)
        r_ref[rows, 2 * h * HEAD_DIM:2 * (h + 1) * HEAD_DIM] = jnp.dot(
            s_ref[idx], v_aug, preferred_element_type=F32)
        kv_ref[idx] = jnp.dot(kwt_ref[idx], v_aug, preferred_element_type=F32)

    def carry(c, h):
        rows, sl, lf, idx = chunk_rows[c], head_cols[h], f_lane[h], c * N_HEADS + h
        decay = jnp.exp(rowb_ref[0][c:c + 1, sl] - rowb_ref[1][c:c + 1, sl])
        c_aug = c_ref[h]
        wi = wi_ref[idx]
        r = jnp.concatenate([wi, wi], axis=1) * jnp.dot(
            qkc_ref[rows, sl], c_aug.astype(BF16),
            preferred_element_type=F32) + r_ref[rows, 2 * h * HEAD_DIM:2 * (h + 1) * HEAD_DIM]
        den = jnp.maximum(jnp.abs(r[:, HEAD_DIM:HEAD_DIM + 1]), en_ref[rows, lf:lf + 1])
        acc_ref[rows, sl] = r[:, :HEAD_DIM] * (1.0 / den)
        c_ref[h] = c_aug * jnp.concatenate([decay, decay], axis=1) + kv_ref[idx]

    for c in range(n_chunks):
        for h in range(N_HEADS):
            decay_weights(c, h)
    for c in range(n_chunks):
        for h in range(N_HEADS):
            scores(c, h)
    for c in range(n_chunks):
        for h in range(N_HEADS):
            local_output(c, h)
    for c in scan_order:
        for h in range(N_HEADS):
            carry(c, h)

    if final:
        for sl in head_cols:
            out_ref[:, sl] = _head_norm_gate(acc_ref[:, sl] + hf_ref[:, sl], gn_ref[:, sl],
                                             _sigmoid(og_ref[:, sl].astype(F32))
                                             ).astype(out_ref.dtype)

    @pl.when(step == pl.num_programs(1) - 1)
    def _():
        cfin_ref[...] = c_ref[...]
        mfin_ref[...] = m_ref[...]


def _mlstm_scan(p3, ps3, conv_w, gate_b, c0, m0, gn, fwd, *, reverse, t, period):
    bsz, seq_len, _ = p3.shape
    n_blocks = seq_len // t
    n_chunks = t // CHUNK
    final = fwd is not None
    spec = functools.partial(_scan_block_spec, n_blocks=n_blocks, reverse=reverse)
    const2 = lambda shape: pl.BlockSpec(shape, lambda b, i: (0, 0))
    c_spec = pl.BlockSpec((None, N_HEADS, HEAD_DIM, 2 * HEAD_DIM), lambda b, i: (b, 0, 0, 0))
    m_spec = pl.BlockSpec((None, 8, LANES), lambda b, i: (b, 0, 0))
    out_specs = [spec(t, D_GROUP, 0), c_spec, m_spec]
    out_shape = [
        jax.ShapeDtypeStruct((bsz, seq_len, D_GROUP), BF16 if final else F32),
        jax.ShapeDtypeStruct((bsz, N_HEADS, HEAD_DIM, 2 * HEAD_DIM), F32),
        jax.ShapeDtypeStruct((bsz, 8, LANES), F32),
    ]
    lane0 = GATE0 + (2 * N_HEADS if reverse else 0)
    lane = np.arange(2 * SMALL_COLS)[:, None] % SMALL_COLS
    head = np.arange(D_GROUP)[None, :] // HEAD_DIM
    spread = jnp.asarray(lane == lane0 + N_HEADS + head, BF16)
    if final:
        h_fwd, qkc = fwd
        in_specs = [
            spec(t, 2 * D_GROUP, 0), spec(t, D_GROUP, C_MV), spec(t, SMALL_COLS, 0),
            const2((1, SMALL_COLS)), const2((2 * SMALL_COLS, D_GROUP)), c_spec, m_spec,
            spec(t, D_GROUP, 0), spec(t, D_GROUP, C_MO), const2((1, D_GROUP)),
        ]
        args = [qkc, p3, ps3, gate_b, spread, c0, m0, h_fwd, p3, gn]
    else:
        in_specs = [
            spec(t, 2 * D_GROUP, C_MQK // 2), spec(t, D_GROUP, C_MV), spec(t, SMALL_COLS, 0),
            const2((3, 2 * D_GROUP)), const2((1, SMALL_COLS)),
            const2((2 * SMALL_COLS, D_GROUP)), c_spec, m_spec,
        ]
        args = [p3, p3, ps3, conv_w, gate_b, spread, c0, m0]
        out_specs.append(spec(t, 2 * D_GROUP, 0))
        out_shape.append(jax.ShapeDtypeStruct((bsz, seq_len, 2 * D_GROUP), BF16))
    units = n_chunks * N_HEADS
    return pl.pallas_call(
        functools.partial(_mlstm_kernel, reverse=reverse, final=final, n_chunks=n_chunks,
                          period=period, lane0=lane0),
        grid=(bsz, n_blocks),
        in_specs=in_specs,
        out_specs=out_specs,
        out_shape=out_shape,
        scratch_shapes=[
            pltpu.VMEM((N_HEADS, HEAD_DIM, 2 * HEAD_DIM), F32),
            pltpu.VMEM((8, LANES), F32),
            pltpu.VMEM((t, SMALL_COLS), F32),
            pltpu.VMEM((t, SMALL_COLS), F32),
            pltpu.VMEM((n_chunks, SMALL_COLS, CHUNK), F32),
            pltpu.VMEM((2, ROW_PAD, D_GROUP), F32),
            pltpu.VMEM((t, D_GROUP), F32),
            pltpu.VMEM((t, SMALL_COLS), F32),
            pltpu.VMEM((units, CHUNK, CHUNK), F32),
            pltpu.VMEM((units, CHUNK, CHUNK), BF16),
            pltpu.VMEM((units, CHUNK, LANES), F32),
            pltpu.VMEM((units, HEAD_DIM, CHUNK), BF16),
            pltpu.VMEM((units, HEAD_DIM, 2 * HEAD_DIM), F32),
            pltpu.VMEM((t, 2 * D_GROUP), F32),
        ] + ([pltpu.VMEM((t, D_GROUP), F32)] if final else []),
        compiler_params=_params(("parallel", "arbitrary")),
        name="mlstm_bwd" if reverse else "mlstm_fwd",
    )(*args)


def _outproj_kernel(y0_ref, y1_ref, y2_ref, y3_ref, w_ref, x_ref, g_ref, o_ref):
    acc = jnp.dot(y0_ref[...], w_ref[0:D_GROUP, :], preferred_element_type=F32)
    for n, y_ref in enumerate((y1_ref, y2_ref, y3_ref), start=1):
        acc = acc + jnp.dot(y_ref[...], w_ref[n * D_GROUP:(n + 1) * D_GROUP, :],
                            preferred_element_type=F32)
    o_ref[...] = x_ref[...] + g_ref[...] * acc


def _out_projection(ys, w_out, layer, x2, mod, seq_len, tm):
    m, d = x2.shape
    y_spec = pl.BlockSpec((tm, D_GROUP), lambda i: (i, 0))
    return pl.pallas_call(
        _outproj_kernel,
        grid=(m // tm,),
        in_specs=[y_spec] * N_MIXERS + [
            pl.BlockSpec((None, d, d), lambda i: (layer, 0, 0), pipeline_mode=pl.Buffered(1)),
            pl.BlockSpec((tm, d), lambda i: (i, 0)),
            _row_spec(2, seq_len, tm, d),
        ],
        out_specs=pl.BlockSpec((tm, d), lambda i: (i, 0)),
        out_shape=jax.ShapeDtypeStruct((m, d), F32),
        compiler_params=_params(("parallel",)),
        name="out_projection",
    )(*ys, w_out, x2, mod)


def _ffn_kernel(*refs, period, final):
    if final:
        (x_ref, g_ref, sh_ref, sc_ref, gate_ref, wa_ref, wv_ref, cw_ref, wd_ref, fn_ref,
         o_ref, hn_ref) = refs
    else:
        (x_ref, g_ref, sh_ref, sc_ref, gate_ref, wa_ref, wv_ref, cw_ref, wd_ref,
         o_ref, hn_ref) = refs
    j = pl.program_id(1)
    last = pl.num_programs(1) - 1

    @pl.when(j == 0)
    def _():
        hn_ref[...] = _modulated_norm(x_ref[...], g_ref[...], sc_ref[...], sh_ref[...]).astype(BF16)

    def tile_contribution():
        hn = hn_ref[...]
        a = _dwconv3(jnp.dot(hn, wa_ref[...], preferred_element_type=F32), cw_ref, period)
        v = jnp.dot(hn, wv_ref[...], preferred_element_type=F32)
        hid = (a * _sigmoid(a) * v).astype(BF16)
        return jnp.dot(hid, wd_ref[...], preferred_element_type=F32)

    @pl.when(j == 0)
    def _():
        o_ref[...] = tile_contribution()

    @pl.when(jnp.logical_and(j > 0, j < last))
    def _():
        o_ref[...] += tile_contribution()

    @pl.when(j == last)
    def _():
        y = x_ref[...] + gate_ref[...] * (o_ref[...] + tile_contribution())
        if final:
            y = y * lax.rsqrt(jnp.mean(y * y, axis=-1, keepdims=True) + EPS) * fn_ref[...]
        o_ref[...] = y


def _conv_ffn(x2, gain, mod, w_up, w_conv, w_down, layer, final_gain, seq_len, period, tm,
              tf=512):
    m, d = x2.shape
    d_ff = w_down.shape[1]
    n_f = d_ff // tf
    final = final_gain is not None
    in_specs = [
        pl.BlockSpec((tm, d), lambda i, j: (i, 0)),
        pl.BlockSpec((1, d), lambda i, j: (0, 0)),
        _row_spec(3, seq_len, tm, d), _row_spec(4, seq_len, tm, d), _row_spec(5, seq_len, tm, d),
        pl.BlockSpec((None, d, tf), lambda i, j: (layer, 0, j)),
        pl.BlockSpec((None, d, tf), lambda i, j: (layer, 0, j + n_f)),
        pl.BlockSpec((None, 3, tf), lambda i, j: (layer, 0, j)),
        pl.BlockSpec((None, tf, d), lambda i, j: (layer, j, 0)),
    ]
    args = [x2, gain.reshape(1, d), mod, mod, mod, w_up, w_up, w_conv, w_down]
    if final:
        in_specs.append(pl.BlockSpec((1, d), lambda i, j: (0, 0)))
        args.append(final_gain.reshape(1, d))
    return pl.pallas_call(
        functools.partial(_ffn_kernel, period=period, final=final),
        grid=(m // tm, n_f),
        in_specs=in_specs,
        out_specs=pl.BlockSpec((tm, d), lambda i, j: (i, 0)),
        out_shape=jax.ShapeDtypeStruct((m, d), F32),
        scratch_shapes=[pltpu.VMEM((tm, d), BF16)],
        compiler_params=pltpu.CompilerParams(dimension_semantics=("parallel", "arbitrary"),
                                             vmem_limit_bytes=FFN_VMEM_LIMIT),
        name="conv_ffn",
    )(*args)


def _relayout_w_in(w_in):
    depth, d, _ = w_in.shape
    gla = 4 * D_GROUP + 2 * GLA_LOWRANK
    ml = 4 * D_GROUP + 4 * N_HEADS
    wide = jnp.concatenate([w_in[..., :4 * D_GROUP], w_in[..., gla:gla + 4 * D_GROUP],
                            w_in[..., gla + ml:]], axis=-1).astype(BF16)
    small = jnp.concatenate([
        w_in[..., 4 * D_GROUP:gla], w_in[..., gla + 4 * D_GROUP:gla + ml],
        jnp.zeros((depth, d, SMALL_COLS - 2 * GLA_LOWRANK - 4 * N_HEADS), w_in.dtype)],
        axis=-1).astype(BF16)
    return wide, small


def _scan_mixers(p2, ps2, bsz, seq_len, prm, states, *, t_scan, period):
    p3 = p2.reshape(bsz, seq_len, -1)
    ps3 = ps2.reshape(bsz, seq_len, SMALL_COLS)
    (gs_f, gs_b), (mc_f, mm_f, mc_b, mm_b) = states
    o_f, gs_f = _gla_scan(p3, ps3, prm["wa_f"], prm["ba_f"], gs_f, None, None,
                          reverse=False, t=t_scan)
    y_gla, gs_b = _gla_scan(p3, ps3, prm["wa_b"], prm["ba_b"], gs_b, prm["gla_norm"], o_f,
                            reverse=True, t=t_scan)
    h_f, mc_f, mm_f, qkc = _mlstm_scan(p3, ps3, prm["ml_conv"], prm["ml_gate_b"], mc_f, mm_f,
                                       None, None, reverse=False, t=t_scan, period=period)
    y_ml, mc_b, mm_b = _mlstm_scan(p3, ps3, None, prm["ml_gate_b"], mc_b, mm_b,
                                   prm["ml_norm"], (h_f, qkc), reverse=True, t=t_scan,
                                   period=period)
    new_states = ((gs_f, gs_b), (mc_f, mm_f, mc_b, mm_b))
    m = bsz * seq_len
    return (y_gla.reshape(m, D_GROUP), y_ml.reshape(m, D_GROUP)), new_states


def kernel(x, c, ctx, c_ctx, w_mod, b_mod, norm1, norm2, w_in, w_out, gla_wa2, gla_ba, gla_norm,
           mlstm_conv, mlstm_gate_b, mlstm_norm, sc_conv, cm_ws, cm_bs, cm_norm,
           ffn_up, ffn_conv, ffn_down, final_norm):
    bsz, seq_len, d = x.shape
    ctx_len = ctx.shape[1]
    depth = w_mod.shape[0]
    assert seq_len % 512 == 0 and ctx_len % CM_CHUNK == 0 and bsz + 1 <= 8

    cond = jnp.zeros((8, d), F32).at[:bsz].set(c).at[bsz].set(c_ctx)
    mod = _modulation(cond, w_mod, b_mod).reshape(depth, 8, 6, d).transpose(0, 2, 1, 3)

    xl = x.reshape(bsz * seq_len, d)
    xc = ctx.reshape(bsz * ctx_len, d)
    tm_l, tm_c = 512, bsz * ctx_len
    tm_in = 512
    tm_ffn = 1024

    zero_states = (
        (jnp.zeros((bsz, N_HEADS, HEAD_DIM, HEAD_DIM), F32),) * 2,
        (jnp.zeros((bsz, N_HEADS, HEAD_DIM, 2 * HEAD_DIM), F32), jnp.zeros((bsz, 8, LANES), F32)) * 2,
    )

    w_wide_all, w_small_all = _relayout_w_in(w_in)
    w_out_b, w_up_b, w_down_b = w_out.astype(BF16), ffn_up.astype(BF16), ffn_down.astype(BF16)
    k_scale = jnp.concatenate([jnp.ones((1, D_GROUP), F32),
                               jnp.full((1, D_GROUP), HEAD_DIM ** -0.5, F32)], axis=1)
    for l in range(depth):
        need_ctx = l < depth - 1
        mod_l = mod[l][:, :bsz, None, :]
        mod_c = jnp.broadcast_to(mod[l][:, bsz:bsz + 1, None, :], (6, bsz, 1, d))
        wa = jnp.zeros((2, SMALL_COLS, D_GROUP), F32)
        wa = wa.at[0, Z_FWD:Z_FWD + GLA_LOWRANK].set(gla_wa2[l, 0])
        wa = wa.at[1, Z_BWD:Z_BWD + GLA_LOWRANK].set(gla_wa2[l, 1])
        wa_hi = wa.astype(BF16)
        wa_lo = (wa - wa_hi.astype(F32)).astype(BF16)
        wa = jnp.concatenate([wa_hi, wa_hi, wa_lo], axis=1)
        gate_b = jnp.zeros((1, SMALL_COLS), F32).at[0, GATE0:GATE0 + 4 * N_HEADS].set(
            mlstm_gate_b[l].reshape(-1))
        prm = dict(
            wa_f=wa[0], wa_b=wa[1], ba_f=gla_ba[l, 0:1], ba_b=gla_ba[l, 1:2],
            gla_norm=gla_norm[l].reshape(1, D_GROUP),
            ml_conv=mlstm_conv[l] * k_scale, ml_gate_b=gate_b,
            ml_norm=mlstm_norm[l].reshape(1, D_GROUP),
            sc_conv=sc_conv[l], cm_ws=cm_ws[l].astype(BF16),
            cm_bs=jnp.repeat(cm_bs[l].T, HEAD_DIM, axis=1),
            cm_norm=cm_norm[l].reshape(1, D_GROUP),
        )

        pc, psc, *yc_local = _in_projection(xc, norm1[l], mod_c, w_wide_all, w_small_all, l, prm,
                                            ctx_len, tm_c, ctx_len)
        yc_scan, ctx_states = _scan_mixers(pc, psc, bsz, ctx_len, prm, zero_states,
                                           t_scan=ctx_len, period=ctx_len)
        p_l, ps_l, *yl_local = _in_projection(xl, norm1[l], mod_l, w_wide_all, w_small_all, l, prm,
                                              seq_len, tm_in, GRID_W)
        yl_scan, _ = _scan_mixers(p_l, ps_l, bsz, seq_len, prm, ctx_states,
                                  t_scan=512, period=GRID_W)
        yl, yc = yl_scan + tuple(yl_local), yc_scan + tuple(yc_local)

        xl = _out_projection(yl, w_out_b, l, xl, mod_l, seq_len, tm_l)
        xl = _conv_ffn(xl, norm2[l], mod_l, w_up_b, ffn_conv, w_down_b, l,
                       final_norm if l == depth - 1 else None, seq_len, GRID_W, tm_ffn)
        if need_ctx:
            xc = _out_projection(yc, w_out_b, l, xc, mod_c, ctx_len, tm_c)
            xc = _conv_ffn(xc, norm2[l], mod_c, w_up_b, ffn_conv, w_down_b, l, None,
                           ctx_len, ctx_len, tm_c)
    return xl.reshape(bsz, seq_len, d)
```

```python
import functools

import numpy as np
import jax
import jax.numpy as jnp
from jax import lax
from jax.experimental import pallas as pl
from jax.experimental.pallas import tpu as pltpu

F32 = jnp.float32
BF16 = jnp.bfloat16

GRID_W = 64
N_MIXERS = 4
HEAD_DIM = 128
N_HEADS = 4
D_GROUP = N_HEADS * HEAD_DIM
GLA_LOWRANK = 16
GLA_GATE_NORM = 16.0
CHUNK = 64
CM_CHUNK = 128
EPS = 1e-6
LANES = 128
SMALL_COLS = LANES
Z_FWD, Z_BWD, GATE0 = 0, GLA_LOWRANK, 2 * GLA_LOWRANK
VMEM_LIMIT = 56 * 1024 * 1024
INPROJ_VMEM_LIMIT = 60 * 1024 * 1024
FFN_VMEM_LIMIT = 62 * 1024 * 1024

(C_GQ, C_GK, C_GV, C_GG, C_MQK, C_MV, C_MO, C_SB, C_SC, C_SH, C_CU, C_CV) = (
    0, 1, 2, 3, 4, 6, 7, 8, 9, 10, 11, 12)
N_WIDE_BLOCKS = 13
N_SCAN_BLOCKS = 8
ROW_PAD = 16


def _params(sem):
    return pltpu.CompilerParams(dimension_semantics=sem, vmem_limit_bytes=VMEM_LIMIT)


def _sigmoid(x):
    return 1.0 / (1.0 + jnp.exp(-x))


def _log_sigmoid(x):
    return jnp.minimum(x, 0.0) - jnp.log(1.0 + jnp.exp(-jnp.abs(x)))


def _gelu_tanh(x):
    return x * (0.5 * (1.0 + jnp.tanh(0.7978845608028654 * (x + 0.044715 * (x * x * x)))))


def _shift_rows(x, period):
    n = x.shape[0]
    row = lax.broadcasted_iota(jnp.int32, (n, 1), 0) % period
    prev = jnp.where(row == 0, 0.0, pltpu.roll(x, 1, axis=0))
    nxt = jnp.where(row == period - 1, 0.0, pltpu.roll(x, n - 1, axis=0))
    return prev, nxt


def _dwconv3(x, w_ref, period):
    prev, nxt = _shift_rows(x, period)
    return w_ref[0:1, :] * prev + w_ref[1:2, :] * x + w_ref[2:3, :] * nxt


def _mod_kernel(c_ref, w_ref, b_ref, o_ref):
    s = c_ref[...]
    s = (s * _sigmoid(s)).astype(BF16)
    o_ref[...] = jnp.dot(s, w_ref[...].astype(BF16), preferred_element_type=F32) + b_ref[...]


def _modulation(cond, w_mod, b_mod, tn=1024):
    depth, d, n = w_mod.shape
    rows = cond.shape[0]
    return pl.pallas_call(
        _mod_kernel,
        grid=(depth, n // tn),
        in_specs=[
            pl.BlockSpec((rows, d), lambda l, j: (0, 0)),
            pl.BlockSpec((None, d, tn), lambda l, j: (l, 0, j)),
            pl.BlockSpec((None, 1, tn), lambda l, j: (l, 0, j)),
        ],
        out_specs=pl.BlockSpec((None, rows, tn), lambda l, j: (l, 0, j)),
        out_shape=jax.ShapeDtypeStruct((depth, rows, n), F32),
        compiler_params=_params(("parallel", "parallel")),
        name="modulation",
    )(cond, w_mod, b_mod.reshape(depth, 1, n))


def _modulated_norm(x, g, sc, sh):
    r = lax.rsqrt(jnp.mean(x * x, axis=-1, keepdims=True) + EPS)
    return (x * r * g) * (1.0 + sc) + sh


def _inproj_kernel(x_ref, g_ref, sh_ref, sc_ref, w_ref, ws_ref, scw_ref, cws_ref, cbs_ref,
                   cgv_ref, mlw_ref, p_ref, ps_ref, ysc_ref, ycm_ref, *, period):
    hn = _modulated_norm(x_ref[...], g_ref[...], sc_ref[...], sh_ref[...]).astype(BF16)

    def proj(block):
        cols = slice(block * D_GROUP, (block + 1) * D_GROUP)
        return jnp.dot(hn, w_ref[:, cols], preferred_element_type=F32)

    ysc_ref[...] = (proj(C_SB) * _dwconv3(proj(C_SC) * proj(C_SH), scw_ref, period)
                    ).astype(ysc_ref.dtype)

    v = _gelu_tanh(proj(C_CV))
    vn = (v * lax.rsqrt(jnp.mean(v * v, axis=-1, keepdims=True) + EPS) * cgv_ref[...]).astype(BF16)
    u = _gelu_tanh(proj(C_CU))
    for n in range(v.shape[0] // CM_CHUNK):
        rows = slice(n * CM_CHUNK, (n + 1) * CM_CHUNK)
        for g in range(N_HEADS):
            sl = slice(g * HEAD_DIM, (g + 1) * HEAD_DIM)
            sv = jnp.dot(cws_ref[g], vn[rows, sl], preferred_element_type=F32) + cbs_ref[:, sl]
            ycm_ref[rows, sl] = (u[rows, sl] * sv).astype(ycm_ref.dtype)

    for n, block in enumerate((C_MQK, C_MQK + 1)):
        cols = slice(block * D_GROUP, (block + 1) * D_GROUP)
        p_ref[:, cols] = _dwconv3(proj(block), mlw_ref.at[:, n * D_GROUP:(n + 1) * D_GROUP],
                                  period).astype(p_ref.dtype)

    ps_ref[...] = jnp.dot(hn, ws_ref[...], preferred_element_type=F32)
    for block in range(N_SCAN_BLOCKS):
        if block not in (C_MQK, C_MQK + 1):
            p_ref[:, block * D_GROUP:(block + 1) * D_GROUP] = proj(block).astype(p_ref.dtype)


def _row_spec(piece, seq_len, tm, d):
    return pl.BlockSpec((None, None, 1, d), lambda i, *_: (piece, (i * tm) // seq_len, 0, 0))


def _in_projection(x2, gain, mod, w_wide, w_small, layer, prm, seq_len, tm, period):
    m, d = x2.shape
    n = w_wide.shape[2]
    n_scan = N_SCAN_BLOCKS * D_GROUP
    resident = lambda shape: pl.BlockSpec((None,) + shape, lambda i: (layer, 0, 0),
                                          pipeline_mode=pl.Buffered(1))
    const = lambda shape: pl.BlockSpec(shape, lambda i: (0,) * len(shape))
    tile = lambda cols: pl.BlockSpec((tm, cols), lambda i: (i, 0))
    return pl.pallas_call(
        functools.partial(_inproj_kernel, period=period),
        grid=(m // tm,),
        in_specs=[
            tile(d),
            const((1, d)),
            _row_spec(0, seq_len, tm, d),
            _row_spec(1, seq_len, tm, d),
            resident((d, n)),
            resident((d, SMALL_COLS)),
            const((3, D_GROUP)), const((N_HEADS, CM_CHUNK, CM_CHUNK)),
            const((CM_CHUNK, D_GROUP)), const((1, D_GROUP)), const((3, 2 * D_GROUP)),
        ],
        out_specs=[tile(n_scan), tile(SMALL_COLS), tile(D_GROUP), tile(D_GROUP)],
        out_shape=[
            jax.ShapeDtypeStruct((m, n_scan), BF16),
            jax.ShapeDtypeStruct((m, SMALL_COLS), F32),
            jax.ShapeDtypeStruct((m, D_GROUP), BF16),
            jax.ShapeDtypeStruct((m, D_GROUP), BF16),
        ],
        compiler_params=pltpu.CompilerParams(dimension_semantics=("parallel",),
                                             vmem_limit_bytes=INPROJ_VMEM_LIMIT),
        name="in_projection",
    )(x2, gain.reshape(1, d), mod, mod, w_wide, w_small,
      prm["sc_conv"], prm["cm_ws"], prm["cm_bs"], prm["cm_norm"], prm["ml_conv"])


def _tri(reverse):
    r = lax.broadcasted_iota(jnp.int32, (CHUNK, CHUNK), 0)
    c = lax.broadcasted_iota(jnp.int32, (CHUNK, CHUNK), 1)
    return (c >= r) if reverse else (c <= r)


def _tri2(reverse):
    r = lax.broadcasted_iota(jnp.int32, (CHUNK, 2 * CHUNK), 0)
    c = lax.broadcasted_iota(jnp.int32, (CHUNK, 2 * CHUNK), 1) % CHUNK
    return ((c >= r) if reverse else (c <= r)).astype(BF16)


def _split2(x):
    hi = x.astype(BF16)
    return hi, (x - hi.astype(F32)).astype(BF16)


def _chunk_cumsum(tri2, x):
    hi, lo = _split2(x)
    return jnp.dot(tri2, jnp.concatenate([hi, lo], axis=0), preferred_element_type=F32)


def _emit_skewed(order, chunk_stages, head_stages):
    n_stage = len(chunk_stages) + len(head_stages)
    for slot in range(len(order) + n_stage - 1):
        for k, stage in enumerate(chunk_stages):
            if 0 <= slot - k < len(order):
                stage(order[slot - k])
        for h in range(N_HEADS):
            for k, stage in enumerate(head_stages, start=len(chunk_stages)):
                if 0 <= slot - k < len(order):
                    stage(order[slot - k], h)


def _cummax_rows(x, reverse):
    n = x.shape[0]
    row = lax.broadcasted_iota(jnp.int32, (n, 1), 0)
    k = 1
    while k < n:
        if reverse:
            shifted = jnp.where(row >= n - k, -jnp.inf, pltpu.roll(x, n - k, axis=0))
        else:
            shifted = jnp.where(row < k, -jnp.inf, pltpu.roll(x, k, axis=0))
        x = jnp.maximum(x, shifted)
        k *= 2
    return x


def _lane_spread(x, bm_ref):
    hi, lo = _split2(x)
    return jnp.dot(jnp.concatenate([hi, lo], axis=1), bm_ref[...], preferred_element_type=F32)


def _head_norm_gate(o, gain, gate):
    r = lax.rsqrt(jnp.mean(o * o, axis=-1, keepdims=True) + EPS)
    return o * r * gain * gate


def _gla_kernel(*refs, reverse, final, n_chunks):
    if final:
        (q_ref, k_ref, v_ref, z_ref, wa_ref, ba_ref, s0_ref, of_ref, g_ref, gn_ref,
         out_ref, sfin_ref, st_ref, qin_ref, qrel_ref, krel_ref, dec_ref, att_ref, kv_ref,
         acc_ref) = refs
    else:
        (q_ref, k_ref, v_ref, z_ref, wa_ref, ba_ref, s0_ref,
         out_ref, sfin_ref, st_ref, qin_ref, qrel_ref, krel_ref, dec_ref, att_ref,
         kv_ref) = refs
        acc_ref = out_ref
    step = pl.program_id(1)

    @pl.when(step == 0)
    def _():
        st_ref[...] = s0_ref[...]

    mask = _tri(reverse)
    tri2 = _tri2(reverse)
    last = 0 if reverse else CHUNK - 1
    scale = HEAD_DIM ** -0.5
    chunk_rows = [slice(c * CHUNK, (c + 1) * CHUNK) for c in range(n_chunks)]
    head_cols = [slice(h * HEAD_DIM, (h + 1) * HEAD_DIM) for h in range(N_HEADS)]

    z_hi, z_lo = _split2(z_ref[...])
    pre = jnp.dot(jnp.concatenate([z_hi, z_lo, z_hi], axis=1), wa_ref[...],
                  preferred_element_type=F32) + ba_ref[...]
    log_a = _log_sigmoid(pre) * (1.0 / GLA_GATE_NORM)

    def decayed_qk(c):
        rows = chunk_rows[c]
        b_cum = _chunk_cumsum(tri2, log_a[rows, :])
        b_last = b_cum[last:last + 1, :]
        q = q_ref[rows, :].astype(F32) * scale
        qin_ref[rows, :] = (q * jnp.exp(b_cum)).astype(BF16)
        qrel_ref[rows, :] = (q * jnp.exp(b_cum - b_last)).astype(BF16)
        krel_ref[rows, :] = (k_ref[rows, :].astype(F32) * jnp.exp(b_last - b_cum)).astype(BF16)
        dec_ref[c] = jnp.exp(b_last)

    def scores(c, h):
        rows, sl = chunk_rows[c], head_cols[h]
        att = lax.dot_general(qrel_ref[rows, sl], krel_ref[rows, sl],
                              (((1,), (1,)), ((), ())), preferred_element_type=F32)
        att_ref[c * N_HEADS + h] = jnp.where(mask, att, 0.0).astype(BF16)

    def local_output(c, h):
        rows, sl = chunk_rows[c], head_cols[h]
        v = v_ref[rows, sl]
        acc_ref[rows, sl] = jnp.dot(att_ref[c * N_HEADS + h], v, preferred_element_type=F32)
        kv_ref[c * N_HEADS + h] = lax.dot_general(
            v, krel_ref[rows, sl], (((0,), (0,)), ((), ())), preferred_element_type=F32)

    def carry(c, h):
        rows, sl = chunk_rows[c], head_cols[h]
        s_t = st_ref[h]
        o = acc_ref[rows, sl] + lax.dot_general(qin_ref[rows, sl], s_t.astype(BF16),
                                                (((1,), (1,)), ((), ())),
                                                preferred_element_type=F32)
        st_ref[h] = s_t * dec_ref[c][:, sl] + kv_ref[c * N_HEADS + h]
        if final:
            g = g_ref[rows, sl].astype(F32)
            o = _head_norm_gate(o + of_ref[rows, sl], gn_ref[:, sl], g * _sigmoid(g))
        out_ref[rows, sl] = o.astype(out_ref.dtype)

    _emit_skewed(list(reversed(range(n_chunks))) if reverse else list(range(n_chunks)),
                 [decayed_qk], [scores, local_output, carry])

    @pl.when(step == pl.num_programs(1) - 1)
    def _():
        sfin_ref[...] = st_ref[...]


def _scan_block_spec(t, cols, col_block, n_blocks, reverse):
    if reverse:
        return pl.BlockSpec((None, t, cols), lambda b, i: (b, n_blocks - 1 - i, col_block))
    return pl.BlockSpec((None, t, cols), lambda b, i: (b, i, col_block))


def _gla_scan(p3, ps3, wa_pad, ba, s0, gn, o_fwd, *, reverse, t):
    bsz, seq_len, _ = p3.shape
    n_blocks = seq_len // t
    n_chunks = t // CHUNK
    final = o_fwd is not None
    spec = functools.partial(_scan_block_spec, n_blocks=n_blocks, reverse=reverse)
    const2 = lambda shape: pl.BlockSpec(shape, lambda b, i: (0, 0))
    state_spec = pl.BlockSpec((None, N_HEADS, HEAD_DIM, HEAD_DIM), lambda b, i: (b, 0, 0, 0))
    in_specs = [
        spec(t, D_GROUP, C_GQ), spec(t, D_GROUP, C_GK), spec(t, D_GROUP, C_GV),
        spec(t, SMALL_COLS, 0),
        const2((3 * SMALL_COLS, D_GROUP)), const2((1, D_GROUP)),
        state_spec,
    ]
    args = [p3, p3, p3, ps3, wa_pad, ba, s0]
    if final:
        in_specs += [spec(t, D_GROUP, 0), spec(t, D_GROUP, C_GG), const2((1, D_GROUP))]
        args += [o_fwd, p3, gn]
    return pl.pallas_call(
        functools.partial(_gla_kernel, reverse=reverse, final=final, n_chunks=n_chunks),
        grid=(bsz, n_blocks),
        in_specs=in_specs,
        out_specs=[spec(t, D_GROUP, 0), state_spec],
        out_shape=[
            jax.ShapeDtypeStruct((bsz, seq_len, D_GROUP), BF16 if final else F32),
            jax.ShapeDtypeStruct((bsz, N_HEADS, HEAD_DIM, HEAD_DIM), F32),
        ],
        scratch_shapes=[
            pltpu.VMEM((N_HEADS, HEAD_DIM, HEAD_DIM), F32),
            pltpu.VMEM((t, D_GROUP), BF16), pltpu.VMEM((t, D_GROUP), BF16),
            pltpu.VMEM((t, D_GROUP), BF16),
            pltpu.VMEM((n_chunks, 1, D_GROUP), F32),
            pltpu.VMEM((n_chunks * N_HEADS, CHUNK, CHUNK), BF16),
            pltpu.VMEM((n_chunks * N_HEADS, HEAD_DIM, HEAD_DIM), F32),
        ] + ([pltpu.VMEM((t, D_GROUP), F32)] if final else []),
        compiler_params=_params(("parallel", "arbitrary")),
        name="gla_bwd" if reverse else "gla_fwd",
    )(*args)


def _mlstm_kernel(*refs, reverse, final, n_chunks, lane0):
    if final:
        (qkc_ref, v_ref, gt_ref, gb_ref, bm_ref, c0_ref, m0_ref, hf_ref, og_ref, gn_ref,
         out_ref, cfin_ref, mfin_ref, c_ref, m_ref, fcol_ref, jcol_ref, jt_ref,
         rowb_ref, ub_ref, en_ref, p_ref, s_ref, wi_ref, kwt_ref, kv_ref, r_ref) = refs
    else:
        (qkc_ref, v_ref, gt_ref, gb_ref, bm_ref, c0_ref, m0_ref,
         out_ref, cfin_ref, mfin_ref, c_ref, m_ref, fcol_ref, jcol_ref, jt_ref,
         rowb_ref, ub_ref, en_ref, p_ref, s_ref, wi_ref, kwt_ref, kv_ref, r_ref) = refs
    step = pl.program_id(1)

    @pl.when(step == 0)
    def _():
        c_ref[...] = c0_ref[...]
        m_ref[...] = m0_ref[...]

    mask = _tri(reverse)
    tri2 = _tri2(reverse)
    last = 0 if reverse else CHUNK - 1
    ones_col = (lax.broadcasted_iota(jnp.int32, (CHUNK, HEAD_DIM), 1) == 0).astype(BF16)
    chunk_rows = [slice(c * CHUNK, (c + 1) * CHUNK) for c in range(n_chunks)]
    head_cols = [slice(h * HEAD_DIM, (h + 1) * HEAD_DIM) for h in range(N_HEADS)]
    key_cols = [slice(D_GROUP + h * HEAD_DIM, D_GROUP + (h + 1) * HEAD_DIM) for h in range(N_HEADS)]
    f_lane = [lane0 + N_HEADS + h for h in range(N_HEADS)]
    scan_order = list(reversed(range(n_chunks))) if reverse else list(range(n_chunks))

    gates = gt_ref[...] + gb_ref[...]
    log_f = _log_sigmoid(gates)
    i_al = pltpu.roll(gates, N_HEADS, axis=1)
    j_max, f_tot = [], []
    for c, rows in enumerate(chunk_rows):
        f_col = _chunk_cumsum(tri2, log_f[rows, :])
        j_col = i_al[rows, :] - f_col
        fcol_ref[rows, :] = f_col
        jcol_ref[rows, :] = j_col
        jt_ref[c] = j_col.T
        j_max.append(jnp.max(j_col, axis=0, keepdims=True))
        f_tot.append(f_col[last:last + 1, :])

    m_run = m_ref[0:1, :]
    m_prev, u_last = [None] * n_chunks, [None] * n_chunks
    for c in scan_order:
        m_prev[c] = m_run
        u_last[c] = jnp.maximum(m_run, j_max[c])
        m_run = f_tot[c] + u_last[c]
    m_ref[0:1, :] = m_run
    pad_rows = [jnp.zeros((ROW_PAD - n_chunks, LANES), F32)]
    rowb_ref[0] = _lane_spread(jnp.concatenate(m_prev + pad_rows, axis=0), bm_ref)
    rowb_ref[1] = _lane_spread(jnp.concatenate(u_last + pad_rows, axis=0), bm_ref)

    for c, rows in enumerate(chunk_rows):
        u_col = jnp.maximum(_cummax_rows(jcol_ref[rows, :], reverse), m_prev[c])
        ub_ref[rows, :] = _lane_spread(u_col, bm_ref)
        en_ref[rows, :] = jnp.exp(-(fcol_ref[rows, :] + u_col))

    def decay_weights(c, h):
        rows, sl, lf, idx = chunk_rows[c], head_cols[h], f_lane[h], c * N_HEADS + h
        u = ub_ref[rows, sl]
        j_row = jt_ref[c][lf:lf + 1, :]
        a = jnp.where(mask, j_row, -jnp.inf)
        p_ref[idx] = jnp.exp(a - u[:, :CHUNK])
        wi_ref[idx] = jnp.exp(rowb_ref[0][c:c + 1, sl] - u)
        w_k = jnp.exp(j_row - rowb_ref[1][c:c + 1, sl][:, :CHUNK])
        kwt_ref[idx] = (qkc_ref[rows, key_cols[h]].astype(F32).T * w_k).astype(BF16)

    def scores(c, h):
        rows, idx = chunk_rows[c], c * N_HEADS + h
        s = lax.dot_general(qkc_ref[rows, head_cols[h]], qkc_ref[rows, key_cols[h]],
                            (((1,), (1,)), ((), ())), preferred_element_type=F32)
        s_ref[idx] = (s * p_ref[idx]).astype(BF16)

    def local_output(c, h):
        rows, idx = chunk_rows[c], c * N_HEADS + h
        v_aug = jnp.concatenate([v_ref[rows, head_cols[h]], ones_col], axis=1)
        r_ref[rows, 2 * h * HEAD_DIM:2 * (h + 1) * HEAD_DIM] = jnp.dot(
            s_ref[idx], v_aug, preferred_element_type=F32)
        kv_ref[idx] = jnp.dot(kwt_ref[idx], v_aug, preferred_element_type=F32)

    def carry(c, h):
        rows, sl, lf, idx = chunk_rows[c], head_cols[h], f_lane[h], c * N_HEADS + h
        decay = jnp.exp(rowb_ref[0][c:c + 1, sl] - rowb_ref[1][c:c + 1, sl])
        c_aug = c_ref[h]
        wi = wi_ref[idx]
        r = jnp.concatenate([wi, wi], axis=1) * jnp.dot(
            qkc_ref[rows, sl], c_aug.astype(BF16),
            preferred_element_type=F32) + r_ref[rows, 2 * h * HEAD_DIM:2 * (h + 1) * HEAD_DIM]
        den = jnp.maximum(jnp.abs(r[:, HEAD_DIM:HEAD_DIM + 1]), en_ref[rows, lf:lf + 1])
        hid = r[:, :HEAD_DIM] * (1.0 / den)
        c_ref[h] = c_aug * jnp.concatenate([decay, decay], axis=1) + kv_ref[idx]
        if final:
            hid = _head_norm_gate(hid + hf_ref[rows, sl], gn_ref[:, sl],
                                  _sigmoid(og_ref[rows, sl].astype(F32)))
        out_ref[rows, sl] = hid.astype(out_ref.dtype)

    for c in range(n_chunks):
        for h in range(N_HEADS):
            decay_weights(c, h)
    for c in range(n_chunks):
        for h in range(N_HEADS):
            scores(c, h)
    for c in range(n_chunks):
        for h in range(N_HEADS):
            local_output(c, h)
    for c in scan_order:
        for h in range(N_HEADS):
            carry(c, h)

    @pl.when(step == pl.num_programs(1) - 1)
    def _():
        cfin_ref[...] = c_ref[...]
        mfin_ref[...] = m_ref[...]


def _mlstm_scan(p3, ps3, gate_b, c0, m0, gn, h_fwd, *, reverse, t):
    bsz, seq_len, _ = p3.shape
    n_blocks = seq_len // t
    n_chunks = t // CHUNK
    final = h_fwd is not None
    spec = functools.partial(_scan_block_spec, n_blocks=n_blocks, reverse=reverse)
    const2 = lambda shape: pl.BlockSpec(shape, lambda b, i: (0, 0))
    c_spec = pl.BlockSpec((None, N_HEADS, HEAD_DIM, 2 * HEAD_DIM), lambda b, i: (b, 0, 0, 0))
    m_spec = pl.BlockSpec((None, 8, LANES), lambda b, i: (b, 0, 0))
    out_specs = [spec(t, D_GROUP, 0), c_spec, m_spec]
    out_shape = [
        jax.ShapeDtypeStruct((bsz, seq_len, D_GROUP), BF16 if final else F32),
        jax.ShapeDtypeStruct((bsz, N_HEADS, HEAD_DIM, 2 * HEAD_DIM), F32),
        jax.ShapeDtypeStruct((bsz, 8, LANES), F32),
    ]
    lane0 = GATE0 + (2 * N_HEADS if reverse else 0)
    lane = np.arange(2 * SMALL_COLS)[:, None] % SMALL_COLS
    head = np.arange(D_GROUP)[None, :] // HEAD_DIM
    spread = jnp.asarray(lane == lane0 + N_HEADS + head, BF16)
    in_specs = [
        spec(t, 2 * D_GROUP, C_MQK // 2), spec(t, D_GROUP, C_MV), spec(t, SMALL_COLS, 0),
        const2((1, SMALL_COLS)), const2((2 * SMALL_COLS, D_GROUP)), c_spec, m_spec,
    ]
    args = [p3, p3, ps3, gate_b, spread, c0, m0]
    if final:
        in_specs += [spec(t, D_GROUP, 0), spec(t, D_GROUP, C_MO), const2((1, D_GROUP))]
        args += [h_fwd, p3, gn]
    units = n_chunks * N_HEADS
    return pl.pallas_call(
        functools.partial(_mlstm_kernel, reverse=reverse, final=final, n_chunks=n_chunks,
                          lane0=lane0),
        grid=(bsz, n_blocks),
        in_specs=in_specs,
        out_specs=out_specs,
        out_shape=out_shape,
        scratch_shapes=[
            pltpu.VMEM((N_HEADS, HEAD_DIM, 2 * HEAD_DIM), F32),
            pltpu.VMEM((8, LANES), F32),
            pltpu.VMEM((t, SMALL_COLS), F32),
            pltpu.VMEM((t, SMALL_COLS), F32),
            pltpu.VMEM((n_chunks, SMALL_COLS, CHUNK), F32),
            pltpu.VMEM((2, ROW_PAD, D_GROUP), F32),
            pltpu.VMEM((t, D_GROUP), F32),
            pltpu.VMEM((t, SMALL_COLS), F32),
            pltpu.VMEM((units, CHUNK, CHUNK), F32),
            pltpu.VMEM((units, CHUNK, CHUNK), BF16),
            pltpu.VMEM((units, CHUNK, LANES), F32),
            pltpu.VMEM((units, HEAD_DIM, CHUNK), BF16),
            pltpu.VMEM((units, HEAD_DIM, 2 * HEAD_DIM), F32),
            pltpu.VMEM((t, 2 * D_GROUP), F32),
        ],
        compiler_params=_params(("parallel", "arbitrary")),
        name="mlstm_bwd" if reverse else "mlstm_fwd",
    )(*args)


def _outproj_kernel(y0_ref, y1_ref, y2_ref, y3_ref, w_ref, x_ref, g_ref, o_ref):
    acc = jnp.dot(y0_ref[...], w_ref[0:D_GROUP, :], preferred_element_type=F32)
    for n, y_ref in enumerate((y1_ref, y2_ref, y3_ref), start=1):
        acc = acc + jnp.dot(y_ref[...], w_ref[n * D_GROUP:(n + 1) * D_GROUP, :],
                            preferred_element_type=F32)
    o_ref[...] = x_ref[...] + g_ref[...] * acc


def _out_projection(ys, w_out, layer, x2, mod, seq_len, tm):
    m, d = x2.shape
    y_spec = pl.BlockSpec((tm, D_GROUP), lambda i: (i, 0))
    return pl.pallas_call(
        _outproj_kernel,
        grid=(m // tm,),
        in_specs=[y_spec] * N_MIXERS + [
            pl.BlockSpec((None, d, d), lambda i: (layer, 0, 0), pipeline_mode=pl.Buffered(1)),
            pl.BlockSpec((tm, d), lambda i: (i, 0)),
            _row_spec(2, seq_len, tm, d),
        ],
        out_specs=pl.BlockSpec((tm, d), lambda i: (i, 0)),
        out_shape=jax.ShapeDtypeStruct((m, d), F32),
        compiler_params=_params(("parallel",)),
        name="out_projection",
    )(*ys, w_out, x2, mod)


def _ffn_kernel(*refs, period, final):
    if final:
        (x_ref, g_ref, sh_ref, sc_ref, gate_ref, wa_ref, wv_ref, cw_ref, wd_ref, fn_ref,
         o_ref, hn_ref) = refs
    else:
        (x_ref, g_ref, sh_ref, sc_ref, gate_ref, wa_ref, wv_ref, cw_ref, wd_ref,
         o_ref, hn_ref) = refs
    j = pl.program_id(1)
    last = pl.num_programs(1) - 1

    @pl.when(j == 0)
    def _():
        hn_ref[...] = _modulated_norm(x_ref[...], g_ref[...], sc_ref[...], sh_ref[...]).astype(BF16)

    def tile_contribution():
        hn = hn_ref[...]
        a = _dwconv3(jnp.dot(hn, wa_ref[...], preferred_element_type=F32), cw_ref, period)
        v = jnp.dot(hn, wv_ref[...], preferred_element_type=F32)
        hid = (a * _sigmoid(a) * v).astype(BF16)
        return jnp.dot(hid, wd_ref[...], preferred_element_type=F32)

    @pl.when(j == 0)
    def _():
        o_ref[...] = tile_contribution()

    @pl.when(jnp.logical_and(j > 0, j < last))
    def _():
        o_ref[...] += tile_contribution()

    @pl.when(j == last)
    def _():
        y = x_ref[...] + gate_ref[...] * (o_ref[...] + tile_contribution())
        if final:
            y = y * lax.rsqrt(jnp.mean(y * y, axis=-1, keepdims=True) + EPS) * fn_ref[...]
        o_ref[...] = y


def _conv_ffn(x2, gain, mod, w_up, w_conv, w_down, layer, final_gain, seq_len, period, tm,
              tf=512):
    m, d = x2.shape
    d_ff = w_down.shape[1]
    n_f = d_ff // tf
    final = final_gain is not None
    in_specs = [
        pl.BlockSpec((tm, d), lambda i, j: (i, 0)),
        pl.BlockSpec((1, d), lambda i, j: (0, 0)),
        _row_spec(3, seq_len, tm, d), _row_spec(4, seq_len, tm, d), _row_spec(5, seq_len, tm, d),
        pl.BlockSpec((None, d, tf), lambda i, j: (layer, 0, j)),
        pl.BlockSpec((None, d, tf), lambda i, j: (layer, 0, j + n_f)),
        pl.BlockSpec((None, 3, tf), lambda i, j: (layer, 0, j)),
        pl.BlockSpec((None, tf, d), lambda i, j: (layer, j, 0)),
    ]
    args = [x2, gain.reshape(1, d), mod, mod, mod, w_up, w_up, w_conv, w_down]
    if final:
        in_specs.append(pl.BlockSpec((1, d), lambda i, j: (0, 0)))
        args.append(final_gain.reshape(1, d))
    return pl.pallas_call(
        functools.partial(_ffn_kernel, period=period, final=final),
        grid=(m // tm, n_f),
        in_specs=in_specs,
        out_specs=pl.BlockSpec((tm, d), lambda i, j: (i, 0)),
        out_shape=jax.ShapeDtypeStruct((m, d), F32),
        scratch_shapes=[pltpu.VMEM((tm, d), BF16)],
        compiler_params=pltpu.CompilerParams(dimension_semantics=("parallel", "arbitrary"),
                                             vmem_limit_bytes=FFN_VMEM_LIMIT),
        name="conv_ffn",
    )(*args)


def _relayout_w_in(w_in):
    depth, d, _ = w_in.shape
    gla = 4 * D_GROUP + 2 * GLA_LOWRANK
    ml = 4 * D_GROUP + 4 * N_HEADS
    wide = jnp.concatenate([w_in[..., :4 * D_GROUP], w_in[..., gla:gla + 4 * D_GROUP],
                            w_in[..., gla + ml:]], axis=-1).astype(BF16)
    small = jnp.concatenate([
        w_in[..., 4 * D_GROUP:gla], w_in[..., gla + 4 * D_GROUP:gla + ml],
        jnp.zeros((depth, d, SMALL_COLS - 2 * GLA_LOWRANK - 4 * N_HEADS), w_in.dtype)],
        axis=-1).astype(BF16)
    return wide, small


def _scan_mixers(p2, ps2, bsz, seq_len, prm, states, *, t_scan):
    p3 = p2.reshape(bsz, seq_len, -1)
    ps3 = ps2.reshape(bsz, seq_len, SMALL_COLS)
    (gs_f, gs_b), (mc_f, mm_f, mc_b, mm_b) = states
    o_f, gs_f = _gla_scan(p3, ps3, prm["wa_f"], prm["ba_f"], gs_f, None, None,
                          reverse=False, t=t_scan)
    y_gla, gs_b = _gla_scan(p3, ps3, prm["wa_b"], prm["ba_b"], gs_b, prm["gla_norm"], o_f,
                            reverse=True, t=t_scan)
    h_f, mc_f, mm_f = _mlstm_scan(p3, ps3, prm["ml_gate_b"], mc_f, mm_f, None, None,
                                  reverse=False, t=t_scan)
    y_ml, mc_b, mm_b = _mlstm_scan(p3, ps3, prm["ml_gate_b"], mc_b, mm_b, prm["ml_norm"], h_f,
                                   reverse=True, t=t_scan)
    new_states = ((gs_f, gs_b), (mc_f, mm_f, mc_b, mm_b))
    m = bsz * seq_len
    return (y_gla.reshape(m, D_GROUP), y_ml.reshape(m, D_GROUP)), new_states


def kernel(x, c, ctx, c_ctx, w_mod, b_mod, norm1, norm2, w_in, w_out, gla_wa2, gla_ba, gla_norm,
           mlstm_conv, mlstm_gate_b, mlstm_norm, sc_conv, cm_ws, cm_bs, cm_norm,
           ffn_up, ffn_conv, ffn_down, final_norm):
    bsz, seq_len, d = x.shape
    ctx_len = ctx.shape[1]
    depth = w_mod.shape[0]
    assert seq_len % 512 == 0 and ctx_len % CM_CHUNK == 0 and bsz + 1 <= 8

    cond = jnp.zeros((8, d), F32).at[:bsz].set(c).at[bsz].set(c_ctx)
    mod = _modulation(cond, w_mod, b_mod).reshape(depth, 8, 6, d).transpose(0, 2, 1, 3)

    xl = x.reshape(bsz * seq_len, d)
    xc = ctx.reshape(bsz * ctx_len, d)
    tm_l, tm_c = 512, bsz * ctx_len
    tm_in = 512
    tm_ffn = 1024

    zero_states = (
        (jnp.zeros((bsz, N_HEADS, HEAD_DIM, HEAD_DIM), F32),) * 2,
        (jnp.zeros((bsz, N_HEADS, HEAD_DIM, 2 * HEAD_DIM), F32), jnp.zeros((bsz, 8, LANES), F32)) * 2,
    )

    w_wide_all, w_small_all = _relayout_w_in(w_in)
    w_out_b, w_up_b, w_down_b = w_out.astype(BF16), ffn_up.astype(BF16), ffn_down.astype(BF16)
    k_scale = jnp.concatenate([jnp.ones((1, D_GROUP), F32),
                               jnp.full((1, D_GROUP), HEAD_DIM ** -0.5, F32)], axis=1)
    for l in range(depth):
        need_ctx = l < depth - 1
        mod_l = mod[l][:, :bsz, None, :]
        mod_c = jnp.broadcast_to(mod[l][:, bsz:bsz + 1, None, :], (6, bsz, 1, d))
        wa = jnp.zeros((2, SMALL_COLS, D_GROUP), F32)
        wa = wa.at[0, Z_FWD:Z_FWD + GLA_LOWRANK].set(gla_wa2[l, 0])
        wa = wa.at[1, Z_BWD:Z_BWD + GLA_LOWRANK].set(gla_wa2[l, 1])
        wa_hi = wa.astype(BF16)
        wa_lo = (wa - wa_hi.astype(F32)).astype(BF16)
        wa = jnp.concatenate([wa_hi, wa_hi, wa_lo], axis=1)
        gate_b = jnp.zeros((1, SMALL_COLS), F32).at[0, GATE0:GATE0 + 4 * N_HEADS].set(
            mlstm_gate_b[l].reshape(-1))
        prm = dict(
            wa_f=wa[0], wa_b=wa[1], ba_f=gla_ba[l, 0:1], ba_b=gla_ba[l, 1:2],
            gla_norm=gla_norm[l].reshape(1, D_GROUP),
            ml_conv=mlstm_conv[l] * k_scale, ml_gate_b=gate_b,
            ml_norm=mlstm_norm[l].reshape(1, D_GROUP),
            sc_conv=sc_conv[l], cm_ws=cm_ws[l].astype(BF16),
            cm_bs=jnp.repeat(cm_bs[l].T, HEAD_DIM, axis=1),
            cm_norm=cm_norm[l].reshape(1, D_GROUP),
        )

        pc, psc, *yc_local = _in_projection(xc, norm1[l], mod_c, w_wide_all, w_small_all, l, prm,
                                            ctx_len, tm_c, ctx_len)
        yc_scan, ctx_states = _scan_mixers(pc, psc, bsz, ctx_len, prm, zero_states,
                                           t_scan=ctx_len)
        p_l, ps_l, *yl_local = _in_projection(xl, norm1[l], mod_l, w_wide_all, w_small_all, l, prm,
                                              seq_len, tm_in, GRID_W)
        yl_scan, _ = _scan_mixers(p_l, ps_l, bsz, seq_len, prm, ctx_states,
                                  t_scan=512)
        yl, yc = yl_scan + tuple(yl_local), yc_scan + tuple(yc_local)

        xl = _out_projection(yl, w_out_b, l, xl, mod_l, seq_len, tm_l)
        xl = _conv_ffn(xl, norm2[l], mod_l, w_up_b, ffn_conv, w_down_b, l,
                       final_norm if l == depth - 1 else None, seq_len, GRID_W, tm_ffn)
        if need_ctx:
            xc = _out_projection(yc, w_out_b, l, xc, mod_c, ctx_len, tm_c)
            xc = _conv_ffn(xc, norm2[l], mod_c, w_up_b, ffn_conv, w_down_b, l, None,
                           ctx_len, ctx_len, tm_c)
    return xl.reshape(bsz, seq_len, d)
```

```python
import functools

import numpy as np
import jax
import jax.numpy as jnp
from jax import lax
from jax.experimental import pallas as pl
from jax.experimental.pallas import tpu as pltpu

F32 = jnp.float32
BF16 = jnp.bfloat16

GRID_W = 64
N_MIXERS = 4
HEAD_DIM = 128
N_HEADS = 4
D_GROUP = N_HEADS * HEAD_DIM
GLA_LOWRANK = 16
GLA_GATE_NORM = 16.0
CHUNK = 64
CM_CHUNK = 128
EPS = 1e-6
LANES = 128
SMALL_COLS = LANES
Z_FWD, Z_BWD, GATE0 = 0, GLA_LOWRANK, 2 * GLA_LOWRANK
VMEM_LIMIT = 56 * 1024 * 1024
INPROJ_VMEM_LIMIT = 60 * 1024 * 1024
FFN_VMEM_LIMIT = 62 * 1024 * 1024

(C_GQ, C_GK, C_GV, C_GG, C_MQK, C_MV, C_MO, C_SB, C_SC, C_SH, C_CU, C_CV) = (
    0, 1, 2, 3, 4, 6, 7, 8, 9, 10, 11, 12)
N_WIDE_BLOCKS = 13
N_SCAN_BLOCKS = 8
ROW_PAD = 16


def _params(sem):
    return pltpu.CompilerParams(dimension_semantics=sem, vmem_limit_bytes=VMEM_LIMIT)


def _sigmoid(x):
    return 1.0 / (1.0 + jnp.exp(-x))


def _log_sigmoid(x):
    return jnp.minimum(x, 0.0) - jnp.log(1.0 + jnp.exp(-jnp.abs(x)))


def _gelu_tanh(x):
    return x * (0.5 * (1.0 + jnp.tanh(0.7978845608028654 * (x + 0.044715 * (x * x * x)))))


def _shift_rows(x, period):
    n = x.shape[0]
    row = lax.broadcasted_iota(jnp.int32, (n, 1), 0) % period
    prev = jnp.where(row == 0, 0.0, pltpu.roll(x, 1, axis=0))
    nxt = jnp.where(row == period - 1, 0.0, pltpu.roll(x, n - 1, axis=0))
    return prev, nxt


def _dwconv3(x, w_ref, period):
    prev, nxt = _shift_rows(x, period)
    return w_ref[0:1, :] * prev + w_ref[1:2, :] * x + w_ref[2:3, :] * nxt


def _mod_kernel(c_ref, w_ref, b_ref, o_ref):
    s = c_ref[...]
    s = (s * _sigmoid(s)).astype(BF16)
    o_ref[...] = jnp.dot(s, w_ref[...].astype(BF16), preferred_element_type=F32) + b_ref[...]


def _modulation(cond, w_mod, b_mod, tn=1024):
    depth, d, n = w_mod.shape
    rows = cond.shape[0]
    return pl.pallas_call(
        _mod_kernel,
        grid=(depth, n // tn),
        in_specs=[
            pl.BlockSpec((rows, d), lambda l, j: (0, 0)),
            pl.BlockSpec((None, d, tn), lambda l, j: (l, 0, j)),
            pl.BlockSpec((None, 1, tn), lambda l, j: (l, 0, j)),
        ],
        out_specs=pl.BlockSpec((None, rows, tn), lambda l, j: (l, 0, j)),
        out_shape=jax.ShapeDtypeStruct((depth, rows, n), F32),
        compiler_params=_params(("parallel", "parallel")),
        name="modulation",
    )(cond, w_mod, b_mod.reshape(depth, 1, n))


def _modulated_norm(x, g, sc, sh):
    r = lax.rsqrt(jnp.mean(x * x, axis=-1, keepdims=True) + EPS)
    return (x * r * g) * (1.0 + sc) + sh


def _inproj_kernel(x_ref, g_ref, sh_ref, sc_ref, w_ref, ws_ref, scw_ref, cws_ref, cbs_ref,
                   cgv_ref, mlw_ref, p_ref, ps_ref, ysc_ref, ycm_ref, *, period):
    hn = _modulated_norm(x_ref[...], g_ref[...], sc_ref[...], sh_ref[...]).astype(BF16)

    def proj(block):
        cols = slice(block * D_GROUP, (block + 1) * D_GROUP)
        return jnp.dot(hn, w_ref[:, cols], preferred_element_type=F32)

    ysc_ref[...] = (proj(C_SB) * _dwconv3(proj(C_SC) * proj(C_SH), scw_ref, period)
                    ).astype(ysc_ref.dtype)

    v = _gelu_tanh(proj(C_CV))
    vn = (v * lax.rsqrt(jnp.mean(v * v, axis=-1, keepdims=True) + EPS) * cgv_ref[...]).astype(BF16)
    u = _gelu_tanh(proj(C_CU))
    for n in range(v.shape[0] // CM_CHUNK):
        rows = slice(n * CM_CHUNK, (n + 1) * CM_CHUNK)
        for g in range(N_HEADS):
            sl = slice(g * HEAD_DIM, (g + 1) * HEAD_DIM)
            sv = jnp.dot(cws_ref[g], vn[rows, sl], preferred_element_type=F32) + cbs_ref[:, sl]
            ycm_ref[rows, sl] = (u[rows, sl] * sv).astype(ycm_ref.dtype)

    for n, block in enumerate((C_MQK, C_MQK + 1)):
        cols = slice(block * D_GROUP, (block + 1) * D_GROUP)
        p_ref[:, cols] = _dwconv3(proj(block), mlw_ref.at[:, n * D_GROUP:(n + 1) * D_GROUP],
                                  period).astype(p_ref.dtype)

    ps_ref[...] = jnp.dot(hn, ws_ref[...], preferred_element_type=F32)
    for block in range(N_SCAN_BLOCKS):
        if block not in (C_MQK, C_MQK + 1):
            p_ref[:, block * D_GROUP:(block + 1) * D_GROUP] = proj(block).astype(p_ref.dtype)


def _row_spec(piece, seq_len, tm, d):
    return pl.BlockSpec((None, None, 1, d), lambda i, *_: (piece, (i * tm) // seq_len, 0, 0))


def _in_projection(x2, gain, mod, w_wide, w_small, layer, prm, seq_len, tm, period):
    m, d = x2.shape
    n = w_wide.shape[2]
    n_scan = N_SCAN_BLOCKS * D_GROUP
    resident = lambda shape: pl.BlockSpec((None,) + shape, lambda i: (layer, 0, 0),
                                          pipeline_mode=pl.Buffered(1))
    const = lambda shape: pl.BlockSpec(shape, lambda i: (0,) * len(shape))
    tile = lambda cols: pl.BlockSpec((tm, cols), lambda i: (i, 0))
    return pl.pallas_call(
        functools.partial(_inproj_kernel, period=period),
        grid=(m // tm,),
        in_specs=[
            tile(d),
            const((1, d)),
            _row_spec(0, seq_len, tm, d),
            _row_spec(1, seq_len, tm, d),
            resident((d, n)),
            resident((d, SMALL_COLS)),
            const((3, D_GROUP)), const((N_HEADS, CM_CHUNK, CM_CHUNK)),
            const((CM_CHUNK, D_GROUP)), const((1, D_GROUP)), const((3, 2 * D_GROUP)),
        ],
        out_specs=[tile(n_scan), tile(SMALL_COLS), tile(D_GROUP), tile(D_GROUP)],
        out_shape=[
            jax.ShapeDtypeStruct((m, n_scan), BF16),
            jax.ShapeDtypeStruct((m, SMALL_COLS), F32),
            jax.ShapeDtypeStruct((m, D_GROUP), BF16),
            jax.ShapeDtypeStruct((m, D_GROUP), BF16),
        ],
        compiler_params=pltpu.CompilerParams(dimension_semantics=("parallel",),
                                             vmem_limit_bytes=INPROJ_VMEM_LIMIT),
        name="in_projection",
    )(x2, gain.reshape(1, d), mod, mod, w_wide, w_small,
      prm["sc_conv"], prm["cm_ws"], prm["cm_bs"], prm["cm_norm"], prm["ml_conv"])


def _tri(reverse):
    r = lax.broadcasted_iota(jnp.int32, (CHUNK, CHUNK), 0)
    c = lax.broadcasted_iota(jnp.int32, (CHUNK, CHUNK), 1)
    return (c >= r) if reverse else (c <= r)


def _tri2(reverse):
    r = lax.broadcasted_iota(jnp.int32, (CHUNK, 2 * CHUNK), 0)
    c = lax.broadcasted_iota(jnp.int32, (CHUNK, 2 * CHUNK), 1) % CHUNK
    return ((c >= r) if reverse else (c <= r)).astype(BF16)


def _split2(x):
    hi = x.astype(BF16)
    return hi, (x - hi.astype(F32)).astype(BF16)


def _chunk_cumsum(tri2, x):
    hi, lo = _split2(x)
    return jnp.dot(tri2, jnp.concatenate([hi, lo], axis=0), preferred_element_type=F32)


def _emit_skewed(order, chunk_stages, head_stages):
    n_stage = len(chunk_stages) + len(head_stages)
    for slot in range(len(order) + n_stage - 1):
        for k, stage in enumerate(chunk_stages):
            if 0 <= slot - k < len(order):
                stage(order[slot - k])
        for h in range(N_HEADS):
            for k, stage in enumerate(head_stages, start=len(chunk_stages)):
                if 0 <= slot - k < len(order):
                    stage(order[slot - k], h)


def _cummax_rows(x, reverse):
    n = x.shape[0]
    row = lax.broadcasted_iota(jnp.int32, (n, 1), 0)
    k = 1
    while k < n:
        if reverse:
            shifted = jnp.where(row >= n - k, -jnp.inf, pltpu.roll(x, n - k, axis=0))
        else:
            shifted = jnp.where(row < k, -jnp.inf, pltpu.roll(x, k, axis=0))
        x = jnp.maximum(x, shifted)
        k *= 2
    return x


def _lane_spread(x, bm_ref):
    hi, lo = _split2(x)
    return jnp.dot(jnp.concatenate([hi, lo], axis=1), bm_ref[...], preferred_element_type=F32)


def _head_norm_gate(o, gain, gate):
    r = lax.rsqrt(jnp.mean(o * o, axis=-1, keepdims=True) + EPS)
    return o * r * gain * gate


def _gla_kernel(*refs, reverse, final, n_chunks):
    if final:
        (q_ref, k_ref, v_ref, z_ref, wa_ref, ba_ref, s0_ref, of_ref, g_ref, gn_ref,
         out_ref, sfin_ref, st_ref, qin_ref, qrel_ref, krel_ref, dec_ref, att_ref, kv_ref,
         acc_ref) = refs
    else:
        (q_ref, k_ref, v_ref, z_ref, wa_ref, ba_ref, s0_ref,
         out_ref, sfin_ref, st_ref, qin_ref, qrel_ref, krel_ref, dec_ref, att_ref,
         kv_ref) = refs
        acc_ref = out_ref
    step = pl.program_id(1)

    @pl.when(step == 0)
    def _():
        st_ref[...] = s0_ref[...]

    mask = _tri(reverse)
    tri2 = _tri2(reverse)
    last = 0 if reverse else CHUNK - 1
    scale = HEAD_DIM ** -0.5
    chunk_rows = [slice(c * CHUNK, (c + 1) * CHUNK) for c in range(n_chunks)]
    head_cols = [slice(h * HEAD_DIM, (h + 1) * HEAD_DIM) for h in range(N_HEADS)]

    z_hi, z_lo = _split2(z_ref[...])
    pre = jnp.dot(jnp.concatenate([z_hi, z_lo, z_hi], axis=1), wa_ref[...],
                  preferred_element_type=F32) + ba_ref[...]
    log_a = _log_sigmoid(pre) * (1.0 / GLA_GATE_NORM)

    def decayed_qk(c):
        rows = chunk_rows[c]
        b_cum = _chunk_cumsum(tri2, log_a[rows, :])
        b_last = b_cum[last:last + 1, :]
        q = q_ref[rows, :].astype(F32) * scale
        qin_ref[rows, :] = (q * jnp.exp(b_cum)).astype(BF16)
        qrel_ref[rows, :] = (q * jnp.exp(b_cum - b_last)).astype(BF16)
        krel_ref[rows, :] = (k_ref[rows, :].astype(F32) * jnp.exp(b_last - b_cum)).astype(BF16)
        dec_ref[c] = jnp.exp(b_last)

    def scores(c, h):
        rows, sl = chunk_rows[c], head_cols[h]
        att = lax.dot_general(qrel_ref[rows, sl], krel_ref[rows, sl],
                              (((1,), (1,)), ((), ())), preferred_element_type=F32)
        att_ref[c * N_HEADS + h] = jnp.where(mask, att, 0.0).astype(BF16)

    def local_output(c, h):
        rows, sl = chunk_rows[c], head_cols[h]
        v = v_ref[rows, sl]
        acc_ref[rows, sl] = jnp.dot(att_ref[c * N_HEADS + h], v, preferred_element_type=F32)
        kv_ref[c * N_HEADS + h] = lax.dot_general(
            v, krel_ref[rows, sl], (((0,), (0,)), ((), ())), preferred_element_type=F32)

    def carry(c, h):
        rows, sl = chunk_rows[c], head_cols[h]
        s_t = st_ref[h]
        o = acc_ref[rows, sl] + lax.dot_general(qin_ref[rows, sl], s_t.astype(BF16),
                                                (((1,), (1,)), ((), ())),
                                                preferred_element_type=F32)
        st_ref[h] = s_t * dec_ref[c][:, sl] + kv_ref[c * N_HEADS + h]
        if final:
            g = g_ref[rows, sl].astype(F32)
            o = _head_norm_gate(o + of_ref[rows, sl], gn_ref[:, sl], g * _sigmoid(g))
        out_ref[rows, sl] = o.astype(out_ref.dtype)

    _emit_skewed(list(reversed(range(n_chunks))) if reverse else list(range(n_chunks)),
                 [decayed_qk], [scores, local_output, carry])

    @pl.when(step == pl.num_programs(1) - 1)
    def _():
        sfin_ref[...] = st_ref[...]


def _scan_block_spec(t, cols, col_block, n_blocks, reverse):
    if reverse:
        return pl.BlockSpec((None, t, cols), lambda b, i: (b, n_blocks - 1 - i, col_block))
    return pl.BlockSpec((None, t, cols), lambda b, i: (b, i, col_block))


def _gla_scan(p3, ps3, wa_pad, ba, s0, gn, o_fwd, *, reverse, t):
    bsz, seq_len, _ = p3.shape
    n_blocks = seq_len // t
    n_chunks = t // CHUNK
    final = o_fwd is not None
    spec = functools.partial(_scan_block_spec, n_blocks=n_blocks, reverse=reverse)
    const2 = lambda shape: pl.BlockSpec(shape, lambda b, i: (0, 0))
    state_spec = pl.BlockSpec((None, N_HEADS, HEAD_DIM, HEAD_DIM), lambda b, i: (b, 0, 0, 0))
    in_specs = [
        spec(t, D_GROUP, C_GQ), spec(t, D_GROUP, C_GK), spec(t, D_GROUP, C_GV),
        spec(t, SMALL_COLS, 0),
        const2((3 * SMALL_COLS, D_GROUP)), const2((1, D_GROUP)),
        state_spec,
    ]
    args = [p3, p3, p3, ps3, wa_pad, ba, s0]
    if final:
        in_specs += [spec(t, D_GROUP, 0), spec(t, D_GROUP, C_GG), const2((1, D_GROUP))]
        args += [o_fwd, p3, gn]
    return pl.pallas_call(
        functools.partial(_gla_kernel, reverse=reverse, final=final, n_chunks=n_chunks),
        grid=(bsz, n_blocks),
        in_specs=in_specs,
        out_specs=[spec(t, D_GROUP, 0), state_spec],
        out_shape=[
            jax.ShapeDtypeStruct((bsz, seq_len, D_GROUP), BF16 if final else F32),
            jax.ShapeDtypeStruct((bsz, N_HEADS, HEAD_DIM, HEAD_DIM), F32),
        ],
        scratch_shapes=[
            pltpu.VMEM((N_HEADS, HEAD_DIM, HEAD_DIM), F32),
            pltpu.VMEM((t, D_GROUP), BF16), pltpu.VMEM((t, D_GROUP), BF16),
            pltpu.VMEM((t, D_GROUP), BF16),
            pltpu.VMEM((n_chunks, 1, D_GROUP), F32),
            pltpu.VMEM((n_chunks * N_HEADS, CHUNK, CHUNK), BF16),
            pltpu.VMEM((n_chunks * N_HEADS, HEAD_DIM, HEAD_DIM), F32),
        ] + ([pltpu.VMEM((t, D_GROUP), F32)] if final else []),
        compiler_params=_params(("parallel", "arbitrary")),
        name="gla_bwd" if reverse else "gla_fwd",
    )(*args)


def _mlstm_kernel(*refs, reverse, final, n_chunks, lane0):
    if final:
        (qkc_ref, v_ref, gt_ref, gb_ref, bm_ref, c0_ref, m0_ref, hf_ref, og_ref, gn_ref,
         out_ref, cfin_ref, mfin_ref, c_ref, m_ref, fcol_ref, jcol_ref, jt_ref,
         rowb_ref, ub_ref, mb_ref, p_ref, s_ref, wi_ref, kwt_ref, kv_ref, r_ref) = refs
    else:
        (qkc_ref, v_ref, gt_ref, gb_ref, bm_ref, c0_ref, m0_ref,
         out_ref, cfin_ref, mfin_ref, c_ref, m_ref, fcol_ref, jcol_ref, jt_ref,
         rowb_ref, ub_ref, mb_ref, p_ref, s_ref, wi_ref, kwt_ref, kv_ref, r_ref) = refs
    step = pl.program_id(1)

    @pl.when(step == 0)
    def _():
        c_ref[...] = c0_ref[...]
        m_ref[...] = m0_ref[...]

    mask = _tri(reverse)
    tri2 = _tri2(reverse)
    last = 0 if reverse else CHUNK - 1
    ones_blk = jnp.ones((CHUNK, HEAD_DIM), BF16)
    chunk_rows = [slice(c * CHUNK, (c + 1) * CHUNK) for c in range(n_chunks)]
    head_cols = [slice(h * HEAD_DIM, (h + 1) * HEAD_DIM) for h in range(N_HEADS)]
    key_cols = [slice(D_GROUP + h * HEAD_DIM, D_GROUP + (h + 1) * HEAD_DIM) for h in range(N_HEADS)]
    f_lane = [lane0 + N_HEADS + h for h in range(N_HEADS)]
    scan_order = list(reversed(range(n_chunks))) if reverse else list(range(n_chunks))

    gates = gt_ref[...] + gb_ref[...]
    log_f = _log_sigmoid(gates)
    i_al = pltpu.roll(gates, N_HEADS, axis=1)
    j_max, f_tot = [], []
    for c, rows in enumerate(chunk_rows):
        f_col = _chunk_cumsum(tri2, log_f[rows, :])
        j_col = i_al[rows, :] - f_col
        fcol_ref[rows, :] = f_col
        jcol_ref[rows, :] = j_col
        jt_ref[c] = j_col.T
        j_max.append(jnp.max(j_col, axis=0, keepdims=True))
        f_tot.append(f_col[last:last + 1, :])

    m_run = m_ref[0:1, :]
    m_prev, u_last = [None] * n_chunks, [None] * n_chunks
    for c in scan_order:
        m_prev[c] = m_run
        u_last[c] = jnp.maximum(m_run, j_max[c])
        m_run = f_tot[c] + u_last[c]
    m_ref[0:1, :] = m_run
    pad_rows = [jnp.zeros((ROW_PAD - n_chunks, LANES), F32)]
    rowb_ref[0] = _lane_spread(jnp.concatenate(m_prev + pad_rows, axis=0), bm_ref)
    rowb_ref[1] = _lane_spread(jnp.concatenate(u_last + pad_rows, axis=0), bm_ref)

    for c, rows in enumerate(chunk_rows):
        u_col = jnp.maximum(_cummax_rows(jcol_ref[rows, :], reverse), m_prev[c])
        ub_ref[rows, :] = _lane_spread(u_col, bm_ref)
        mb_ref[rows, :] = _lane_spread(fcol_ref[rows, :] + u_col, bm_ref)

    def decay_weights(c, h):
        rows, sl, lf, idx = chunk_rows[c], head_cols[h], f_lane[h], c * N_HEADS + h
        u = ub_ref[rows, sl]
        j_row = jt_ref[c][lf:lf + 1, :]
        a = jnp.where(mask, j_row, -jnp.inf)
        p_ref[idx] = jnp.exp(a - u[:, :CHUNK])
        wi_ref[idx] = jnp.exp(rowb_ref[0][c:c + 1, sl] - u)
        w_k = jnp.exp(j_row - rowb_ref[1][c:c + 1, sl][:, :CHUNK])
        kwt_ref[idx] = (qkc_ref[rows, key_cols[h]].astype(F32).T * w_k).astype(BF16)

    def scores(c, h):
        rows, idx = chunk_rows[c], c * N_HEADS + h
        s = lax.dot_general(qkc_ref[rows, head_cols[h]], qkc_ref[rows, key_cols[h]],
                            (((1,), (1,)), ((), ())), preferred_element_type=F32)
        s_ref[idx] = (s * p_ref[idx]).astype(BF16)

    def local_output(c, h):
        rows, idx = chunk_rows[c], c * N_HEADS + h
        v_aug = jnp.concatenate([v_ref[rows, head_cols[h]], ones_blk], axis=1)
        r_ref[rows, 2 * h * HEAD_DIM:2 * (h + 1) * HEAD_DIM] = jnp.dot(
            s_ref[idx], v_aug, preferred_element_type=F32)
        kv_ref[idx] = jnp.dot(kwt_ref[idx], v_aug, preferred_element_type=F32)

    def carry(c, h):
        rows, sl, idx = chunk_rows[c], head_cols[h], c * N_HEADS + h
        decay = jnp.exp(rowb_ref[0][c:c + 1, sl] - rowb_ref[1][c:c + 1, sl])
        c_aug = c_ref[h]
        wi = wi_ref[idx]
        r = jnp.concatenate([wi, wi], axis=1) * jnp.dot(
            qkc_ref[rows, sl], c_aug.astype(BF16),
            preferred_element_type=F32) + r_ref[rows, 2 * h * HEAD_DIM:2 * (h + 1) * HEAD_DIM]
        den = jnp.maximum(jnp.abs(r[:, HEAD_DIM:]), jnp.exp(-mb_ref[rows, sl]))
        hid = r[:, :HEAD_DIM] * (1.0 / den)
        c_ref[h] = c_aug * jnp.concatenate([decay, decay], axis=1) + kv_ref[idx]
        if final:
            hid = _head_norm_gate(hid + hf_ref[rows, sl], gn_ref[:, sl],
                                  _sigmoid(og_ref[rows, sl].astype(F32)))
        out_ref[rows, sl] = hid.astype(out_ref.dtype)

    for c in range(n_chunks):
        for h in range(N_HEADS):
            decay_weights(c, h)
    for c in range(n_chunks):
        for h in range(N_HEADS):
            scores(c, h)
    for c in range(n_chunks):
        for h in range(N_HEADS):
            local_output(c, h)
    for c in scan_order:
        for h in range(N_HEADS):
            carry(c, h)

    @pl.when(step == pl.num_programs(1) - 1)
    def _():
        cfin_ref[...] = c_ref[...]
        mfin_ref[...] = m_ref[...]


def _mlstm_scan(p3, ps3, gate_b, c0, m0, gn, h_fwd, *, reverse, t):
    bsz, seq_len, _ = p3.shape
    n_blocks = seq_len // t
    n_chunks = t // CHUNK
    final = h_fwd is not None
    spec = functools.partial(_scan_block_spec, n_blocks=n_blocks, reverse=reverse)
    const2 = lambda shape: pl.BlockSpec(shape, lambda b, i: (0, 0))
    c_spec = pl.BlockSpec((None, N_HEADS, HEAD_DIM, 2 * HEAD_DIM), lambda b, i: (b, 0, 0, 0))
    m_spec = pl.BlockSpec((None, 8, LANES), lambda b, i: (b, 0, 0))
    out_specs = [spec(t, D_GROUP, 0), c_spec, m_spec]
    out_shape = [
        jax.ShapeDtypeStruct((bsz, seq_len, D_GROUP), BF16 if final else F32),
        jax.ShapeDtypeStruct((bsz, N_HEADS, HEAD_DIM, 2 * HEAD_DIM), F32),
        jax.ShapeDtypeStruct((bsz, 8, LANES), F32),
    ]
    lane0 = GATE0 + (2 * N_HEADS if reverse else 0)
    lane = np.arange(2 * SMALL_COLS)[:, None] % SMALL_COLS
    head = np.arange(D_GROUP)[None, :] // HEAD_DIM
    spread = jnp.asarray(lane == lane0 + N_HEADS + head, BF16)
    in_specs = [
        spec(t, 2 * D_GROUP, C_MQK // 2), spec(t, D_GROUP, C_MV), spec(t, SMALL_COLS, 0),
        const2((1, SMALL_COLS)), const2((2 * SMALL_COLS, D_GROUP)), c_spec, m_spec,
    ]
    args = [p3, p3, ps3, gate_b, spread, c0, m0]
    if final:
        in_specs += [spec(t, D_GROUP, 0), spec(t, D_GROUP, C_MO), const2((1, D_GROUP))]
        args += [h_fwd, p3, gn]
    units = n_chunks * N_HEADS
    return pl.pallas_call(
        functools.partial(_mlstm_kernel, reverse=reverse, final=final, n_chunks=n_chunks,
                          lane0=lane0),
        grid=(bsz, n_blocks),
        in_specs=in_specs,
        out_specs=out_specs,
        out_shape=out_shape,
        scratch_shapes=[
            pltpu.VMEM((N_HEADS, HEAD_DIM, 2 * HEAD_DIM), F32),
            pltpu.VMEM((8, LANES), F32),
            pltpu.VMEM((t, SMALL_COLS), F32),
            pltpu.VMEM((t, SMALL_COLS), F32),
            pltpu.VMEM((n_chunks, SMALL_COLS, CHUNK), F32),
            pltpu.VMEM((2, ROW_PAD, D_GROUP), F32),
            pltpu.VMEM((t, D_GROUP), F32),
            pltpu.VMEM((t, D_GROUP), F32),
            pltpu.VMEM((units, CHUNK, CHUNK), F32),
            pltpu.VMEM((units, CHUNK, CHUNK), BF16),
            pltpu.VMEM((units, CHUNK, LANES), F32),
            pltpu.VMEM((units, HEAD_DIM, CHUNK), BF16),
            pltpu.VMEM((units, HEAD_DIM, 2 * HEAD_DIM), F32),
            pltpu.VMEM((t, 2 * D_GROUP), F32),
        ],
        compiler_params=_params(("parallel", "arbitrary")),
        name="mlstm_bwd" if reverse else "mlstm_fwd",
    )(*args)


def _outproj_kernel(y0_ref, y1_ref, y2_ref, y3_ref, w_ref, x_ref, g_ref, o_ref):
    acc = jnp.dot(y0_ref[...], w_ref[0:D_GROUP, :], preferred_element_type=F32)
    for n, y_ref in enumerate((y1_ref, y2_ref, y3_ref), start=1):
        acc = acc + jnp.dot(y_ref[...], w_ref[n * D_GROUP:(n + 1) * D_GROUP, :],
                            preferred_element_type=F32)
    o_ref[...] = x_ref[...] + g_ref[...] * acc


def _out_projection(ys, w_out, layer, x2, mod, seq_len, tm):
    m, d = x2.shape
    y_spec = pl.BlockSpec((tm, D_GROUP), lambda i: (i, 0))
    return pl.pallas_call(
        _outproj_kernel,
        grid=(m // tm,),
        in_specs=[y_spec] * N_MIXERS + [
            pl.BlockSpec((None, d, d), lambda i: (layer, 0, 0), pipeline_mode=pl.Buffered(1)),
            pl.BlockSpec((tm, d), lambda i: (i, 0)),
            _row_spec(2, seq_len, tm, d),
        ],
        out_specs=pl.BlockSpec((tm, d), lambda i: (i, 0)),
        out_shape=jax.ShapeDtypeStruct((m, d), F32),
        compiler_params=_params(("parallel",)),
        name="out_projection",
    )(*ys, w_out, x2, mod)


def _ffn_kernel(*refs, period, final):
    if final:
        (x_ref, g_ref, sh_ref, sc_ref, gate_ref, wa_ref, wv_ref, cw_ref, wd_ref, fn_ref,
         o_ref, hn_ref) = refs
    else:
        (x_ref, g_ref, sh_ref, sc_ref, gate_ref, wa_ref, wv_ref, cw_ref, wd_ref,
         o_ref, hn_ref) = refs
    j = pl.program_id(1)
    last = pl.num_programs(1) - 1

    @pl.when(j == 0)
    def _():
        hn_ref[...] = _modulated_norm(x_ref[...], g_ref[...], sc_ref[...], sh_ref[...]).astype(BF16)

    def tile_contribution():
        hn = hn_ref[...]
        a = _dwconv3(jnp.dot(hn, wa_ref[...], preferred_element_type=F32), cw_ref, period)
        v = jnp.dot(hn, wv_ref[...], preferred_element_type=F32)
        hid = (a * _sigmoid(a) * v).astype(BF16)
        return jnp.dot(hid, wd_ref[...], preferred_element_type=F32)

    @pl.when(j == 0)
    def _():
        o_ref[...] = tile_contribution()

    @pl.when(jnp.logical_and(j > 0, j < last))
    def _():
        o_ref[...] += tile_contribution()

    @pl.when(j == last)
    def _():
        y = x_ref[...] + gate_ref[...] * (o_ref[...] + tile_contribution())
        if final:
            y = y * lax.rsqrt(jnp.mean(y * y, axis=-1, keepdims=True) + EPS) * fn_ref[...]
        o_ref[...] = y


def _conv_ffn(x2, gain, mod, w_up, w_conv, w_down, layer, final_gain, seq_len, period, tm,
              tf=512):
    m, d = x2.shape
    d_ff = w_down.shape[1]
    n_f = d_ff // tf
    final = final_gain is not None
    in_specs = [
        pl.BlockSpec((tm, d), lambda i, j: (i, 0)),
        pl.BlockSpec((1, d), lambda i, j: (0, 0)),
        _row_spec(3, seq_len, tm, d), _row_spec(4, seq_len, tm, d), _row_spec(5, seq_len, tm, d),
        pl.BlockSpec((None, d, tf), lambda i, j: (layer, 0, j)),
        pl.BlockSpec((None, d, tf), lambda i, j: (layer, 0, j + n_f)),
        pl.BlockSpec((None, 3, tf), lambda i, j: (layer, 0, j)),
        pl.BlockSpec((None, tf, d), lambda i, j: (layer, j, 0)),
    ]
    args = [x2, gain.reshape(1, d), mod, mod, mod, w_up, w_up, w_conv, w_down]
    if final:
        in_specs.append(pl.BlockSpec((1, d), lambda i, j: (0, 0)))
        args.append(final_gain.reshape(1, d))
    return pl.pallas_call(
        functools.partial(_ffn_kernel, period=period, final=final),
        grid=(m // tm, n_f),
        in_specs=in_specs,
        out_specs=pl.BlockSpec((tm, d), lambda i, j: (i, 0)),
        out_shape=jax.ShapeDtypeStruct((m, d), F32),
        scratch_shapes=[pltpu.VMEM((tm, d), BF16)],
        compiler_params=pltpu.CompilerParams(dimension_semantics=("parallel", "arbitrary"),
                                             vmem_limit_bytes=FFN_VMEM_LIMIT),
        name="conv_ffn",
    )(*args)


def _relayout_w_in(w_in):
    depth, d, _ = w_in.shape
    gla = 4 * D_GROUP + 2 * GLA_LOWRANK
    ml = 4 * D_GROUP + 4 * N_HEADS
    wide = jnp.concatenate([w_in[..., :4 * D_GROUP], w_in[..., gla:gla + 4 * D_GROUP],
                            w_in[..., gla + ml:]], axis=-1).astype(BF16)
    small = jnp.concatenate([
        w_in[..., 4 * D_GROUP:gla], w_in[..., gla + 4 * D_GROUP:gla + ml],
        jnp.zeros((depth, d, SMALL_COLS - 2 * GLA_LOWRANK - 4 * N_HEADS), w_in.dtype)],
        axis=-1).astype(BF16)
    return wide, small


def _scan_mixers(p2, ps2, bsz, seq_len, prm, states, *, t_scan):
    p3 = p2.reshape(bsz, seq_len, -1)
    ps3 = ps2.reshape(bsz, seq_len, SMALL_COLS)
    (gs_f, gs_b), (mc_f, mm_f, mc_b, mm_b) = states
    o_f, gs_f = _gla_scan(p3, ps3, prm["wa_f"], prm["ba_f"], gs_f, None, None,
                          reverse=False, t=t_scan)
    y_gla, gs_b = _gla_scan(p3, ps3, prm["wa_b"], prm["ba_b"], gs_b, prm["gla_norm"], o_f,
                            reverse=True, t=t_scan)
    h_f, mc_f, mm_f = _mlstm_scan(p3, ps3, prm["ml_gate_b"], mc_f, mm_f, None, None,
                                  reverse=False, t=t_scan)
    y_ml, mc_b, mm_b = _mlstm_scan(p3, ps3, prm["ml_gate_b"], mc_b, mm_b, prm["ml_norm"], h_f,
                                   reverse=True, t=t_scan)
    new_states = ((gs_f, gs_b), (mc_f, mm_f, mc_b, mm_b))
    m = bsz * seq_len
    return (y_gla.reshape(m, D_GROUP), y_ml.reshape(m, D_GROUP)), new_states


def kernel(x, c, ctx, c_ctx, w_mod, b_mod, norm1, norm2, w_in, w_out, gla_wa2, gla_ba, gla_norm,
           mlstm_conv, mlstm_gate_b, mlstm_norm, sc_conv, cm_ws, cm_bs, cm_norm,
           ffn_up, ffn_conv, ffn_down, final_norm):
    bsz, seq_len, d = x.shape
    ctx_len = ctx.shape[1]
    depth = w_mod.shape[0]
    assert seq_len % 512 == 0 and ctx_len % CM_CHUNK == 0 and bsz + 1 <= 8

    cond = jnp.zeros((8, d), F32).at[:bsz].set(c).at[bsz].set(c_ctx)
    mod = _modulation(cond, w_mod, b_mod).reshape(depth, 8, 6, d).transpose(0, 2, 1, 3)

    xl = x.reshape(bsz * seq_len, d)
    xc = ctx.reshape(bsz * ctx_len, d)
    tm_l, tm_c = 512, bsz * ctx_len
    tm_in = 512
    tm_ffn = 1024

    zero_states = (
        (jnp.zeros((bsz, N_HEADS, HEAD_DIM, HEAD_DIM), F32),) * 2,
        (jnp.zeros((bsz, N_HEADS, HEAD_DIM, 2 * HEAD_DIM), F32), jnp.zeros((bsz, 8, LANES), F32)) * 2,
    )

    w_wide_all, w_small_all = _relayout_w_in(w_in)
    w_out_b, w_up_b, w_down_b = w_out.astype(BF16), ffn_up.astype(BF16), ffn_down.astype(BF16)
    k_scale = jnp.concatenate([jnp.ones((1, D_GROUP), F32),
                               jnp.full((1, D_GROUP), HEAD_DIM ** -0.5, F32)], axis=1)
    for l in range(depth):
        need_ctx = l < depth - 1
        mod_l = mod[l][:, :bsz, None, :]
        mod_c = jnp.broadcast_to(mod[l][:, bsz:bsz + 1, None, :], (6, bsz, 1, d))
        wa = jnp.zeros((2, SMALL_COLS, D_GROUP), F32)
        wa = wa.at[0, Z_FWD:Z_FWD + GLA_LOWRANK].set(gla_wa2[l, 0])
        wa = wa.at[1, Z_BWD:Z_BWD + GLA_LOWRANK].set(gla_wa2[l, 1])
        wa_hi = wa.astype(BF16)
        wa_lo = (wa - wa_hi.astype(F32)).astype(BF16)
        wa = jnp.concatenate([wa_hi, wa_hi, wa_lo], axis=1)
        gate_b = jnp.zeros((1, SMALL_COLS), F32).at[0, GATE0:GATE0 + 4 * N_HEADS].set(
            mlstm_gate_b[l].reshape(-1))
        prm = dict(
            wa_f=wa[0], wa_b=wa[1], ba_f=gla_ba[l, 0:1], ba_b=gla_ba[l, 1:2],
            gla_norm=gla_norm[l].reshape(1, D_GROUP),
            ml_conv=mlstm_conv[l] * k_scale, ml_gate_b=gate_b,
            ml_norm=mlstm_norm[l].reshape(1, D_GROUP),
            sc_conv=sc_conv[l], cm_ws=cm_ws[l].astype(BF16),
            cm_bs=jnp.repeat(cm_bs[l].T, HEAD_DIM, axis=1),
            cm_norm=cm_norm[l].reshape(1, D_GROUP),
        )

        pc, psc, *yc_local = _in_projection(xc, norm1[l], mod_c, w_wide_all, w_small_all, l, prm,
                                            ctx_len, tm_c, ctx_len)
        yc_scan, ctx_states = _scan_mixers(pc, psc, bsz, ctx_len, prm, zero_states,
                                           t_scan=ctx_len)
        p_l, ps_l, *yl_local = _in_projection(xl, norm1[l], mod_l, w_wide_all, w_small_all, l, prm,
                                              seq_len, tm_in, GRID_W)
        yl_scan, _ = _scan_mixers(p_l, ps_l, bsz, seq_len, prm, ctx_states,
                                  t_scan=512)
        yl, yc = yl_scan + tuple(yl_local), yc_scan + tuple(yc_local)

        xl = _out_projection(yl, w_out_b, l, xl, mod_l, seq_len, tm_l)
        xl = _conv_ffn(xl, norm2[l], mod_l, w_up_b, ffn_conv, w_down_b, l,
                       final_norm if l == depth - 1 else None, seq_len, GRID_W, tm_ffn)
        if need_ctx:
            xc = _out_projection(yc, w_out_b, l, xc, mod_c, ctx_len, tm_c)
            xc = _conv_ffn(xc, norm2[l], mod_c, w_up_b, ffn_conv, w_down_b, l, None,
                           ctx_len, ctx_len, tm_c)
    return xl.reshape(bsz, seq_len, d)
```

```python
import functools

import numpy as np
import jax
import jax.numpy as jnp
from jax import lax
from jax.experimental import pallas as pl
from jax.experimental.pallas import tpu as pltpu

F32 = jnp.float32
BF16 = jnp.bfloat16

GRID_W = 64
N_MIXERS = 4
HEAD_DIM = 128
N_HEADS = 4
D_GROUP = N_HEADS * HEAD_DIM
GLA_LOWRANK = 16
GLA_GATE_NORM = 16.0
CHUNK = 64
CM_CHUNK = 128
EPS = 1e-6
LOG2_E = 1.4426950408889634
LANES = 128
SMALL_COLS = LANES
Z_FWD, Z_BWD, GATE0 = 0, GLA_LOWRANK, 2 * GLA_LOWRANK
VMEM_LIMIT = 56 * 1024 * 1024
INPROJ_VMEM_LIMIT = 60 * 1024 * 1024
FFN_VMEM_LIMIT = 62 * 1024 * 1024

(C_GQ, C_GK, C_GV, C_GG, C_MQK, C_MV, C_MO, C_SB, C_SC, C_SH, C_CU, C_CV) = (
    0, 1, 2, 3, 4, 6, 7, 8, 9, 10, 11, 12)
N_WIDE_BLOCKS = 13
N_SCAN_BLOCKS = 8
ROW_PAD = 16
T_SCAN = 1024


def _params(sem):
    return pltpu.CompilerParams(dimension_semantics=sem, vmem_limit_bytes=VMEM_LIMIT)


def _sigmoid(x):
    return 1.0 / (1.0 + jnp.exp(-x))


def _log_sigmoid(x):
    return jnp.minimum(x, 0.0) - jnp.log(1.0 + jnp.exp(-jnp.abs(x)))


def _gelu_tanh(x):
    return x * (0.5 * (1.0 + jnp.tanh(0.7978845608028654 * (x + 0.044715 * (x * x * x)))))


def _shift_rows(x, period):
    n = x.shape[0]
    row = lax.broadcasted_iota(jnp.int32, (n, 1), 0) % period
    prev = jnp.where(row == 0, 0.0, pltpu.roll(x, 1, axis=0))
    nxt = jnp.where(row == period - 1, 0.0, pltpu.roll(x, n - 1, axis=0))
    return prev, nxt


def _dwconv3(x, w_ref, period):
    prev, nxt = _shift_rows(x, period)
    return w_ref[0:1, :] * prev + w_ref[1:2, :] * x + w_ref[2:3, :] * nxt


def _mod_kernel(c_ref, w_ref, b_ref, o_ref):
    s = c_ref[...]
    s = (s * _sigmoid(s)).astype(BF16)
    o_ref[...] = jnp.dot(s, w_ref[...].astype(BF16), preferred_element_type=F32) + b_ref[...]


def _modulation(cond, w_mod, b_mod, tn=1024):
    depth, d, n = w_mod.shape
    rows = cond.shape[0]
    return pl.pallas_call(
        _mod_kernel,
        grid=(depth, n // tn),
        in_specs=[
            pl.BlockSpec((rows, d), lambda l, j: (0, 0)),
            pl.BlockSpec((None, d, tn), lambda l, j: (l, 0, j)),
            pl.BlockSpec((None, 1, tn), lambda l, j: (l, 0, j)),
        ],
        out_specs=pl.BlockSpec((None, rows, tn), lambda l, j: (l, 0, j)),
        out_shape=jax.ShapeDtypeStruct((depth, rows, n), F32),
        compiler_params=_params(("parallel", "parallel")),
        name="modulation",
    )(cond, w_mod, b_mod.reshape(depth, 1, n))


def _modulated_norm(x, g, sc, sh):
    r = lax.rsqrt(jnp.mean(x * x, axis=-1, keepdims=True) + EPS)
    return (x * r * g) * (1.0 + sc) + sh


def _inproj_kernel(x_ref, g_ref, sh_ref, sc_ref, w_ref, ws_ref, scw_ref, cws_ref, cbs_ref,
                   cgv_ref, mlw_ref, p_ref, ps_ref, ysc_ref, ycm_ref, *, period):
    hn = _modulated_norm(x_ref[...], g_ref[...], sc_ref[...], sh_ref[...]).astype(BF16)

    def proj(block):
        cols = slice(block * D_GROUP, (block + 1) * D_GROUP)
        return jnp.dot(hn, w_ref[:, cols], preferred_element_type=F32)

    ysc_ref[...] = (proj(C_SB) * _dwconv3(proj(C_SC) * proj(C_SH), scw_ref, period)
                    ).astype(ysc_ref.dtype)

    v = _gelu_tanh(proj(C_CV))
    vn = (v * lax.rsqrt(jnp.mean(v * v, axis=-1, keepdims=True) + EPS) * cgv_ref[...]).astype(BF16)
    u = _gelu_tanh(proj(C_CU))
    for n in range(v.shape[0] // CM_CHUNK):
        rows = slice(n * CM_CHUNK, (n + 1) * CM_CHUNK)
        for g in range(N_HEADS):
            sl = slice(g * HEAD_DIM, (g + 1) * HEAD_DIM)
            sv = jnp.dot(cws_ref[g], vn[rows, sl], preferred_element_type=F32) + cbs_ref[:, sl]
            ycm_ref[rows, sl] = (u[rows, sl] * sv).astype(ycm_ref.dtype)

    for n, block in enumerate((C_MQK, C_MQK + 1)):
        cols = slice(block * D_GROUP, (block + 1) * D_GROUP)
        p_ref[:, cols] = _dwconv3(proj(block), mlw_ref.at[:, n * D_GROUP:(n + 1) * D_GROUP],
                                  period).astype(p_ref.dtype)

    ps_ref[...] = jnp.dot(hn, ws_ref[...], preferred_element_type=F32)
    for block in range(N_SCAN_BLOCKS):
        if block not in (C_MQK, C_MQK + 1):
            p_ref[:, block * D_GROUP:(block + 1) * D_GROUP] = proj(block).astype(p_ref.dtype)


def _row_spec(piece, seq_len, tm, d):
    return pl.BlockSpec((None, None, 1, d), lambda i, *_: (piece, (i * tm) // seq_len, 0, 0))


def _in_projection(x2, gain, mod, w_wide, w_small, layer, prm, seq_len, tm, period):
    m, d = x2.shape
    n = w_wide.shape[2]
    n_scan = N_SCAN_BLOCKS * D_GROUP
    resident = lambda shape: pl.BlockSpec((None,) + shape, lambda i: (layer, 0, 0),
                                          pipeline_mode=pl.Buffered(1))
    const = lambda shape: pl.BlockSpec(shape, lambda i: (0,) * len(shape))
    tile = lambda cols: pl.BlockSpec((tm, cols), lambda i: (i, 0))
    return pl.pallas_call(
        functools.partial(_inproj_kernel, period=period),
        grid=(m // tm,),
        in_specs=[
            tile(d),
            const((1, d)),
            _row_spec(0, seq_len, tm, d),
            _row_spec(1, seq_len, tm, d),
            resident((d, n)),
            resident((d, SMALL_COLS)),
            const((3, D_GROUP)), const((N_HEADS, CM_CHUNK, CM_CHUNK)),
            const((CM_CHUNK, D_GROUP)), const((1, D_GROUP)), const((3, 2 * D_GROUP)),
        ],
        out_specs=[tile(n_scan), tile(SMALL_COLS), tile(D_GROUP), tile(D_GROUP)],
        out_shape=[
            jax.ShapeDtypeStruct((m, n_scan), BF16),
            jax.ShapeDtypeStruct((m, SMALL_COLS), F32),
            jax.ShapeDtypeStruct((m, D_GROUP), BF16),
            jax.ShapeDtypeStruct((m, D_GROUP), BF16),
        ],
        compiler_params=pltpu.CompilerParams(dimension_semantics=("parallel",),
                                             vmem_limit_bytes=INPROJ_VMEM_LIMIT),
        name="in_projection",
    )(x2, gain.reshape(1, d), mod, mod, w_wide, w_small,
      prm["sc_conv"], prm["cm_ws"], prm["cm_bs"], prm["cm_norm"], prm["ml_conv"])


def _tri(reverse):
    r = lax.broadcasted_iota(jnp.int32, (CHUNK, CHUNK), 0)
    c = lax.broadcasted_iota(jnp.int32, (CHUNK, CHUNK), 1)
    return (c >= r) if reverse else (c <= r)


def _tri2(reverse):
    r = lax.broadcasted_iota(jnp.int32, (CHUNK, 2 * CHUNK), 0)
    c = lax.broadcasted_iota(jnp.int32, (CHUNK, 2 * CHUNK), 1) % CHUNK
    return ((c >= r) if reverse else (c <= r)).astype(BF16)


def _split2(x):
    hi = x.astype(BF16)
    return hi, (x - hi.astype(F32)).astype(BF16)


def _chunk_cumsum(tri2, x):
    hi, lo = _split2(x)
    return jnp.dot(tri2, jnp.concatenate([hi, lo], axis=0), preferred_element_type=F32)


def _emit_skewed(order, chunk_stages, head_stages):
    n_stage = len(chunk_stages) + len(head_stages)
    for slot in range(len(order) + n_stage - 1):
        for k, stage in enumerate(chunk_stages):
            if 0 <= slot - k < len(order):
                stage(order[slot - k])
        for h in range(N_HEADS):
            for k, stage in enumerate(head_stages, start=len(chunk_stages)):
                if 0 <= slot - k < len(order):
                    stage(order[slot - k], h)


def _cummax_rows(x, reverse):
    n = x.shape[0]
    row = lax.broadcasted_iota(jnp.int32, (n, 1), 0)
    k = 1
    while k < n:
        if reverse:
            shifted = jnp.where(row >= n - k, -jnp.inf, pltpu.roll(x, n - k, axis=0))
        else:
            shifted = jnp.where(row < k, -jnp.inf, pltpu.roll(x, k, axis=0))
        x = jnp.maximum(x, shifted)
        k *= 2
    return x


def _lane_spread(x, bm_ref):
    hi, lo = _split2(x)
    return jnp.dot(jnp.concatenate([hi, lo], axis=1), bm_ref[...], preferred_element_type=F32)


def _head_norm_gate(o, gain, gate):
    r = lax.rsqrt(jnp.mean(o * o, axis=-1, keepdims=True) + EPS)
    return o * r * gain * gate


def _gla_kernel(*refs, reverse, final, n_chunks):
    if final:
        (q_ref, k_ref, v_ref, z_ref, wa_ref, ba_ref, s0_ref, of_ref, g_ref, gn_ref,
         out_ref, sfin_ref, st_ref, qin_ref, qrel_ref, krel_ref, dec_ref, att_ref, kv_ref,
         acc_ref) = refs
    else:
        (q_ref, k_ref, v_ref, z_ref, wa_ref, ba_ref, s0_ref,
         out_ref, sfin_ref, st_ref, qin_ref, qrel_ref, krel_ref, dec_ref, att_ref,
         kv_ref) = refs
        acc_ref = out_ref
    step = pl.program_id(1)

    @pl.when(step == 0)
    def _():
        st_ref[...] = s0_ref[...]

    mask = _tri(reverse)
    tri2 = _tri2(reverse)
    last = 0 if reverse else CHUNK - 1
    scale = HEAD_DIM ** -0.5
    chunk_rows = [slice(c * CHUNK, (c + 1) * CHUNK) for c in range(n_chunks)]
    head_cols = [slice(h * HEAD_DIM, (h + 1) * HEAD_DIM) for h in range(N_HEADS)]

    z_hi, z_lo = _split2(z_ref[...])
    pre = jnp.dot(jnp.concatenate([z_hi, z_lo, z_hi], axis=1), wa_ref[...],
                  preferred_element_type=F32) + ba_ref[...]
    log_a = _log_sigmoid(pre) * (LOG2_E / GLA_GATE_NORM)

    def decayed_qk(c):
        rows = chunk_rows[c]
        b_cum = _chunk_cumsum(tri2, log_a[rows, :])
        b_last = b_cum[last:last + 1, :]
        q = q_ref[rows, :].astype(F32) * scale
        qin_ref[rows, :] = (q * jnp.exp2(b_cum)).astype(BF16)
        qrel_ref[rows, :] = (q * jnp.exp2(b_cum - b_last)).astype(BF16)
        krel_ref[rows, :] = (k_ref[rows, :].astype(F32) * jnp.exp2(b_last - b_cum)).astype(BF16)
        dec_ref[c] = jnp.exp2(b_last)

    def scores(c, h):
        rows, sl = chunk_rows[c], head_cols[h]
        att = lax.dot_general(qrel_ref[rows, sl], krel_ref[rows, sl],
                              (((1,), (1,)), ((), ())), preferred_element_type=F32)
        att_ref[c * N_HEADS + h] = jnp.where(mask, att, 0.0).astype(BF16)

    def local_output(c, h):
        rows, sl = chunk_rows[c], head_cols[h]
        v = v_ref[rows, sl]
        acc_ref[rows, sl] = jnp.dot(att_ref[c * N_HEADS + h], v, preferred_element_type=F32)
        kv_ref[c * N_HEADS + h] = lax.dot_general(
            v, krel_ref[rows, sl], (((0,), (0,)), ((), ())), preferred_element_type=F32)

    def carry(c, h):
        rows, sl = chunk_rows[c], head_cols[h]
        s_t = st_ref[h]
        o = acc_ref[rows, sl] + lax.dot_general(qin_ref[rows, sl], s_t.astype(BF16),
                                                (((1,), (1,)), ((), ())),
                                                preferred_element_type=F32)
        st_ref[h] = s_t * dec_ref[c][:, sl] + kv_ref[c * N_HEADS + h]
        if final:
            g = g_ref[rows, sl].astype(F32)
            o = _head_norm_gate(o + of_ref[rows, sl], gn_ref[:, sl], g * _sigmoid(g))
        out_ref[rows, sl] = o.astype(out_ref.dtype)

    _emit_skewed(list(reversed(range(n_chunks))) if reverse else list(range(n_chunks)),
                 [decayed_qk], [scores, local_output, carry])

    @pl.when(step == pl.num_programs(1) - 1)
    def _():
        sfin_ref[...] = st_ref[...]


def _scan_block_spec(t, cols, col_block, n_blocks, reverse):
    if reverse:
        return pl.BlockSpec((None, t, cols), lambda b, i: (b, n_blocks - 1 - i, col_block))
    return pl.BlockSpec((None, t, cols), lambda b, i: (b, i, col_block))


def _gla_scan(p3, ps3, wa_pad, ba, s0, gn, o_fwd, *, reverse, t):
    bsz, seq_len, _ = p3.shape
    n_blocks = seq_len // t
    n_chunks = t // CHUNK
    final = o_fwd is not None
    spec = functools.partial(_scan_block_spec, n_blocks=n_blocks, reverse=reverse)
    const2 = lambda shape: pl.BlockSpec(shape, lambda b, i: (0, 0))
    state_spec = pl.BlockSpec((None, N_HEADS, HEAD_DIM, HEAD_DIM), lambda b, i: (b, 0, 0, 0))
    in_specs = [
        spec(t, D_GROUP, C_GQ), spec(t, D_GROUP, C_GK), spec(t, D_GROUP, C_GV),
        spec(t, SMALL_COLS, 0),
        const2((3 * SMALL_COLS, D_GROUP)), const2((1, D_GROUP)),
        state_spec,
    ]
    args = [p3, p3, p3, ps3, wa_pad, ba, s0]
    if final:
        in_specs += [spec(t, D_GROUP, 0), spec(t, D_GROUP, C_GG), const2((1, D_GROUP))]
        args += [o_fwd, p3, gn]
    return pl.pallas_call(
        functools.partial(_gla_kernel, reverse=reverse, final=final, n_chunks=n_chunks),
        grid=(bsz, n_blocks),
        in_specs=in_specs,
        out_specs=[spec(t, D_GROUP, 0), state_spec],
        out_shape=[
            jax.ShapeDtypeStruct((bsz, seq_len, D_GROUP), BF16 if final else F32),
            jax.ShapeDtypeStruct((bsz, N_HEADS, HEAD_DIM, HEAD_DIM), F32),
        ],
        scratch_shapes=[
            pltpu.VMEM((N_HEADS, HEAD_DIM, HEAD_DIM), F32),
            pltpu.VMEM((t, D_GROUP), BF16), pltpu.VMEM((t, D_GROUP), BF16),
            pltpu.VMEM((t, D_GROUP), BF16),
            pltpu.VMEM((n_chunks, 1, D_GROUP), F32),
            pltpu.VMEM((n_chunks * N_HEADS, CHUNK, CHUNK), BF16),
            pltpu.VMEM((n_chunks * N_HEADS, HEAD_DIM, HEAD_DIM), F32),
        ] + ([pltpu.VMEM((t, D_GROUP), F32)] if final else []),
        compiler_params=_params(("parallel", "arbitrary")),
        name="gla_bwd" if reverse else "gla_fwd",
    )(*args)


def _mlstm_kernel(*refs, reverse, final, n_chunks, lane0):
    if final:
        (qkc_ref, v_ref, gt_ref, gb_ref, bm_ref, c0_ref, m0_ref, hf_ref, og_ref, gn_ref,
         out_ref, cfin_ref, mfin_ref, c_ref, m_ref, fcol_ref, jcol_ref, jt_ref,
         rowb_ref, ub_ref, mb_ref, p_ref, s_ref, wi_ref, kwt_ref, kv_ref, r_ref) = refs
    else:
        (qkc_ref, v_ref, gt_ref, gb_ref, bm_ref, c0_ref, m0_ref,
         out_ref, cfin_ref, mfin_ref, c_ref, m_ref, fcol_ref, jcol_ref, jt_ref,
         rowb_ref, ub_ref, mb_ref, p_ref, s_ref, wi_ref, kwt_ref, kv_ref, r_ref) = refs
    step = pl.program_id(1)

    @pl.when(step == 0)
    def _():
        c_ref[...] = c0_ref[...]
        m_ref[...] = m0_ref[...]

    mask = _tri(reverse)
    tri2 = _tri2(reverse)
    last = 0 if reverse else CHUNK - 1
    ones_blk = jnp.ones((CHUNK, HEAD_DIM), BF16)
    chunk_rows = [slice(c * CHUNK, (c + 1) * CHUNK) for c in range(n_chunks)]
    head_cols = [slice(h * HEAD_DIM, (h + 1) * HEAD_DIM) for h in range(N_HEADS)]
    key_cols = [slice(D_GROUP + h * HEAD_DIM, D_GROUP + (h + 1) * HEAD_DIM) for h in range(N_HEADS)]
    f_lane = [lane0 + N_HEADS + h for h in range(N_HEADS)]
    scan_order = list(reversed(range(n_chunks))) if reverse else list(range(n_chunks))

    gates = gt_ref[...] + gb_ref[...]
    log_f = _log_sigmoid(gates)
    i_al = pltpu.roll(gates, N_HEADS, axis=1)
    j_max, f_tot = [], []
    for c, rows in enumerate(chunk_rows):
        f_col = _chunk_cumsum(tri2, log_f[rows, :])
        j_col = i_al[rows, :] - f_col
        fcol_ref[rows, :] = f_col
        jcol_ref[rows, :] = j_col
        jt_ref[c] = j_col.T
        j_max.append(jnp.max(j_col, axis=0, keepdims=True))
        f_tot.append(f_col[last:last + 1, :])

    m_run = m_ref[0:1, :]
    m_prev, u_last = [None] * n_chunks, [None] * n_chunks
    for c in scan_order:
        m_prev[c] = m_run
        u_last[c] = jnp.maximum(m_run, j_max[c])
        m_run = f_tot[c] + u_last[c]
    m_ref[0:1, :] = m_run
    pad_rows = [jnp.zeros((ROW_PAD - n_chunks, LANES), F32)] if n_chunks < ROW_PAD else []
    rowb_ref[0] = _lane_spread(jnp.concatenate(m_prev + pad_rows, axis=0), bm_ref)
    rowb_ref[1] = _lane_spread(jnp.concatenate(u_last + pad_rows, axis=0), bm_ref)

    for c, rows in enumerate(chunk_rows):
        u_col = jnp.maximum(_cummax_rows(jcol_ref[rows, :], reverse), m_prev[c])
        ub_ref[rows, :] = _lane_spread(u_col, bm_ref)
        mb_ref[rows, :] = _lane_spread(fcol_ref[rows, :] + u_col, bm_ref)

    def decay_weights(c, h):
        rows, sl, lf, idx = chunk_rows[c], head_cols[h], f_lane[h], c * N_HEADS + h
        u = ub_ref[rows, sl]
        j_row = jt_ref[c][lf:lf + 1, :]
        a = jnp.where(mask, j_row, -jnp.inf)
        p_ref[idx] = jnp.exp(a - u[:, :CHUNK])
        wi_ref[idx] = jnp.exp(rowb_ref[0][c:c + 1, sl] - u)
        w_k = jnp.exp(j_row - rowb_ref[1][c:c + 1, sl][:, :CHUNK])
        kwt_ref[idx] = (qkc_ref[rows, key_cols[h]].astype(F32).T * w_k).astype(BF16)

    def scores(c, h):
        rows, idx = chunk_rows[c], c * N_HEADS + h
        s = lax.dot_general(qkc_ref[rows, head_cols[h]], qkc_ref[rows, key_cols[h]],
                            (((1,), (1,)), ((), ())), preferred_element_type=F32)
        s_ref[idx] = (s * p_ref[idx]).astype(BF16)

    def local_output(c, h):
        rows, idx = chunk_rows[c], c * N_HEADS + h
        v_aug = jnp.concatenate([v_ref[rows, head_cols[h]], ones_blk], axis=1)
        r_ref[rows, 2 * h * HEAD_DIM:2 * (h + 1) * HEAD_DIM] = jnp.dot(
            s_ref[idx], v_aug, preferred_element_type=F32)
        kv_ref[idx] = jnp.dot(kwt_ref[idx], v_aug, preferred_element_type=F32)

    def carry(c, h):
        rows, sl, idx = chunk_rows[c], head_cols[h], c * N_HEADS + h
        decay = jnp.exp(rowb_ref[0][c:c + 1, sl] - rowb_ref[1][c:c + 1, sl])
        c_aug = c_ref[h]
        wi = wi_ref[idx]
        r = jnp.concatenate([wi, wi], axis=1) * jnp.dot(
            qkc_ref[rows, sl], c_aug.astype(BF16),
            preferred_element_type=F32) + r_ref[rows, 2 * h * HEAD_DIM:2 * (h + 1) * HEAD_DIM]
        den = jnp.maximum(jnp.abs(r[:, HEAD_DIM:]), jnp.exp(-mb_ref[rows, sl]))
        hid = r[:, :HEAD_DIM] * (1.0 / den)
        c_ref[h] = c_aug * jnp.concatenate([decay, decay], axis=1) + kv_ref[idx]
        if final:
            hid = _head_norm_gate(hid + hf_ref[rows, sl], gn_ref[:, sl],
                                  _sigmoid(og_ref[rows, sl].astype(F32)))
        out_ref[rows, sl] = hid.astype(out_ref.dtype)

    for c in range(n_chunks):
        for h in range(N_HEADS):
            decay_weights(c, h)
    for c in range(n_chunks):
        for h in range(N_HEADS):
            scores(c, h)
    for c in range(n_chunks):
        for h in range(N_HEADS):
            local_output(c, h)
    for c in scan_order:
        for h in range(N_HEADS):
            carry(c, h)

    @pl.when(step == pl.num_programs(1) - 1)
    def _():
        cfin_ref[...] = c_ref[...]
        mfin_ref[...] = m_ref[...]


def _mlstm_scan(p3, ps3, gate_b, c0, m0, gn, h_fwd, *, reverse, t):
    bsz, seq_len, _ = p3.shape
    n_blocks = seq_len // t
    n_chunks = t // CHUNK
    final = h_fwd is not None
    spec = functools.partial(_scan_block_spec, n_blocks=n_blocks, reverse=reverse)
    const2 = lambda shape: pl.BlockSpec(shape, lambda b, i: (0, 0))
    c_spec = pl.BlockSpec((None, N_HEADS, HEAD_DIM, 2 * HEAD_DIM), lambda b, i: (b, 0, 0, 0))
    m_spec = pl.BlockSpec((None, 8, LANES), lambda b, i: (b, 0, 0))
    out_specs = [spec(t, D_GROUP, 0), c_spec, m_spec]
    out_shape = [
        jax.ShapeDtypeStruct((bsz, seq_len, D_GROUP), BF16 if final else F32),
        jax.ShapeDtypeStruct((bsz, N_HEADS, HEAD_DIM, 2 * HEAD_DIM), F32),
        jax.ShapeDtypeStruct((bsz, 8, LANES), F32),
    ]
    lane0 = GATE0 + (2 * N_HEADS if reverse else 0)
    lane = np.arange(2 * SMALL_COLS)[:, None] % SMALL_COLS
    head = np.arange(D_GROUP)[None, :] // HEAD_DIM
    spread = jnp.asarray(lane == lane0 + N_HEADS + head, BF16)
    in_specs = [
        spec(t, 2 * D_GROUP, C_MQK // 2), spec(t, D_GROUP, C_MV), spec(t, SMALL_COLS, 0),
        const2((1, SMALL_COLS)), const2((2 * SMALL_COLS, D_GROUP)), c_spec, m_spec,
    ]
    args = [p3, p3, ps3, gate_b, spread, c0, m0]
    if final:
        in_specs += [spec(t, D_GROUP, 0), spec(t, D_GROUP, C_MO), const2((1, D_GROUP))]
        args += [h_fwd, p3, gn]
    units = n_chunks * N_HEADS
    return pl.pallas_call(
        functools.partial(_mlstm_kernel, reverse=reverse, final=final, n_chunks=n_chunks,
                          lane0=lane0),
        grid=(bsz, n_blocks),
        in_specs=in_specs,
        out_specs=out_specs,
        out_shape=out_shape,
        scratch_shapes=[
            pltpu.VMEM((N_HEADS, HEAD_DIM, 2 * HEAD_DIM), F32),
            pltpu.VMEM((8, LANES), F32),
            pltpu.VMEM((t, SMALL_COLS), F32),
            pltpu.VMEM((t, SMALL_COLS), F32),
            pltpu.VMEM((n_chunks, SMALL_COLS, CHUNK), F32),
            pltpu.VMEM((2, ROW_PAD, D_GROUP), F32),
            pltpu.VMEM((t, D_GROUP), F32),
            pltpu.VMEM((t, D_GROUP), F32),
            pltpu.VMEM((units, CHUNK, CHUNK), F32),
            pltpu.VMEM((units, CHUNK, CHUNK), BF16),
            pltpu.VMEM((units, CHUNK, LANES), F32),
            pltpu.VMEM((units, HEAD_DIM, CHUNK), BF16),
            pltpu.VMEM((units, HEAD_DIM, 2 * HEAD_DIM), F32),
            pltpu.VMEM((t, 2 * D_GROUP), F32),
        ],
        compiler_params=_params(("parallel", "arbitrary")),
        name="mlstm_bwd" if reverse else "mlstm_fwd",
    )(*args)


def _outproj_kernel(y0_ref, y1_ref, y2_ref, y3_ref, w_ref, x_ref, g_ref, o_ref):
    acc = jnp.dot(y0_ref[...], w_ref[0:D_GROUP, :], preferred_element_type=F32)
    for n, y_ref in enumerate((y1_ref, y2_ref, y3_ref), start=1):
        acc = acc + jnp.dot(y_ref[...], w_ref[n * D_GROUP:(n + 1) * D_GROUP, :],
                            preferred_element_type=F32)
    o_ref[...] = x_ref[...] + g_ref[...] * acc


def _out_projection(ys, w_out, layer, x2, mod, seq_len, tm):
    m, d = x2.shape
    y_spec = pl.BlockSpec((tm, D_GROUP), lambda i: (i, 0))
    return pl.pallas_call(
        _outproj_kernel,
        grid=(m // tm,),
        in_specs=[y_spec] * N_MIXERS + [
            pl.BlockSpec((None, d, d), lambda i: (layer, 0, 0), pipeline_mode=pl.Buffered(1)),
            pl.BlockSpec((tm, d), lambda i: (i, 0)),
            _row_spec(2, seq_len, tm, d),
        ],
        out_specs=pl.BlockSpec((tm, d), lambda i: (i, 0)),
        out_shape=jax.ShapeDtypeStruct((m, d), F32),
        compiler_params=_params(("parallel",)),
        name="out_projection",
    )(*ys, w_out, x2, mod)


def _ffn_kernel(*refs, period, final):
    if final:
        (x_ref, g_ref, sh_ref, sc_ref, gate_ref, wa_ref, wv_ref, cw_ref, wd_ref, fn_ref,
         o_ref, hn_ref) = refs
    else:
        (x_ref, g_ref, sh_ref, sc_ref, gate_ref, wa_ref, wv_ref, cw_ref, wd_ref,
         o_ref, hn_ref) = refs
    j = pl.program_id(1)
    last = pl.num_programs(1) - 1

    @pl.when(j == 0)
    def _():
        hn_ref[...] = _modulated_norm(x_ref[...], g_ref[...], sc_ref[...], sh_ref[...]).astype(BF16)

    def tile_contribution():
        hn = hn_ref[...]
        a = _dwconv3(jnp.dot(hn, wa_ref[...], preferred_element_type=F32), cw_ref, period)
        v = jnp.dot(hn, wv_ref[...], preferred_element_type=F32)
        hid = (a * _sigmoid(a) * v).astype(BF16)
        return jnp.dot(hid, wd_ref[...], preferred_element_type=F32)

    @pl.when(j == 0)
    def _():
        o_ref[...] = tile_contribution()

    @pl.when(jnp.logical_and(j > 0, j < last))
    def _():
        o_ref[...] += tile_contribution()

    @pl.when(j == last)
    def _():
        y = x_ref[...] + gate_ref[...] * (o_ref[...] + tile_contribution())
        if final:
            y = y * lax.rsqrt(jnp.mean(y * y, axis=-1, keepdims=True) + EPS) * fn_ref[...]
        o_ref[...] = y


def _conv_ffn(x2, gain, mod, w_up, w_conv, w_down, layer, final_gain, seq_len, period, tm,
              tf=512):
    m, d = x2.shape
    d_ff = w_down.shape[1]
    n_f = d_ff // tf
    final = final_gain is not None
    in_specs = [
        pl.BlockSpec((tm, d), lambda i, j: (i, 0)),
        pl.BlockSpec((1, d), lambda i, j: (0, 0)),
        _row_spec(3, seq_len, tm, d), _row_spec(4, seq_len, tm, d), _row_spec(5, seq_len, tm, d),
        pl.BlockSpec((None, d, tf), lambda i, j: (layer, 0, j)),
        pl.BlockSpec((None, d, tf), lambda i, j: (layer, 0, j + n_f)),
        pl.BlockSpec((None, 3, tf), lambda i, j: (layer, 0, j)),
        pl.BlockSpec((None, tf, d), lambda i, j: (layer, j, 0)),
    ]
    args = [x2, gain.reshape(1, d), mod, mod, mod, w_up, w_up, w_conv, w_down]
    if final:
        in_specs.append(pl.BlockSpec((1, d), lambda i, j: (0, 0)))
        args.append(final_gain.reshape(1, d))
    return pl.pallas_call(
        functools.partial(_ffn_kernel, period=period, final=final),
        grid=(m // tm, n_f),
        in_specs=in_specs,
        out_specs=pl.BlockSpec((tm, d), lambda i, j: (i, 0)),
        out_shape=jax.ShapeDtypeStruct((m, d), F32),
        scratch_shapes=[pltpu.VMEM((tm, d), BF16)],
        compiler_params=pltpu.CompilerParams(dimension_semantics=("parallel", "arbitrary"),
                                             vmem_limit_bytes=FFN_VMEM_LIMIT),
        name="conv_ffn",
    )(*args)


def _relayout_w_in(w_in):
    depth, d, _ = w_in.shape
    gla = 4 * D_GROUP + 2 * GLA_LOWRANK
    ml = 4 * D_GROUP + 4 * N_HEADS
    wide = jnp.concatenate([w_in[..., :4 * D_GROUP], w_in[..., gla:gla + 4 * D_GROUP],
                            w_in[..., gla + ml:]], axis=-1).astype(BF16)
    small = jnp.concatenate([
        w_in[..., 4 * D_GROUP:gla], w_in[..., gla + 4 * D_GROUP:gla + ml],
        jnp.zeros((depth, d, SMALL_COLS - 2 * GLA_LOWRANK - 4 * N_HEADS), w_in.dtype)],
        axis=-1).astype(BF16)
    return wide, small


def _scan_mixers(p2, ps2, bsz, seq_len, prm, states, *, t_scan):
    p3 = p2.reshape(bsz, seq_len, -1)
    ps3 = ps2.reshape(bsz, seq_len, SMALL_COLS)
    (gs_f, gs_b), (mc_f, mm_f, mc_b, mm_b) = states
    o_f, gs_f = _gla_scan(p3, ps3, prm["wa_f"], prm["ba_f"], gs_f, None, None,
                          reverse=False, t=t_scan)
    y_gla, gs_b = _gla_scan(p3, ps3, prm["wa_b"], prm["ba_b"], gs_b, prm["gla_norm"], o_f,
                            reverse=True, t=t_scan)
    h_f, mc_f, mm_f = _mlstm_scan(p3, ps3, prm["ml_gate_b"], mc_f, mm_f, None, None,
                                  reverse=False, t=t_scan)
    y_ml, mc_b, mm_b = _mlstm_scan(p3, ps3, prm["ml_gate_b"], mc_b, mm_b, prm["ml_norm"], h_f,
                                   reverse=True, t=t_scan)
    new_states = ((gs_f, gs_b), (mc_f, mm_f, mc_b, mm_b))
    m = bsz * seq_len
    return (y_gla.reshape(m, D_GROUP), y_ml.reshape(m, D_GROUP)), new_states


def kernel(x, c, ctx, c_ctx, w_mod, b_mod, norm1, norm2, w_in, w_out, gla_wa2, gla_ba, gla_norm,
           mlstm_conv, mlstm_gate_b, mlstm_norm, sc_conv, cm_ws, cm_bs, cm_norm,
           ffn_up, ffn_conv, ffn_down, final_norm):
    bsz, seq_len, d = x.shape
    ctx_len = ctx.shape[1]
    depth = w_mod.shape[0]
    assert seq_len % T_SCAN == 0 and ctx_len % CM_CHUNK == 0 and bsz + 1 <= 8

    cond = jnp.zeros((8, d), F32).at[:bsz].set(c).at[bsz].set(c_ctx)
    mod = _modulation(cond, w_mod, b_mod).reshape(depth, 8, 6, d).transpose(0, 2, 1, 3)

    xl = x.reshape(bsz * seq_len, d)
    xc = ctx.reshape(bsz * ctx_len, d)
    tm_l, tm_c = 512, bsz * ctx_len
    tm_in = 512
    tm_ffn = 1024

    zero_states = (
        (jnp.zeros((bsz, N_HEADS, HEAD_DIM, HEAD_DIM), F32),) * 2,
        (jnp.zeros((bsz, N_HEADS, HEAD_DIM, 2 * HEAD_DIM), F32), jnp.zeros((bsz, 8, LANES), F32)) * 2,
    )

    w_wide_all, w_small_all = _relayout_w_in(w_in)
    w_out_b, w_up_b, w_down_b = w_out.astype(BF16), ffn_up.astype(BF16), ffn_down.astype(BF16)
    k_scale = jnp.concatenate([jnp.ones((1, D_GROUP), F32),
                               jnp.full((1, D_GROUP), HEAD_DIM ** -0.5, F32)], axis=1)
    for l in range(depth):
        need_ctx = l < depth - 1
        mod_l = mod[l][:, :bsz, None, :]
        mod_c = jnp.broadcast_to(mod[l][:, bsz:bsz + 1, None, :], (6, bsz, 1, d))
        wa = jnp.zeros((2, SMALL_COLS, D_GROUP), F32)
        wa = wa.at[0, Z_FWD:Z_FWD + GLA_LOWRANK].set(gla_wa2[l, 0])
        wa = wa.at[1, Z_BWD:Z_BWD + GLA_LOWRANK].set(gla_wa2[l, 1])
        wa_hi = wa.astype(BF16)
        wa_lo = (wa - wa_hi.astype(F32)).astype(BF16)
        wa = jnp.concatenate([wa_hi, wa_hi, wa_lo], axis=1)
        gate_b = jnp.zeros((1, SMALL_COLS), F32).at[0, GATE0:GATE0 + 4 * N_HEADS].set(
            mlstm_gate_b[l].reshape(-1))
        prm = dict(
            wa_f=wa[0], wa_b=wa[1], ba_f=gla_ba[l, 0:1], ba_b=gla_ba[l, 1:2],
            gla_norm=gla_norm[l].reshape(1, D_GROUP),
            ml_conv=mlstm_conv[l] * k_scale, ml_gate_b=gate_b,
            ml_norm=mlstm_norm[l].reshape(1, D_GROUP),
            sc_conv=sc_conv[l], cm_ws=cm_ws[l].astype(BF16),
            cm_bs=jnp.repeat(cm_bs[l].T, HEAD_DIM, axis=1),
            cm_norm=cm_norm[l].reshape(1, D_GROUP),
        )

        pc, psc, *yc_local = _in_projection(xc, norm1[l], mod_c, w_wide_all, w_small_all, l, prm,
                                            ctx_len, tm_c, ctx_len)
        yc_scan, ctx_states = _scan_mixers(pc, psc, bsz, ctx_len, prm, zero_states,
                                           t_scan=ctx_len)
        p_l, ps_l, *yl_local = _in_projection(xl, norm1[l], mod_l, w_wide_all, w_small_all, l, prm,
                                              seq_len, tm_in, GRID_W)
        yl_scan, _ = _scan_mixers(p_l, ps_l, bsz, seq_len, prm, ctx_states,
                                  t_scan=T_SCAN)
        yl, yc = yl_scan + tuple(yl_local), yc_scan + tuple(yc_local)

        xl = _out_projection(yl, w_out_b, l, xl, mod_l, seq_len, tm_l)
        xl = _conv_ffn(xl, norm2[l], mod_l, w_up_b, ffn_conv, w_down_b, l,
                       final_norm if l == depth - 1 else None, seq_len, GRID_W, tm_ffn)
        if need_ctx:
            xc = _out_projection(yc, w_out_b, l, xc, mod_c, ctx_len, tm_c)
            xc = _conv_ffn(xc, norm2[l], mod_c, w_up_b, ffn_conv, w_down_b, l, None,
                           ctx_len, ctx_len, tm_c)
    return xl.reshape(bsz, seq_len, d)
```

```python
import functools

import numpy as np
import jax
import jax.numpy as jnp
from jax import lax
from jax.experimental import pallas as pl
from jax.experimental.pallas import tpu as pltpu

F32 = jnp.float32
BF16 = jnp.bfloat16

GRID_W = 64
N_MIXERS = 4
HEAD_DIM = 128
N_HEADS = 4
D_GROUP = N_HEADS * HEAD_DIM
GLA_LOWRANK = 16
GLA_GATE_NORM = 16.0
CHUNK = 64
CM_CHUNK = 128
EPS = 1e-6
LOG2_E = 1.4426950408889634
LANES = 128
SMALL_COLS = LANES
Z_FWD, Z_BWD, GATE0 = 0, GLA_LOWRANK, 2 * GLA_LOWRANK
VMEM_LIMIT = 56 * 1024 * 1024
INPROJ_VMEM_LIMIT = 60 * 1024 * 1024
FFN_VMEM_LIMIT = 62 * 1024 * 1024

(C_GQ, C_GK, C_GV, C_GG, C_MQK, C_MV, C_MO, C_SB, C_SC, C_SH, C_CU, C_CV) = (
    0, 1, 2, 3, 4, 6, 7, 8, 9, 10, 11, 12)
N_WIDE_BLOCKS = 13
N_SCAN_BLOCKS = 8
ROW_PAD = 16
T_SCAN = 1024


def _params(sem):
    return pltpu.CompilerParams(dimension_semantics=sem, vmem_limit_bytes=VMEM_LIMIT)


def _sigmoid(x):
    return 1.0 / (1.0 + jnp.exp(-x))


def _log_sigmoid(x):
    return jnp.minimum(x, 0.0) - jnp.log(1.0 + jnp.exp(-jnp.abs(x)))


def _gelu_tanh(x):
    return x * (0.5 * (1.0 + jnp.tanh(0.7978845608028654 * (x + 0.044715 * (x * x * x)))))


def _shift_rows(x, period):
    n = x.shape[0]
    row = lax.broadcasted_iota(jnp.int32, (n, 1), 0) % period
    prev = jnp.where(row == 0, 0.0, pltpu.roll(x, 1, axis=0))
    nxt = jnp.where(row == period - 1, 0.0, pltpu.roll(x, n - 1, axis=0))
    return prev, nxt


def _dwconv3(x, w_ref, period):
    prev, nxt = _shift_rows(x, period)
    return w_ref[0:1, :] * prev + w_ref[1:2, :] * x + w_ref[2:3, :] * nxt


def _mod_kernel(c_ref, w_ref, b_ref, o_ref):
    s = c_ref[...]
    s = (s * _sigmoid(s)).astype(BF16)
    o_ref[...] = jnp.dot(s, w_ref[...].astype(BF16), preferred_element_type=F32) + b_ref[...]


def _modulation(cond, w_mod, b_mod, tn=1024):
    depth, d, n = w_mod.shape
    rows = cond.shape[0]
    return pl.pallas_call(
        _mod_kernel,
        grid=(depth, n // tn),
        in_specs=[
            pl.BlockSpec((rows, d), lambda l, j: (0, 0)),
            pl.BlockSpec((None, d, tn), lambda l, j: (l, 0, j)),
            pl.BlockSpec((None, 1, tn), lambda l, j: (l, 0, j)),
        ],
        out_specs=pl.BlockSpec((None, rows, tn), lambda l, j: (l, 0, j)),
        out_shape=jax.ShapeDtypeStruct((depth, rows, n), F32),
        compiler_params=_params(("parallel", "parallel")),
        name="modulation",
    )(cond, w_mod, b_mod.reshape(depth, 1, n))


def _modulated_norm(x, g, sc, sh):
    r = lax.rsqrt(jnp.mean(x * x, axis=-1, keepdims=True) + EPS)
    return (x * r * g) * (1.0 + sc) + sh


def _inproj_kernel(x_ref, g_ref, sh_ref, sc_ref, w_ref, ws_ref, scw_ref, cws_ref, cbs_ref,
                   cgv_ref, mlw_ref, p_ref, ps_ref, ysc_ref, ycm_ref, *, period):
    hn = _modulated_norm(x_ref[...], g_ref[...], sc_ref[...], sh_ref[...]).astype(BF16)

    def proj(block):
        cols = slice(block * D_GROUP, (block + 1) * D_GROUP)
        return jnp.dot(hn, w_ref[:, cols], preferred_element_type=F32)

    ysc_ref[...] = (proj(C_SB) * _dwconv3(proj(C_SC) * proj(C_SH), scw_ref, period)
                    ).astype(ysc_ref.dtype)

    v = _gelu_tanh(proj(C_CV))
    vn = (v * lax.rsqrt(jnp.mean(v * v, axis=-1, keepdims=True) + EPS) * cgv_ref[...]).astype(BF16)
    u = _gelu_tanh(proj(C_CU))
    for n in range(v.shape[0] // CM_CHUNK):
        rows = slice(n * CM_CHUNK, (n + 1) * CM_CHUNK)
        for g in range(N_HEADS):
            sl = slice(g * HEAD_DIM, (g + 1) * HEAD_DIM)
            sv = jnp.dot(cws_ref[g], vn[rows, sl], preferred_element_type=F32) + cbs_ref[:, sl]
            ycm_ref[rows, sl] = (u[rows, sl] * sv).astype(ycm_ref.dtype)

    for n, block in enumerate((C_MQK, C_MQK + 1)):
        cols = slice(block * D_GROUP, (block + 1) * D_GROUP)
        p_ref[:, cols] = _dwconv3(proj(block), mlw_ref.at[:, n * D_GROUP:(n + 1) * D_GROUP],
                                  period).astype(p_ref.dtype)

    ps_ref[...] = jnp.dot(hn, ws_ref[...], preferred_element_type=F32)
    for block in range(N_SCAN_BLOCKS):
        if block not in (C_MQK, C_MQK + 1):
            p_ref[:, block * D_GROUP:(block + 1) * D_GROUP] = proj(block).astype(p_ref.dtype)


def _row_spec(piece, seq_len, tm, d):
    return pl.BlockSpec((None, None, 1, d), lambda i, *_: (piece, (i * tm) // seq_len, 0, 0))


def _in_projection(x2, gain, mod, w_wide, w_small, layer, prm, seq_len, tm, period):
    m, d = x2.shape
    n = w_wide.shape[2]
    n_scan = N_SCAN_BLOCKS * D_GROUP
    resident = lambda shape: pl.BlockSpec((None,) + shape, lambda i: (layer, 0, 0),
                                          pipeline_mode=pl.Buffered(1))
    const = lambda shape: pl.BlockSpec(shape, lambda i: (0,) * len(shape))
    tile = lambda cols: pl.BlockSpec((tm, cols), lambda i: (i, 0))
    return pl.pallas_call(
        functools.partial(_inproj_kernel, period=period),
        grid=(m // tm,),
        in_specs=[
            tile(d),
            const((1, d)),
            _row_spec(0, seq_len, tm, d),
            _row_spec(1, seq_len, tm, d),
            resident((d, n)),
            resident((d, SMALL_COLS)),
            const((3, D_GROUP)), const((N_HEADS, CM_CHUNK, CM_CHUNK)),
            const((CM_CHUNK, D_GROUP)), const((1, D_GROUP)), const((3, 2 * D_GROUP)),
        ],
        out_specs=[tile(n_scan), tile(SMALL_COLS), tile(D_GROUP), tile(D_GROUP)],
        out_shape=[
            jax.ShapeDtypeStruct((m, n_scan), BF16),
            jax.ShapeDtypeStruct((m, SMALL_COLS), F32),
            jax.ShapeDtypeStruct((m, D_GROUP), BF16),
            jax.ShapeDtypeStruct((m, D_GROUP), BF16),
        ],
        compiler_params=pltpu.CompilerParams(dimension_semantics=("parallel",),
                                             vmem_limit_bytes=INPROJ_VMEM_LIMIT),
        name="in_projection",
    )(x2, gain.reshape(1, d), mod, mod, w_wide, w_small,
      prm["sc_conv"], prm["cm_ws"], prm["cm_bs"], prm["cm_norm"], prm["ml_conv"])


def _tri(reverse):
    r = lax.broadcasted_iota(jnp.int32, (CHUNK, CHUNK), 0)
    c = lax.broadcasted_iota(jnp.int32, (CHUNK, CHUNK), 1)
    return (c >= r) if reverse else (c <= r)


def _tri2(reverse):
    r = lax.broadcasted_iota(jnp.int32, (CHUNK, 2 * CHUNK), 0)
    c = lax.broadcasted_iota(jnp.int32, (CHUNK, 2 * CHUNK), 1) % CHUNK
    return ((c >= r) if reverse else (c <= r)).astype(BF16)


def _split2(x):
    hi = x.astype(BF16)
    return hi, (x - hi.astype(F32)).astype(BF16)


def _chunk_cumsum(tri2, x):
    hi, lo = _split2(x)
    return jnp.dot(tri2, jnp.concatenate([hi, lo], axis=0), preferred_element_type=F32)


def _emit_skewed(order, chunk_stages, head_stages):
    n_stage = len(chunk_stages) + len(head_stages)
    for slot in range(len(order) + n_stage - 1):
        for k, stage in enumerate(chunk_stages):
            if 0 <= slot - k < len(order):
                stage(order[slot - k])
        for h in range(N_HEADS):
            for k, stage in enumerate(head_stages, start=len(chunk_stages)):
                if 0 <= slot - k < len(order):
                    stage(order[slot - k], h)


def _cummax_rows(x, reverse):
    n = x.shape[0]
    row = lax.broadcasted_iota(jnp.int32, (n, 1), 0)
    k = 1
    while k < n:
        if reverse:
            shifted = jnp.where(row >= n - k, -jnp.inf, pltpu.roll(x, n - k, axis=0))
        else:
            shifted = jnp.where(row < k, -jnp.inf, pltpu.roll(x, k, axis=0))
        x = jnp.maximum(x, shifted)
        k *= 2
    return x


def _lane_spread(x, bm_ref):
    hi, lo = _split2(x)
    return jnp.dot(jnp.concatenate([hi, lo], axis=1), bm_ref[...], preferred_element_type=F32)


def _head_norm_gate(o, gain, gate):
    r = lax.rsqrt(jnp.mean(o * o, axis=-1, keepdims=True) + EPS)
    return o * r * gain * gate


def _gla_kernel(*refs, reverse, final, n_chunks):
    if final:
        (q_ref, k_ref, v_ref, z_ref, wa_ref, ba_ref, s0_ref, of_ref, g_ref, gn_ref,
         out_ref, sfin_ref, st_ref, qin_ref, qrel_ref, krel_ref, dec_ref, att_ref, kv_ref,
         acc_ref) = refs
    else:
        (q_ref, k_ref, v_ref, z_ref, wa_ref, ba_ref, s0_ref,
         out_ref, sfin_ref, st_ref, qin_ref, qrel_ref, krel_ref, dec_ref, att_ref,
         kv_ref) = refs
        acc_ref = out_ref
    step = pl.program_id(1)

    @pl.when(step == 0)
    def _():
        st_ref[...] = s0_ref[...]

    mask = _tri(reverse)
    tri2 = _tri2(reverse)
    last = 0 if reverse else CHUNK - 1
    scale = HEAD_DIM ** -0.5
    chunk_rows = [slice(c * CHUNK, (c + 1) * CHUNK) for c in range(n_chunks)]
    head_cols = [slice(h * HEAD_DIM, (h + 1) * HEAD_DIM) for h in range(N_HEADS)]

    z_hi, z_lo = _split2(z_ref[...])
    pre = jnp.dot(jnp.concatenate([z_hi, z_lo, z_hi], axis=1), wa_ref[...],
                  preferred_element_type=F32) + ba_ref[...]
    log_a = _log_sigmoid(pre) * (LOG2_E / GLA_GATE_NORM)

    def decayed_qk(c):
        rows = chunk_rows[c]
        b_cum = _chunk_cumsum(tri2, log_a[rows, :])
        b_last = b_cum[last:last + 1, :]
        q = q_ref[rows, :].astype(F32) * scale
        qin_ref[rows, :] = (q * jnp.exp2(b_cum)).astype(BF16)
        qrel_ref[rows, :] = (q * jnp.exp2(b_cum - b_last)).astype(BF16)
        krel_ref[rows, :] = (k_ref[rows, :].astype(F32) * jnp.exp2(b_last - b_cum)).astype(BF16)
        dec_ref[c] = jnp.exp2(b_last)

    def scores(c, h):
        rows, sl = chunk_rows[c], head_cols[h]
        att = lax.dot_general(qrel_ref[rows, sl], krel_ref[rows, sl],
                              (((1,), (1,)), ((), ())), preferred_element_type=F32)
        att_ref[c * N_HEADS + h] = jnp.where(mask, att, 0.0).astype(BF16)

    def local_output(c, h):
        rows, sl = chunk_rows[c], head_cols[h]
        v = v_ref[rows, sl]
        acc_ref[rows, sl] = jnp.dot(att_ref[c * N_HEADS + h], v, preferred_element_type=F32)
        kv_ref[c * N_HEADS + h] = lax.dot_general(
            v, krel_ref[rows, sl], (((0,), (0,)), ((), ())), preferred_element_type=F32)

    def carry(c, h):
        rows, sl = chunk_rows[c], head_cols[h]
        s_t = st_ref[h]
        o = acc_ref[rows, sl] + lax.dot_general(qin_ref[rows, sl], s_t.astype(BF16),
                                                (((1,), (1,)), ((), ())),
                                                preferred_element_type=F32)
        st_ref[h] = s_t * dec_ref[c][:, sl] + kv_ref[c * N_HEADS + h]
        if final:
            g = g_ref[rows, sl].astype(F32)
            o = _head_norm_gate(o + of_ref[rows, sl], gn_ref[:, sl], g * _sigmoid(g))
        out_ref[rows, sl] = o.astype(out_ref.dtype)

    _emit_skewed(list(reversed(range(n_chunks))) if reverse else list(range(n_chunks)),
                 [decayed_qk], [scores, local_output, carry])

    @pl.when(step == pl.num_programs(1) - 1)
    def _():
        sfin_ref[...] = st_ref[...]


def _scan_block_spec(t, cols, col_block, n_blocks, reverse):
    if reverse:
        return pl.BlockSpec((None, t, cols), lambda b, i: (b, n_blocks - 1 - i, col_block))
    return pl.BlockSpec((None, t, cols), lambda b, i: (b, i, col_block))


def _gla_scan(p3, ps3, wa_pad, ba, s0, gn, o_fwd, *, reverse, t):
    bsz, seq_len, _ = p3.shape
    n_blocks = seq_len // t
    n_chunks = t // CHUNK
    final = o_fwd is not None
    spec = functools.partial(_scan_block_spec, n_blocks=n_blocks, reverse=reverse)
    const2 = lambda shape: pl.BlockSpec(shape, lambda b, i: (0, 0))
    state_spec = pl.BlockSpec((None, N_HEADS, HEAD_DIM, HEAD_DIM), lambda b, i: (b, 0, 0, 0))
    in_specs = [
        spec(t, D_GROUP, C_GQ), spec(t, D_GROUP, C_GK), spec(t, D_GROUP, C_GV),
        spec(t, SMALL_COLS, 0),
        const2((3 * SMALL_COLS, D_GROUP)), const2((1, D_GROUP)),
        state_spec,
    ]
    args = [p3, p3, p3, ps3, wa_pad, ba, s0]
    if final:
        in_specs += [spec(t, D_GROUP, 0), spec(t, D_GROUP, C_GG), const2((1, D_GROUP))]
        args += [o_fwd, p3, gn]
    return pl.pallas_call(
        functools.partial(_gla_kernel, reverse=reverse, final=final, n_chunks=n_chunks),
        grid=(bsz, n_blocks),
        in_specs=in_specs,
        out_specs=[spec(t, D_GROUP, 0), state_spec],
        out_shape=[
            jax.ShapeDtypeStruct((bsz, seq_len, D_GROUP), BF16 if final else F32),
            jax.ShapeDtypeStruct((bsz, N_HEADS, HEAD_DIM, HEAD_DIM), F32),
        ],
        scratch_shapes=[
            pltpu.VMEM((N_HEADS, HEAD_DIM, HEAD_DIM), F32),
            pltpu.VMEM((t, D_GROUP), BF16), pltpu.VMEM((t, D_GROUP), BF16),
            pltpu.VMEM((t, D_GROUP), BF16),
            pltpu.VMEM((n_chunks, 1, D_GROUP), F32),
            pltpu.VMEM((n_chunks * N_HEADS, CHUNK, CHUNK), BF16),
            pltpu.VMEM((n_chunks * N_HEADS, HEAD_DIM, HEAD_DIM), F32),
        ] + ([pltpu.VMEM((t, D_GROUP), F32)] if final else []),
        compiler_params=_params(("parallel", "arbitrary")),
        name="gla_bwd" if reverse else "gla_fwd",
    )(*args)


def _mlstm_kernel(*refs, reverse, final, n_chunks, lane0):
    if final:
        (qkc_ref, v_ref, gt_ref, gb_ref, bm_ref, c0_ref, m0_ref, hf_ref, og_ref, gn_ref,
         out_ref, cfin_ref, mfin_ref, c_ref, m_ref, fcol_ref, jcol_ref, jt_ref,
         rowb_ref, ub_ref, mb_ref, p_ref, s_ref, wi_ref, kwt_ref, kv_ref, r_ref) = refs
    else:
        (qkc_ref, v_ref, gt_ref, gb_ref, bm_ref, c0_ref, m0_ref,
         out_ref, cfin_ref, mfin_ref, c_ref, m_ref, fcol_ref, jcol_ref, jt_ref,
         rowb_ref, ub_ref, mb_ref, p_ref, s_ref, wi_ref, kwt_ref, kv_ref, r_ref) = refs
    step = pl.program_id(1)

    @pl.when(step == 0)
    def _():
        c_ref[...] = c0_ref[...]
        m_ref[...] = m0_ref[...]

    mask = _tri(reverse)
    tri2 = _tri2(reverse)
    last = 0 if reverse else CHUNK - 1
    ones_blk = jnp.ones((CHUNK, HEAD_DIM), BF16)
    chunk_rows = [slice(c * CHUNK, (c + 1) * CHUNK) for c in range(n_chunks)]
    head_cols = [slice(h * HEAD_DIM, (h + 1) * HEAD_DIM) for h in range(N_HEADS)]
    key_cols = [slice(D_GROUP + h * HEAD_DIM, D_GROUP + (h + 1) * HEAD_DIM) for h in range(N_HEADS)]
    f_lane = [lane0 + N_HEADS + h for h in range(N_HEADS)]
    scan_order = list(reversed(range(n_chunks))) if reverse else list(range(n_chunks))

    gates = gt_ref[...] + gb_ref[...]
    log_f = _log_sigmoid(gates)
    i_al = pltpu.roll(gates, N_HEADS, axis=1)
    j_max, f_tot = [], []
    for c, rows in enumerate(chunk_rows):
        f_col = _chunk_cumsum(tri2, log_f[rows, :])
        j_col = i_al[rows, :] - f_col
        fcol_ref[rows, :] = f_col
        jcol_ref[rows, :] = j_col
        jt_ref[c] = j_col.T
        j_max.append(jnp.max(j_col, axis=0, keepdims=True))
        f_tot.append(f_col[last:last + 1, :])

    m_run = m_ref[0:1, :]
    m_prev, u_last = [None] * n_chunks, [None] * n_chunks
    for c in scan_order:
        m_prev[c] = m_run
        u_last[c] = jnp.maximum(m_run, j_max[c])
        m_run = f_tot[c] + u_last[c]
    m_ref[0:1, :] = m_run
    pad_rows = [jnp.zeros((ROW_PAD - n_chunks, LANES), F32)] if n_chunks < ROW_PAD else []
    rowb_ref[0] = _lane_spread(jnp.concatenate(m_prev + pad_rows, axis=0), bm_ref)
    rowb_ref[1] = _lane_spread(jnp.concatenate(u_last + pad_rows, axis=0), bm_ref)

    for c, rows in enumerate(chunk_rows):
        u_col = jnp.maximum(_cummax_rows(jcol_ref[rows, :], reverse), m_prev[c])
        ub_ref[rows, :] = _lane_spread(u_col, bm_ref)
        mb_ref[rows, :] = _lane_spread(fcol_ref[rows, :] + u_col, bm_ref)

    def decay_weights(c, h):
        rows, sl, lf, idx = chunk_rows[c], head_cols[h], f_lane[h], c * N_HEADS + h
        u = ub_ref[rows, sl]
        j_row = jt_ref[c][lf:lf + 1, :]
        a = jnp.where(mask, j_row, -jnp.inf)
        p_ref[idx] = jnp.exp(a - u[:, :CHUNK])
        wi_ref[idx] = jnp.exp(rowb_ref[0][c:c + 1, sl] - u)
        w_k = jnp.exp(j_row - rowb_ref[1][c:c + 1, sl][:, :CHUNK])
        kwt_ref[idx] = (qkc_ref[rows, key_cols[h]].astype(F32).T * w_k).astype(BF16)

    def scores(c, h):
        rows, idx = chunk_rows[c], c * N_HEADS + h
        s = lax.dot_general(qkc_ref[rows, head_cols[h]], qkc_ref[rows, key_cols[h]],
                            (((1,), (1,)), ((), ())), preferred_element_type=F32)
        s_ref[idx] = (s * p_ref[idx]).astype(BF16)

    def local_output(c, h):
        rows, idx = chunk_rows[c], c * N_HEADS + h
        v_aug = jnp.concatenate([v_ref[rows, head_cols[h]], ones_blk], axis=1)
        r_ref[rows, 2 * h * HEAD_DIM:2 * (h + 1) * HEAD_DIM] = jnp.dot(
            s_ref[idx], v_aug, preferred_element_type=F32)
        kv_ref[idx] = jnp.dot(kwt_ref[idx], v_aug, preferred_element_type=F32)

    def carry(c, h):
        rows, sl, idx = chunk_rows[c], head_cols[h], c * N_HEADS + h
        decay = jnp.exp(rowb_ref[0][c:c + 1, sl] - rowb_ref[1][c:c + 1, sl])
        c_aug = c_ref[h]
        wi = wi_ref[idx]
        r = jnp.concatenate([wi, wi], axis=1) * jnp.dot(
            qkc_ref[rows, sl], c_aug.astype(BF16),
            preferred_element_type=F32) + r_ref[rows, 2 * h * HEAD_DIM:2 * (h + 1) * HEAD_DIM]
        den = jnp.maximum(jnp.abs(r[:, HEAD_DIM:]), jnp.exp(-mb_ref[rows, sl]))
        hid = r[:, :HEAD_DIM] * (1.0 / den)
        c_ref[h] = c_aug * jnp.concatenate([decay, decay], axis=1) + kv_ref[idx]
        if final:
            hid = _head_norm_gate(hid + hf_ref[rows, sl], gn_ref[:, sl],
                                  _sigmoid(og_ref[rows, sl].astype(F32)))
        out_ref[rows, sl] = hid.astype(out_ref.dtype)

    for c in range(n_chunks):
        for h in range(N_HEADS):
            decay_weights(c, h)
    for c in range(n_chunks):
        for h in range(N_HEADS):
            scores(c, h)
    for c in range(n_chunks):
        for h in range(N_HEADS):
            local_output(c, h)
    for c in scan_order:
        for h in range(N_HEADS):
            carry(c, h)

    @pl.when(step == pl.num_programs(1) - 1)
    def _():
        cfin_ref[...] = c_ref[...]
        mfin_ref[...] = m_ref[...]


def _mlstm_scan(p3, ps3, gate_b, c0, m0, gn, h_fwd, *, reverse, t):
    bsz, seq_len, _ = p3.shape
    n_blocks = seq_len // t
    n_chunks = t // CHUNK
    final = h_fwd is not None
    spec = functools.partial(_scan_block_spec, n_blocks=n_blocks, reverse=reverse)
    const2 = lambda shape: pl.BlockSpec(shape, lambda b, i: (0, 0))
    c_spec = pl.BlockSpec((None, N_HEADS, HEAD_DIM, 2 * HEAD_DIM), lambda b, i: (b, 0, 0, 0))
    m_spec = pl.BlockSpec((None, 8, LANES), lambda b, i: (b, 0, 0))
    out_specs = [spec(t, D_GROUP, 0), c_spec, m_spec]
    out_shape = [
        jax.ShapeDtypeStruct((bsz, seq_len, D_GROUP), BF16 if final else F32),
        jax.ShapeDtypeStruct((bsz, N_HEADS, HEAD_DIM, 2 * HEAD_DIM), F32),
        jax.ShapeDtypeStruct((bsz, 8, LANES), F32),
    ]
    lane0 = GATE0 + (2 * N_HEADS if reverse else 0)
    lane = np.arange(2 * SMALL_COLS)[:, None] % SMALL_COLS
    head = np.arange(D_GROUP)[None, :] // HEAD_DIM
    spread = jnp.asarray(lane == lane0 + N_HEADS + head, BF16)
    in_specs = [
        spec(t, 2 * D_GROUP, C_MQK // 2), spec(t, D_GROUP, C_MV), spec(t, SMALL_COLS, 0),
        const2((1, SMALL_COLS)), const2((2 * SMALL_COLS, D_GROUP)), c_spec, m_spec,
    ]
    args = [p3, p3, ps3, gate_b, spread, c0, m0]
    if final:
        in_specs += [spec(t, D_GROUP, 0), spec(t, D_GROUP, C_MO), const2((1, D_GROUP))]
        args += [h_fwd, p3, gn]
    units = n_chunks * N_HEADS
    return pl.pallas_call(
        functools.partial(_mlstm_kernel, reverse=reverse, final=final, n_chunks=n_chunks,
                          lane0=lane0),
        grid=(bsz, n_blocks),
        in_specs=in_specs,
        out_specs=out_specs,
        out_shape=out_shape,
        scratch_shapes=[
            pltpu.VMEM((N_HEADS, HEAD_DIM, 2 * HEAD_DIM), F32),
            pltpu.VMEM((8, LANES), F32),
            pltpu.VMEM((t, SMALL_COLS), F32),
            pltpu.VMEM((t, SMALL_COLS), F32),
            pltpu.VMEM((n_chunks, SMALL_COLS, CHUNK), F32),
            pltpu.VMEM((2, ROW_PAD, D_GROUP), F32),
            pltpu.VMEM((t, D_GROUP), F32),
            pltpu.VMEM((t, D_GROUP), F32),
            pltpu.VMEM((units, CHUNK, CHUNK), F32),
            pltpu.VMEM((units, CHUNK, CHUNK), BF16),
            pltpu.VMEM((units, CHUNK, LANES), F32),
            pltpu.VMEM((units, HEAD_DIM, CHUNK), BF16),
            pltpu.VMEM((units, HEAD_DIM, 2 * HEAD_DIM), F32),
            pltpu.VMEM((t, 2 * D_GROUP), F32),
        ],
        compiler_params=_params(("parallel", "arbitrary")),
        name="mlstm_bwd" if reverse else "mlstm_fwd",
    )(*args)


def _outproj_kernel(y0_ref, y1_ref, y2_ref, y3_ref, w_ref, x_ref, g_ref, o_ref):
    acc = jnp.dot(y0_ref[...], w_ref[0:D_GROUP, :], preferred_element_type=F32)
    for n, y_ref in enumerate((y1_ref, y2_ref, y3_ref), start=1):
        acc = acc + jnp.dot(y_ref[...], w_ref[n * D_GROUP:(n + 1) * D_GROUP, :],
                            preferred_element_type=F32)
    o_ref[...] = x_ref[...] + g_ref[...] * acc


def _out_projection(ys, w_out, layer, x2, mod, seq_len, tm):
    m, d = x2.shape
    y_spec = pl.BlockSpec((tm, D_GROUP), lambda i: (i, 0))
    return pl.pallas_call(
        _outproj_kernel,
        grid=(m // tm,),
        in_specs=[y_spec] * N_MIXERS + [
            pl.BlockSpec((None, d, d), lambda i: (layer, 0, 0), pipeline_mode=pl.Buffered(1)),
            pl.BlockSpec((tm, d), lambda i: (i, 0)),
            _row_spec(2, seq_len, tm, d),
        ],
        out_specs=pl.BlockSpec((tm, d), lambda i: (i, 0)),
        out_shape=jax.ShapeDtypeStruct((m, d), F32),
        compiler_params=_params(("parallel",)),
        name="out_projection",
    )(*ys, w_out, x2, mod)


def _ffn_kernel(*refs, period, final):
    if final:
        (x_ref, g_ref, sh_ref, sc_ref, gate_ref, wa_ref, wv_ref, cw_ref, wd_ref, fn_ref,
         o_ref, hn_ref) = refs
    else:
        (x_ref, g_ref, sh_ref, sc_ref, gate_ref, wa_ref, wv_ref, cw_ref, wd_ref,
         o_ref, hn_ref) = refs
    j = pl.program_id(1)
    last = pl.num_programs(1) - 1

    @pl.when(j == 0)
    def _():
        hn_ref[...] = _modulated_norm(x_ref[...], g_ref[...], sc_ref[...], sh_ref[...]).astype(BF16)

    def tile_contribution():
        hn = hn_ref[...]
        a = _dwconv3(jnp.dot(hn, wa_ref[...], preferred_element_type=F32), cw_ref, period)
        v = jnp.dot(hn, wv_ref[...], preferred_element_type=F32)
        hid = (a * _sigmoid(a) * v).astype(BF16)
        return jnp.dot(hid, wd_ref[...], preferred_element_type=F32)

    @pl.when(j == 0)
    def _():
        o_ref[...] = tile_contribution()

    @pl.when(jnp.logical_and(j > 0, j < last))
    def _():
        o_ref[...] += tile_contribution()

    @pl.when(j == last)
    def _():
        y = x_ref[...] + gate_ref[...] * (o_ref[...] + tile_contribution())
        if final:
            y = y * lax.rsqrt(jnp.mean(y * y, axis=-1, keepdims=True) + EPS) * fn_ref[...]
        o_ref[...] = y


def _conv_ffn(x2, gain, mod, w_up, w_conv, w_down, layer, final_gain, seq_len, period, tm,
              tf=512):
    m, d = x2.shape
    d_ff = w_down.shape[1]
    n_f = d_ff // tf
    final = final_gain is not None
    in_specs = [
        pl.BlockSpec((tm, d), lambda i, j: (i, 0)),
        pl.BlockSpec((1, d), lambda i, j: (0, 0)),
        _row_spec(3, seq_len, tm, d), _row_spec(4, seq_len, tm, d), _row_spec(5, seq_len, tm, d),
        pl.BlockSpec((None, d, tf), lambda i, j: (layer, 0, j)),
        pl.BlockSpec((None, d, tf), lambda i, j: (layer, 0, j + n_f)),
        pl.BlockSpec((None, 3, tf), lambda i, j: (layer, 0, j)),
        pl.BlockSpec((None, tf, d), lambda i, j: (layer, j, 0)),
    ]
    args = [x2, gain.reshape(1, d), mod, mod, mod, w_up, w_up, w_conv, w_down]
    if final:
        in_specs.append(pl.BlockSpec((1, d), lambda i, j: (0, 0)))
        args.append(final_gain.reshape(1, d))
    return pl.pallas_call(
        functools.partial(_ffn_kernel, period=period, final=final),
        grid=(m // tm, n_f),
        in_specs=in_specs,
        out_specs=pl.BlockSpec((tm, d), lambda i, j: (i, 0)),
        out_shape=jax.ShapeDtypeStruct((m, d), F32),
        scratch_shapes=[pltpu.VMEM((tm, d), BF16)],
        compiler_params=pltpu.CompilerParams(dimension_semantics=("parallel", "arbitrary"),
                                             vmem_limit_bytes=FFN_VMEM_LIMIT),
        name="conv_ffn",
    )(*args)


def _relayout_w_in(w_in):
    depth, d, _ = w_in.shape
    gla = 4 * D_GROUP + 2 * GLA_LOWRANK
    ml = 4 * D_GROUP + 4 * N_HEADS
    wide = jnp.concatenate([w_in[..., :4 * D_GROUP], w_in[..., gla:gla + 4 * D_GROUP],
                            w_in[..., gla + ml:]], axis=-1).astype(BF16)
    small = jnp.concatenate([
        w_in[..., 4 * D_GROUP:gla], w_in[..., gla + 4 * D_GROUP:gla + ml],
        jnp.zeros((depth, d, SMALL_COLS - 2 * GLA_LOWRANK - 4 * N_HEADS), w_in.dtype)],
        axis=-1).astype(BF16)
    return wide, small


def _scan_mixers(p2, ps2, bsz, seq_len, prm, states, *, t_scan):
    p3 = p2.reshape(bsz, seq_len, -1)
    ps3 = ps2.reshape(bsz, seq_len, SMALL_COLS)
    (gs_f, gs_b), (mc_f, mm_f, mc_b, mm_b) = states
    o_f, gs_f = _gla_scan(p3, ps3, prm["wa_f"], prm["ba_f"], gs_f, None, None,
                          reverse=False, t=t_scan)
    y_gla, gs_b = _gla_scan(p3, ps3, prm["wa_b"], prm["ba_b"], gs_b, prm["gla_norm"], o_f,
                            reverse=True, t=t_scan)
    h_f, mc_f, mm_f = _mlstm_scan(p3, ps3, prm["ml_gate_b"], mc_f, mm_f, None, None,
                                  reverse=False, t=t_scan)
    y_ml, mc_b, mm_b = _mlstm_scan(p3, ps3, prm["ml_gate_b"], mc_b, mm_b, prm["ml_norm"], h_f,
                                   reverse=True, t=t_scan)
    new_states = ((gs_f, gs_b), (mc_f, mm_f, mc_b, mm_b))
    m = bsz * seq_len
    return (y_gla.reshape(m, D_GROUP), y_ml.reshape(m, D_GROUP)), new_states


def kernel(x, c, ctx, c_ctx, w_mod, b_mod, norm1, norm2, w_in, w_out, gla_wa2, gla_ba, gla_norm,
           mlstm_conv, mlstm_gate_b, mlstm_norm, sc_conv, cm_ws, cm_bs, cm_norm,
           ffn_up, ffn_conv, ffn_down, final_norm):
    bsz, seq_len, d = x.shape
    ctx_len = ctx.shape[1]
    depth = w_mod.shape[0]
    assert seq_len % T_SCAN == 0 and ctx_len % CM_CHUNK == 0 and bsz + 1 <= 8

    cond = jnp.zeros((8, d), F32).at[:bsz].set(c).at[bsz].set(c_ctx)
    mod = _modulation(cond, w_mod, b_mod).reshape(depth, 8, 6, d).transpose(0, 2, 1, 3)

    xl = x.reshape(bsz * seq_len, d)
    xc = ctx.reshape(bsz * ctx_len, d)
    tm_l, tm_c = 512, bsz * ctx_len
    tm_in = 512
    tm_ffn = 1024

    zero_states = (
        (jnp.zeros((bsz, N_HEADS, HEAD_DIM, HEAD_DIM), F32),) * 2,
        (jnp.zeros((bsz, N_HEADS, HEAD_DIM, 2 * HEAD_DIM), F32), jnp.zeros((bsz, 8, LANES), F32)) * 2,
    )

    w_wide_all, w_small_all = _relayout_w_in(w_in)
    w_out_b, w_up_b, w_down_b = w_out.astype(BF16), ffn_up.astype(BF16), ffn_down.astype(BF16)
    k_scale = jnp.concatenate([jnp.ones((1, D_GROUP), F32),
                               jnp.full((1, D_GROUP), HEAD_DIM ** -0.5, F32)], axis=1)
    for l in range(depth):
        need_ctx = l < depth - 1
        mod_l = mod[l][:, :bsz, None, :]
        mod_c = jnp.broadcast_to(mod[l][:, bsz:bsz + 1, None, :], (6, bsz, 1, d))
        wa = jnp.zeros((2, SMALL_COLS, D_GROUP), F32)
        wa = wa.at[0, Z_FWD:Z_FWD + GLA_LOWRANK].set(gla_wa2[l, 0])
        wa = wa.at[1, Z_BWD:Z_BWD + GLA_LOWRANK].set(gla_wa2[l, 1])
        wa_hi = wa.astype(BF16)
        wa_lo = (wa - wa_hi.astype(F32)).astype(BF16)
        wa = jnp.concatenate([wa_hi, wa_hi, wa_lo], axis=1)
        gate_b = jnp.zeros((1, SMALL_COLS), F32).at[0, GATE0:GATE0 + 4 * N_HEADS].set(
            mlstm_gate_b[l].reshape(-1))
        prm = dict(
            wa_f=wa[0], wa_b=wa[1], ba_f=gla_ba[l, 0:1], ba_b=gla_ba[l, 1:2],
            gla_norm=gla_norm[l].reshape(1, D_GROUP),
            ml_conv=mlstm_conv[l] * k_scale, ml_gate_b=gate_b,
            ml_norm=mlstm_norm[l].reshape(1, D_GROUP),
            sc_conv=sc_conv[l], cm_ws=cm_ws[l].astype(BF16),
            cm_bs=jnp.repeat(cm_bs[l].T, HEAD_DIM, axis=1),
            cm_norm=cm_norm[l].reshape(1, D_GROUP),
        )

        pc, psc, *yc_local = _in_projection(xc, norm1[l], mod_c, w_wide_all, w_small_all, l, prm,
                                            ctx_len, tm_c, ctx_len)
        yc_scan, ctx_states = _scan_mixers(pc, psc, bsz, ctx_len, prm, zero_states,
                                           t_scan=ctx_len)
        p_l, ps_l, *yl_local = _in_projection(xl, norm1[l], mod_l, w_wide_all, w_small_all, l, prm,
                                              seq_len, tm_in, GRID_W)
        yl_scan, _ = _scan_mixers(p_l, ps_l, bsz, seq_len, prm, ctx_states,
                                  t_scan=T_SCAN)
        yl, yc = yl_scan + tuple(yl_local), yc_scan + tuple(yc_local)

        xl = _out_projection(yl, w_out_b, l, xl, mod_l, seq_len, tm_ffn)
        xl = _conv_ffn(xl, norm2[l], mod_l, w_up_b, ffn_conv, w_down_b, l,
                       final_norm if l == depth - 1 else None, seq_len, GRID_W, tm_ffn)
        if need_ctx:
            xc = _out_projection(yc, w_out_b, l, xc, mod_c, ctx_len, tm_c)
            xc = _conv_ffn(xc, norm2[l], mod_c, w_up_b, ffn_conv, w_down_b, l, None,
                           ctx_len, ctx_len, tm_c)
    return xl.reshape(bsz, seq_len, d)
```
